```python
import jax
import jax.numpy as jnp
from jax import lax
import numpy as np

D_MODEL = 1024
BATCH = 8
SEQ = 4096
DEPTH = 4

CTX_LEN = 256
GRID_W = 64
HEAD_DIM = 64
NORM_EPS = 1e-6
NA_HEADS = 4
NA_KH = 8
NA_KW = 16
NA_WIDTH = NA_HEADS * HEAD_DIM
GLA_HEADS = 4
GLA_DK = 128
GLA_DV = 256
GLA_LOWRANK = 16
GLA_TAU = 16.0
GLA_CHUNK = 64
GQA_HEADS = 4
GQA_KV_HEADS = 2
GQA_WIDTH = GQA_HEADS * HEAD_DIM
GQA_KV_WIDTH = GQA_KV_HEADS * HEAD_DIM
Q_BLOCK = 128
ROPE_BASE = 10000.0
N_EXPERTS = 16
EXPERT_FF = 2048
EC_CAPACITY_FACTOR = 2
IN_SPLITS = (NA_WIDTH, NA_WIDTH, NA_WIDTH,
             GLA_DK, GLA_DK, GLA_DV, GLA_DV, GLA_LOWRANK, GLA_LOWRANK,
             GQA_WIDTH, GQA_KV_WIDTH, GQA_KV_WIDTH,
             D_MODEL, D_MODEL, D_MODEL)
IN_COLS = 5152

kernel_name = "hybrid_na_gla_gqa_ec_moe_diffusion"


def rmsnorm(x, g):
    xf = x.astype(jnp.float32)
    y = xf * lax.rsqrt(jnp.mean(xf * xf, axis=-1, keepdims=True) + NORM_EPS)
    return y.astype(x.dtype) * g


def modulate(x, g, shift, scale):
    return rmsnorm(x, g) * (1 + scale) + shift


def adaln(cv, w, b):
    mod = (jax.nn.silu(cv) @ w + b)[:, None, :]
    return jnp.split(mod, 6, axis=-1)


def split_heads(t, n):
    return t.reshape(t.shape[:-1] + (n, t.shape[-1] // n))


def split_projection(p):
    offs = np.cumsum(IN_SPLITS)[:-1].tolist()
    return jnp.split(p, offs, axis=-1)


def rope_1d(x, pos):
    half = x.shape[-1] // 2
    inv_freq = ROPE_BASE ** (-jnp.arange(half, dtype=jnp.float32) / half)
    ang = pos.astype(jnp.float32)[:, None] * inv_freq[None, :]
    cos = jnp.cos(ang)[:, None, :]
    sin = jnp.sin(ang)[:, None, :]
    x1 = x[..., :half].astype(jnp.float32)
    x2 = x[..., half:].astype(jnp.float32)
    return jnp.concatenate([x1 * cos - x2 * sin, x2 * cos + x1 * sin], axis=-1).astype(x.dtype)


def axial_rope(x, row, col):
    half = x.shape[-1] // 2
    return jnp.concatenate([rope_1d(x[..., :half], row), rope_1d(x[..., half:], col)], axis=-1)


def attend(q, k, v):
    s = jnp.einsum('btkgd,bskd->bkgts', q, k) * (q.shape[-1] ** -0.5)
    p = jax.nn.softmax(s.astype(jnp.float32), axis=-1).astype(v.dtype)
    return jnp.einsum('bkgts,bskd->btkgd', p, v)


def attend_blocked(q, k, v):
    B, T = q.shape[:2]
    nb = T // Q_BLOCK
    qb = jnp.moveaxis(q.reshape((B, nb, Q_BLOCK) + q.shape[2:]), 1, 0)
    o = lax.map(lambda qi: attend(qi, k, v), qb)
    return jnp.moveaxis(o, 0, 1).reshape(q.shape)


def neighbourhood_attention(q, k, v, k_ctx, v_ctx, rpb):
    B, T, H, D = q.shape
    rows = T // GRID_W
    kh = min(NA_KH, rows)
    kw = NA_KW
    scale = D ** -0.5
    qg = q.reshape(B, rows, GRID_W, H, D)
    kg = k.reshape(B, rows, GRID_W, H, D)
    vg = v.reshape(B, rows, GRID_W, H, D)
    col = jnp.arange(GRID_W)
    col_start = jnp.clip(col - kw // 2, 0, GRID_W - kw)
    col_idx = col_start[:, None] + jnp.arange(kw)[None, :]
    dc = col_idx - col[:, None] + (NA_KW - 1)

    def row_block(args):
        q_row, r = args
        r_start = jnp.clip(r - kh // 2, 0, rows - kh)
        k_rows = lax.dynamic_slice_in_dim(kg, r_start, kh, axis=1)
        v_rows = lax.dynamic_slice_in_dim(vg, r_start, kh, axis=1)
        k_win = k_rows[:, :, col_idx]
        v_win = v_rows[:, :, col_idx]
        dr = r_start + jnp.arange(kh) - r + (NA_KH - 1)
        bias = rpb[:, dr[:, None, None], dc[None, :, :]]
        bias = jnp.transpose(bias, (0, 2, 1, 3)).astype(jnp.float32)
        s_loc = jnp.einsum('bqhd,baqkhd->bhqak', q_row, k_win).astype(jnp.float32) * scale + bias[None]
        s_ctx = jnp.einsum('bqhd,bshd->bhqs', q_row, k_ctx).astype(jnp.float32) * scale
        s = jnp.concatenate([s_loc.reshape(B, H, GRID_W, kh * kw), s_ctx], axis=-1)
        p = jax.nn.softmax(s, axis=-1).astype(v.dtype)
        p_loc = p[..., :kh * kw].reshape(B, H, GRID_W, kh, kw)
        p_ctx = p[..., kh * kw:]
        return (jnp.einsum('bhqak,baqkhd->bqhd', p_loc, v_win)
                + jnp.einsum('bhqs,bshd->bqhd', p_ctx, v_ctx))

    out = lax.map(row_block, (jnp.moveaxis(qg, 1, 0), jnp.arange(rows)))
    return jnp.moveaxis(out, 0, 1).reshape(B, T, H, D)


def gla_log_gate(lr, w, b):
    g = jax.nn.log_sigmoid((lr @ w + b).astype(jnp.float32)) / GLA_TAU
    return split_heads(g, GLA_HEADS)


def gla_chunk_states(k, v, g, s0):
    b = jnp.cumsum(g, axis=2)
    b_last = b[:, :, -1]
    k_end = k * jnp.exp(b_last[:, :, None] - b)
    d_state = jnp.einsum('bnjhk,bnjhv->bnhkv', k_end, v)

    def step(s, inp):
        decay, ds = inp
        return decay[..., None] * s + ds, s

    s_final, s_in = lax.scan(step, s0, (jnp.moveaxis(jnp.exp(b_last), 1, 0), jnp.moveaxis(d_state, 1, 0)))
    return b, jnp.moveaxis(s_in, 0, 1), s_final


def _to_chunks(t):
    B, T, H, d = t.shape
    return t.reshape(B, T // GLA_CHUNK, GLA_CHUNK, H, d)


def gla_scan(q, k, v, g, s0):
    B, T, H, dk = q.shape
    qn, kn, vn, gn = _to_chunks(q), _to_chunks(k), _to_chunks(v), _to_chunks(g)
    b, s_in, s_final = gla_chunk_states(kn, vn, gn, s0)
    q_dec = qn * (dk ** -0.5) * jnp.exp(b)
    k_inv = kn * jnp.exp(-b)
    tri = jnp.tril(jnp.ones((GLA_CHUNK, GLA_CHUNK), dtype=bool))
    a = jnp.where(tri, jnp.einsum('bnihk,bnjhk->bnhij', q_dec, k_inv), 0.0)
    o = (jnp.einsum('bnhij,bnjhv->bnihv', a, vn)
         + jnp.einsum('bnihk,bnhkv->bnihv', q_dec, s_in))
    return o.reshape(B, T, H, v.shape[-1]).astype(v.dtype), s_final


def gla_final_state(k, v, g, s0):
    _, _, s_final = gla_chunk_states(_to_chunks(k), _to_chunks(v), _to_chunks(g), s0)
    return s_final


def gla_output(o, r, g):
    B, T = o.shape[:2]
    return rmsnorm(o, g).reshape(B, T, GLA_DV) * jax.nn.silu(r)


def rev(t):
    return jnp.flip(t, axis=1)


def merge_branches(ya, yb, yc, mga, mgb, mgc, wb_a, wb_b, wb_c, w_out):
    m = (jax.nn.sigmoid(mga) * (ya @ wb_a)
         + jax.nn.sigmoid(mgb) * (yb @ wb_b)
         + jax.nn.sigmoid(mgc) * (yc @ wb_c))
    return m @ w_out


def hybrid_mixer(u, uc, w_in, rpb, wg_f, bg_f, wg_b, bg_b, gla_g, qn_g, kn_g,
                 wb_a, wb_b, wb_c, w_out, row, col, with_ctx_out):
    B, T, _ = u.shape
    S = uc.shape[1]
    (qa, ka, va, qb, kb, vb, rb, lf, lb, qc, kc, vc, mga, mgb, mgc) = split_projection(u @ w_in)
    (qa_c, ka_c, va_c, qb_c, kb_c, vb_c, rb_c, lf_c, lb_c,
     qc_c, kc_c, vc_c, mga_c, mgb_c, mgc_c) = split_projection(uc @ w_in)

    ka_ch, va_ch = split_heads(ka_c, NA_HEADS), split_heads(va_c, NA_HEADS)
    ya = neighbourhood_attention(split_heads(qa, NA_HEADS), split_heads(ka, NA_HEADS),
                                 split_heads(va, NA_HEADS), ka_ch, va_ch, rpb).reshape(B, T, NA_WIDTH)

    kb_ch, vb_ch = split_heads(kb_c, GLA_HEADS), split_heads(vb_c, GLA_HEADS)
    gf_c, gb_c = gla_log_gate(lf_c, wg_f, bg_f), gla_log_gate(lb_c, wg_b, bg_b)
    s0 = jnp.zeros((B, GLA_HEADS, GLA_DK // GLA_HEADS, GLA_DV // GLA_HEADS), jnp.float32)
    if with_ctx_out:
        qb_ch = split_heads(qb_c, GLA_HEADS)
        oc_f, s_f = gla_scan(qb_ch, kb_ch, vb_ch, gf_c, s0)
        oc_b, s_b = gla_scan(rev(qb_ch), rev(kb_ch), rev(vb_ch), rev(gb_c), s0)
        yb_c = gla_output(oc_f + rev(oc_b), rb_c, gla_g)
    else:
        s_f = gla_final_state(kb_ch, vb_ch, gf_c, s0)
        s_b = gla_final_state(rev(kb_ch), rev(vb_ch), rev(gb_c), s0)
    qb_h, kb_h, vb_h = split_heads(qb, GLA_HEADS), split_heads(kb, GLA_HEADS), split_heads(vb, GLA_HEADS)
    gf, gbk = gla_log_gate(lf, wg_f, bg_f), gla_log_gate(lb, wg_b, bg_b)
    o_f, _ = gla_scan(qb_h, kb_h, vb_h, gf, s_f)
    o_b, _ = gla_scan(rev(qb_h), rev(kb_h), rev(vb_h), rev(gbk), s_b)
    yb = gla_output(o_f + rev(o_b), rb, gla_g)

    grp = GQA_HEADS // GQA_KV_HEADS
    q_h = axial_rope(rmsnorm(split_heads(qc, GQA_HEADS), qn_g), row, col)
    k_h = axial_rope(rmsnorm(split_heads(kc, GQA_KV_HEADS), kn_g), row, col)
    k_ch = rmsnorm(split_heads(kc_c, GQA_KV_HEADS), kn_g)
    v_ch = split_heads(vc_c, GQA_KV_HEADS)
    k_all = jnp.concatenate([k_ch, k_h], axis=1)
    v_all = jnp.concatenate([v_ch, split_heads(vc, GQA_KV_HEADS)], axis=1)
    yc = attend_blocked(q_h.reshape(B, T, GQA_KV_HEADS, grp, HEAD_DIM), k_all, v_all).reshape(B, T, GQA_WIDTH)

    m_lat = merge_branches(ya, yb, yc, mga, mgb, mgc, wb_a, wb_b, wb_c, w_out)
    if not with_ctx_out:
        return m_lat, None

    ya_c = attend(split_heads(qa_c, NA_HEADS)[:, :, :, None, :], ka_ch, va_ch).reshape(B, S, NA_WIDTH)
    q_ch = rmsnorm(split_heads(qc_c, GQA_HEADS), qn_g).reshape(B, S, GQA_KV_HEADS, grp, HEAD_DIM)
    yc_c = attend(q_ch, k_ch, v_ch).reshape(B, S, GQA_WIDTH)
    m_ctx = merge_branches(ya_c, yb_c, yc_c, mga_c, mgb_c, mgc_c, wb_a, wb_b, wb_c, w_out)
    return m_lat, m_ctx


def expert_choice_ffn(x, w_router, w_gate, w_up, w_down):
    B, T, Dm = x.shape
    cap = EC_CAPACITY_FACTOR * T // N_EXPERTS
    aff = jax.nn.softmax((x @ w_router).astype(jnp.float32), axis=-1)
    vals, idx = lax.top_k(jnp.swapaxes(aff, 1, 2), cap)
    xs = jax.vmap(lambda xb, ib: xb[ib])(x, idx)
    h = jax.nn.silu(jnp.einsum('becd,edf->becf', xs, w_gate)) * jnp.einsum('becd,edf->becf', xs, w_up)
    y = jnp.einsum('becf,efd->becd', h, w_down) * vals[..., None].astype(x.dtype)

    def scatter_one(yb, ib):
        return jnp.zeros((T, Dm), yb.dtype).at[ib.reshape(-1)].add(yb.reshape(-1, Dm))

    return jax.vmap(scatter_one)(y, idx)


def _normal(k, shape, scale):
    return jax.random.normal(k, shape, jnp.float32) * scale


def setup_inputs(seed: int = 0) -> dict:
    key = jax.random.key(seed)
    k = jax.random.split(key, 26)
    L, D = DEPTH, D_MODEL
    return {
        "x": _normal(k[0], (BATCH, SEQ, D), 1.0),
        "c": _normal(k[1], (BATCH, D), 1.0),
        "ctx": _normal(k[2], (BATCH, CTX_LEN, D), 1.0),
        "c_ctx": _normal(k[3], (D,), 1.0),
        "w_ada": _normal(k[4], (L, D, 6 * D), 0.5 * D ** -0.5),
        "b_ada": _normal(k[5], (L, 6 * D), 0.01),
        "norm1_g": 1.0 + _normal(k[6], (L, D), 0.02),
        "norm2_g": 1.0 + _normal(k[7], (L, D), 0.02),
        "w_in": _normal(k[8], (L, D, IN_COLS), D ** -0.5),
        "na_rpb": _normal(k[9], (L, NA_HEADS, 2 * NA_KH - 1, 2 * NA_KW - 1), 0.1),
        "gla_wg_f": _normal(k[10], (L, GLA_LOWRANK, GLA_DK), GLA_LOWRANK ** -0.5),
        "gla_bg_f": _normal(k[11], (L, GLA_DK), 0.1),
        "gla_wg_b": _normal(k[12], (L, GLA_LOWRANK, GLA_DK), GLA_LOWRANK ** -0.5),
        "gla_bg_b": _normal(k[13], (L, GLA_DK), 0.1),
        "gla_norm_g": 1.0 + _normal(k[14], (L, GLA_DV // GLA_HEADS), 0.02),
        "gqa_qn_g": 1.0 + _normal(k[15], (L, HEAD_DIM), 0.02),
        "gqa_kn_g": 1.0 + _normal(k[16], (L, HEAD_DIM), 0.02),
        "w_branch_a": _normal(k[17], (L, NA_WIDTH, D), NA_WIDTH ** -0.5),
        "w_branch_b": _normal(k[18], (L, GLA_DV, D), GLA_DV ** -0.5),
        "w_branch_c": _normal(k[19], (L, GQA_WIDTH, D), GQA_WIDTH ** -0.5),
        "w_out": _normal(k[20], (L, D, D), D ** -0.5),
        "w_router": _normal(k[21], (L, D, N_EXPERTS), D ** -0.5),
        "w_e_gate": _normal(k[22], (L, N_EXPERTS, D, EXPERT_FF), D ** -0.5),
        "w_e_up": _normal(k[23], (L, N_EXPERTS, D, EXPERT_FF), D ** -0.5),
        "w_e_down": _normal(k[24], (L, N_EXPERTS, EXPERT_FF, D), EXPERT_FF ** -0.5),
        "final_norm_g": 1.0 + _normal(k[25], (D,), 0.02),
    }


def reference(x, c, ctx, c_ctx, w_ada, b_ada, norm1_g, norm2_g, w_in, na_rpb,
              gla_wg_f, gla_bg_f, gla_wg_b, gla_bg_b, gla_norm_g, gqa_qn_g, gqa_kn_g,
              w_branch_a, w_branch_b, w_branch_c, w_out, w_router, w_e_gate, w_e_up,
              w_e_down, final_norm_g):
    T = x.shape[1]
    t = jnp.arange(T)
    row = t // GRID_W
    col = t % GRID_W
    xc = ctx
    for l in range(DEPTH):
        last = l == DEPTH - 1
        sh1, sc1, g1, sh2, sc2, g2 = adaln(c, w_ada[l], b_ada[l])
        sh1c, sc1c, g1c, sh2c, sc2c, g2c = adaln(c_ctx[None], w_ada[l], b_ada[l])
        u = modulate(x, norm1_g[l], sh1, sc1)
        uc = modulate(xc, norm1_g[l], sh1c, sc1c)
        m, mc = hybrid_mixer(u, uc, w_in[l], na_rpb[l], gla_wg_f[l], gla_bg_f[l], gla_wg_b[l],
                             gla_bg_b[l], gla_norm_g[l], gqa_qn_g[l], gqa_kn_g[l],
                             w_branch_a[l], w_branch_b[l], w_branch_c[l], w_out[l],
                             row, col, not last)
        x = x + g1 * m
        x = x + g2 * expert_choice_ffn(modulate(x, norm2_g[l], sh2, sc2),
                                       w_router[l], w_e_gate[l], w_e_up[l], w_e_down[l])
        if not last:
            xc = xc + g1c * mc
            xc = xc + g2c * expert_choice_ffn(modulate(xc, norm2_g[l], sh2c, sc2c),
                                              w_router[l], w_e_gate[l], w_e_up[l], w_e_down[l])
    return rmsnorm(x, final_norm_g)
```

```python
import functools

import jax
import jax.numpy as jnp
import numpy as np
from jax import lax
from jax.experimental import pallas as pl
from jax.experimental.pallas import tpu as pltpu

F32 = jnp.float32
BF16 = jnp.bfloat16
I32 = jnp.int32

GRID_W = 64
HEAD_DIM = 64
NORM_EPS = 1e-6
NA_HEADS = 4
NA_KH = 8
NA_KW = 16
GLA_HEADS = 4
GLA_DK = 128
GLA_DV = 256
GLA_LOWRANK = 16
GLA_TAU = 16.0
GLA_CHUNK = 64
GQA_HEADS = 4
GQA_KV_HEADS = 2
ROPE_BASE = 10000.0
N_EXPERTS = 16
EC_CAPACITY_FACTOR = 2
NEG_BIG = -1e30

LANES = 128
HW = 256
VMEM_LIMIT = 56 * 1024 * 1024

COL_A = 0
COL_B = 768
COL_C = 1536
COL_G = 2304
COL_LR = 5376
COL_END = 5504


def _cparams(n_grid):
    return pltpu.CompilerParams(dimension_semantics=("arbitrary",) * n_grid,
                                vmem_limit_bytes=VMEM_LIMIT)


def _resident(block_shape, index_map):
    return pl.BlockSpec(block_shape, index_map, pipeline_mode=pl.Buffered(1))


def _dot(a, b):
    return jnp.dot(a, b, preferred_element_type=F32)


def _dot_nt(a, b):
    return lax.dot_general(a, b, (((1,), (1,)), ((), ())), preferred_element_type=F32)


def _split2(a):
    hi = a.astype(BF16)
    lo = (a - hi.astype(F32)).astype(BF16)
    return hi, lo


def _split3(a):
    h1 = a.astype(BF16)
    r1 = a - h1.astype(F32)
    h2 = r1.astype(BF16)
    h3 = (r1 - h2.astype(F32)).astype(BF16)
    return h1, h2, h3


def _seg_meansq(x, seg_ref):
    hi, lo = _split2(x * x)
    seg = seg_ref[...]
    return (_dot(hi, seg) + _dot(lo, seg)) * (1.0 / HEAD_DIM)


def _silu(x):
    return x * jax.nn.sigmoid(x)


def _adaln_kernel(cv_ref, w_ref, b_ref, o_ref):
    s = _silu(cv_ref[...])
    w = w_ref[0]
    s1, s2, s3 = _split3(s)
    w1, w2, w3 = _split3(w)
    acc = _dot(s1, w1) + (_dot(s1, w2) + _dot(s2, w1)) + (_dot(s2, w2) + _dot(s1, w3) + _dot(s3, w1))
    o_ref[0] = acc + b_ref[0]


def _adaln(cv, w_ada, b_ada):
    L, D, D6 = w_ada.shape
    R = cv.shape[0]
    tn = 1024
    return pl.pallas_call(
        _adaln_kernel,
        grid=(L, D6 // tn),
        in_specs=[pl.BlockSpec((R, D), lambda l, n: (0, 0)),
                  pl.BlockSpec((1, D, tn), lambda l, n: (l, 0, n)),
                  pl.BlockSpec((1, 1, tn), lambda l, n: (l, 0, n))],
        out_specs=pl.BlockSpec((1, R, tn), lambda l, n: (l, 0, n)),
        out_shape=jax.ShapeDtypeStruct((L, R, D6), F32),
        compiler_params=_cparams(2),
        name="adaln",
    )(cv, w_ada, b_ada.reshape(L, 1, D6))


def _swap16(y):
    lane = lax.broadcasted_iota(I32, y.shape, 1)
    first = (lane % 32) < 16
    return jnp.where(first, pltpu.roll(y, LANES - 16, 1), pltpu.roll(y, 16, 1))


def _norm_rope(x, seg_ref, g_ref, cos_ref, sin_ref, scale):
    y = x * lax.rsqrt(_seg_meansq(x, seg_ref) + NORM_EPS) * g_ref[...]
    halves = []
    for i in range(HW // LANES):
        sl = slice(i * LANES, (i + 1) * LANES)
        yh = y[:, sl]
        halves.append(yh * cos_ref[:, sl] + _swap16(yh) * sin_ref[:, sl])
    out = jnp.concatenate(halves, axis=1)
    return out * scale if scale != 1.0 else out


def _proj_in_kernel(x_ref, g_ref, sc_ref, sh_ref, w_ref, seg_ref, qn_ref, kn_ref, cos_ref, sin_ref,
                    a_ref, b_ref, c_ref, gt_ref, lr_ref):
    x = x_ref[...]
    y = x * lax.rsqrt(jnp.mean(x * x, axis=-1, keepdims=True) + NORM_EPS)
    u = (y * g_ref[...] * (1.0 + sc_ref[0]) + sh_ref[0]).astype(BF16)

    def proj(c0, c1):
        return _dot(u, w_ref[:, c0:c1])

    qscale = HEAD_DIM ** -0.5
    a_ref[:, 0:HW] = (proj(COL_A, COL_A + HW) * qscale).astype(BF16)
    a_ref[:, HW:3 * HW] = proj(COL_A + HW, COL_B).astype(BF16)
    b_ref[...] = proj(COL_B, COL_C).astype(BF16)
    qc = proj(COL_C, COL_C + HW)
    c_ref[:, 0:HW] = _norm_rope(qc, seg_ref, qn_ref, cos_ref, sin_ref, qscale).astype(BF16)
    kc = proj(COL_C + HW, COL_C + 2 * HW)
    c_ref[:, HW:2 * HW] = _norm_rope(kc, seg_ref, kn_ref, cos_ref, sin_ref, 1.0).astype(BF16)
    c_ref[:, 2 * HW:3 * HW] = proj(COL_C + 2 * HW, COL_G).astype(BF16)
    for i in range(3):
        gt_ref[:, i * 1024:(i + 1) * 1024] = proj(COL_G + i * 1024, COL_G + (i + 1) * 1024).astype(BF16)
    lr_ref[...] = proj(COL_LR, COL_END).astype(BF16)


def _proj_in(X, g, sc, sh, w_p, seg, qn, kn, cos_t, sin_t, *, B, S, T, tm):
    N, D = X.shape
    P = S + T
    NT = P // tm
    ns = S // tm
    nt = T // tm

    def row(b, j):
        return (b * NT + j, 0)

    def grp(b, j):
        return (jnp.where(j < ns, B, b), 0, 0)

    def tab(b, j):
        return (jnp.where(j < ns, nt, j - ns), 0)

    const = lambda b, j: (0, 0)
    outs = [jax.ShapeDtypeStruct((N, 768), BF16), jax.ShapeDtypeStruct((N, 768), BF16),
            jax.ShapeDtypeStruct((N, 768), BF16), jax.ShapeDtypeStruct((N, 3072), BF16),
            jax.ShapeDtypeStruct((N, LANES), BF16)]
    return pl.pallas_call(
        _proj_in_kernel,
        grid=(B, NT),
        in_specs=[pl.BlockSpec((tm, D), row),
                  pl.BlockSpec((1, D), const),
                  pl.BlockSpec((1, 1, D), grp),
                  pl.BlockSpec((1, 1, D), grp),
                  _resident((D, COL_END), const),
                  pl.BlockSpec((HW, HW), const),
                  pl.BlockSpec((1, HW), const),
                  pl.BlockSpec((1, HW), const),
                  pl.BlockSpec((tm, HW), tab),
                  pl.BlockSpec((tm, HW), tab)],
        out_specs=[pl.BlockSpec((tm, 768), row), pl.BlockSpec((tm, 768), row),
                   pl.BlockSpec((tm, 768), row), pl.BlockSpec((tm, 3072), row),
                   pl.BlockSpec((tm, LANES), row)],
        out_shape=outs,
        compiler_params=_cparams(2),
        name="proj_in",
    )(X, g, sc, sh, w_p, seg, qn, kn, cos_t, sin_t)


def _mh_attend(q, segments):
    lane = lax.broadcasted_iota(I32, (1, HW), 1)
    acc = jnp.zeros(q.shape, F32)
    for h in range(HW // HEAD_DIM):
        m = (lane // HEAD_DIM) == h
        qh = jnp.where(m, q, jnp.zeros_like(q))
        scores = []
        for k, _, bias in segments:
            s = _dot_nt(qh, k)
            if bias is not None:
                s = s + bias(h)
            scores.append(s)
        mx = scores[0].max(axis=-1, keepdims=True)
        for s in scores[1:]:
            mx = jnp.maximum(mx, s.max(axis=-1, keepdims=True))
        den = jnp.zeros_like(mx)
        o = jnp.zeros(q.shape, F32)
        for s, (_, v, _) in zip(scores, segments):
            p = jnp.exp(s - mx)
            den = den + p.sum(axis=-1, keepdims=True)
            o = o + _dot(p.astype(BF16), v)
        acc = acc + jnp.where(m, o / den, 0.0)
    return acc


def _na_kernel(q_ref, k_ref, v_ref, bias_ref, o_ref, *, S, R, kh):
    r = pl.program_id(1)
    r0 = jnp.clip(r - kh // 2, 0, R - kh)
    start = pl.multiple_of(S + r0 * GRID_W, GRID_W)
    kwin = k_ref[pl.ds(start, kh * GRID_W), :]
    vwin = v_ref[pl.ds(start, kh * GRID_W), :]
    segs = [(kwin, vwin, lambda h: bias_ref[h, 0]),
            (k_ref[0:S, :], v_ref[0:S, :], None)]
    o_ref[...] = _mh_attend(q_ref[...], segs).astype(BF16)


def _na_bias_table(rpb, kh):
    H = rpb.shape[0]
    col = np.arange(GRID_W)
    col_start = np.clip(col - NA_KW // 2, 0, GRID_W - NA_KW)
    kc = np.arange(GRID_W)
    inwin = (kc[None, :] >= col_start[:, None]) & (kc[None, :] < col_start[:, None] + NA_KW)
    dc = np.clip(kc[None, :] - col[:, None] + (NA_KW - 1), 0, 2 * NA_KW - 2)
    case = np.arange(kh)
    a = np.arange(kh)
    dr = np.clip(a[None, :] - case[:, None] + (NA_KH - 1), 0, 2 * NA_KH - 2)
    t = rpb[:, dr][:, :, :, dc]
    t = jnp.where(jnp.asarray(inwin)[None, None, None], t, NEG_BIG)
    t = jnp.transpose(t, (0, 1, 3, 2, 4))
    return t.reshape(H, kh, GRID_W, kh * GRID_W).astype(F32)


def _na_attention(A, bias_t, *, B, S, T):
    N = A.shape[0]
    P = S + T
    R = T // GRID_W
    kh = min(NA_KH, R)
    qoff = S // GRID_W
    rows_per_b = P // GRID_W

    def case(b, r):
        return (0, r - jnp.clip(r - kh // 2, 0, R - kh), 0, 0)

    return pl.pallas_call(
        functools.partial(_na_kernel, S=S, R=R, kh=kh),
        grid=(B, R),
        in_specs=[pl.BlockSpec((GRID_W, HW), lambda b, r: (b * rows_per_b + qoff + r, 0)),
                  pl.BlockSpec((P, HW), lambda b, r: (b, 1)),
                  pl.BlockSpec((P, HW), lambda b, r: (b, 2)),
                  pl.BlockSpec((NA_HEADS, 1, GRID_W, kh * GRID_W), case)],
        out_specs=pl.BlockSpec((GRID_W, HW), lambda b, r: (b * rows_per_b + qoff + r, 0)),
        out_shape=jax.ShapeDtypeStruct((N, HW), BF16),
        compiler_params=_cparams(2),
        name="na_attention",
    )(A, A, A, bias_t)


def _gqa_kernel(q_ref, k_ref, v_ref, o_ref):
    o_ref[...] = _mh_attend(q_ref[...], [(k_ref[...], v_ref[...], None)]).astype(BF16)


def _gqa_attention(C, *, B, S, T, tq):
    N = C.shape[0]
    P = S + T
    nq = T // tq
    qoff = S // tq
    per_b = P // tq
    return pl.pallas_call(
        _gqa_kernel,
        grid=(B, nq),
        in_specs=[pl.BlockSpec((tq, HW), lambda b, i: (b * per_b + qoff + i, 0)),
                  pl.BlockSpec((P, HW), lambda b, i: (b, 1)),
                  pl.BlockSpec((P, HW), lambda b, i: (b, 2))],
        out_specs=pl.BlockSpec((tq, HW), lambda b, i: (b * per_b + qoff + i, 0)),
        out_shape=jax.ShapeDtypeStruct((N, HW), BF16),
        compiler_params=_cparams(2),
        name="gqa_attention",
    )(C, C, C)


def _ctx_attn_kernel(qa_ref, ka_ref, va_ref, qc_ref, kc_ref, vc_ref, ya_in, yc_in, ya_ref, yc_ref):
    del ya_in, yc_in
    ya_ref[...] = _mh_attend(qa_ref[...], [(ka_ref[...], va_ref[...], None)]).astype(BF16)
    yc_ref[...] = _mh_attend(qc_ref[...], [(kc_ref[...], vc_ref[...], None)]).astype(BF16)


def _ctx_attention(A, C, YA, YC, *, B, S, T):
    per_b = (S + T) // S
    blk = lambda c: pl.BlockSpec((S, HW), lambda b: (b * per_b, c))
    anyspec = pl.BlockSpec(memory_space=pl.ANY)
    return pl.pallas_call(
        _ctx_attn_kernel,
        grid=(B,),
        in_specs=[blk(0), blk(1), blk(2), blk(0), blk(1), blk(2), anyspec, anyspec],
        out_specs=[blk(0), blk(0)],
        out_shape=[jax.ShapeDtypeStruct(YA.shape, YA.dtype), jax.ShapeDtypeStruct(YC.shape, YC.dtype)],
        input_output_aliases={6: 0, 7: 1},
        compiler_params=_cparams(1),
        name="ctx_attention",
    )(A, A, A, C, C, C, YA, YC)


def _gla_kernel(bq_ref, lr_ref, wg_ref, bg_ref, tri_ref, tri4_ref, o_ref, g_scr, *, S, P):
    d = pl.program_id(1)
    C = GLA_CHUNK
    ncs = S // C
    nc = P // C
    dk_h = GLA_DK // GLA_HEADS
    dv_h = GLA_DV // GLA_HEADS

    z = _dot(lr_ref[...], wg_ref[0]) + bg_ref[0]
    g_scr[...] = (jnp.minimum(z, 0.0) - jnp.log(1.0 + jnp.exp(-jnp.abs(z)))) * (1.0 / GLA_TAU)

    tri = tri_ref[0].astype(BF16)
    tri4 = tri4_ref[0]
    lane_k = lax.broadcasted_iota(I32, (1, GLA_DK), 1) // dk_h
    lane_v = lax.broadcasted_iota(I32, (1, GLA_DV), 1) // dv_h
    bd = (lax.broadcasted_iota(I32, (GLA_DV, GLA_DK), 0) // dv_h
          == lax.broadcasted_iota(I32, (GLA_DV, GLA_DK), 1) // dk_h)

    def body(n, st):
        cb = jnp.where(n < ncs, ncs - 1 - n, nc - 1 - (n - ncs))
        ci = jnp.where(d == 0, n, cb)
        r0 = pl.multiple_of(ci * C, C)
        q = bq_ref[pl.ds(r0, C), 0:GLA_DK].astype(F32)
        k = bq_ref[pl.ds(r0, C), GLA_DK:2 * GLA_DK].astype(F32)
        v = bq_ref[pl.ds(r0, C), 2 * GLA_DK:2 * GLA_DK + GLA_DV]
        g = g_scr[pl.ds(r0, C), :]
        g1, g2, g3 = _split3(g)
        bcum = _dot(tri, g1) + _dot(tri, g2) + _dot(tri, g3)
        btot = g.sum(axis=0, keepdims=True)
        q_dec = (q * (dk_h ** -0.5) * jnp.exp(bcum)).astype(BF16)
        k_inv = k * jnp.exp(-bcum)
        k_end = (k * jnp.exp(btot - bcum)).astype(BF16)
        k_stack = jnp.concatenate(
            [jnp.where(lane_k == h, k_inv, 0.0) for h in range(GLA_HEADS)], axis=0).astype(BF16)
        a_cat = _dot_nt(q_dec, k_stack) * tri4
        v_stack = jnp.concatenate(
            [jnp.where(lane_v == h, v, jnp.zeros_like(v)) for h in range(GLA_HEADS)], axis=0)
        o = _dot(a_cat.astype(BF16), v_stack) + _dot_nt(q_dec, st.astype(BF16))
        o_ref[0, pl.ds(r0, C), :] = o
        v_t = jnp.transpose(v.astype(F32)).astype(BF16)
        ds_t = _dot(v_t, k_end)
        return st * jnp.exp(btot) + jnp.where(bd, ds_t, 0.0)

    lax.fori_loop(0, nc, body, jnp.zeros((GLA_DV, GLA_DK), F32))


def _gla(Bq, LR, wg_pad, bg, tri, tri4, *, B, S, T):
    N = Bq.shape[0]
    P = S + T
    return pl.pallas_call(
        functools.partial(_gla_kernel, S=S, P=P),
        grid=(B, 2),
        in_specs=[pl.BlockSpec((P, 768), lambda b, d: (b, 0)),
                  pl.BlockSpec((P, LANES), lambda b, d: (b, 0)),
                  pl.BlockSpec((1, LANES, GLA_DK), lambda b, d: (d, 0, 0)),
                  pl.BlockSpec((1, 1, GLA_DK), lambda b, d: (d, 0, 0)),
                  pl.BlockSpec((1, GLA_CHUNK, GLA_CHUNK), lambda b, d: (d, 0, 0)),
                  pl.BlockSpec((1, GLA_CHUNK, GLA_HEADS * GLA_CHUNK), lambda b, d: (d, 0, 0))],
        out_specs=pl.BlockSpec((1, P, GLA_DV), lambda b, d: (d, b, 0)),
        out_shape=jax.ShapeDtypeStruct((2, N, GLA_DV), F32),
        scratch_shapes=[pltpu.VMEM((P, GLA_DK), F32)],
        compiler_params=_cparams(2),
        name="gla",
    )(Bq, LR, wg_pad, bg, tri, tri4)


def _merge_kernel(ya_ref, of_ref, ob_ref, rb_ref, yc_ref, gt_ref, x_ref,
                  wa_ref, wb_ref, wc_ref, wo_ref, seg_ref, gg_ref, g1_ref,
                  n2_ref, sc2_ref, sh2_ref, wr_ref,
                  x1_ref, xn_ref, aff_ref):
    o = of_ref[0] + ob_ref[0]
    yb = o * lax.rsqrt(_seg_meansq(o, seg_ref) + NORM_EPS) * gg_ref[...]
    yb = (yb * _silu(rb_ref[...].astype(F32))).astype(BF16)

    def gate(i):
        return jax.nn.sigmoid(gt_ref[:, i * 1024:(i + 1) * 1024].astype(F32))

    m = (gate(0) * _dot(ya_ref[...], wa_ref[...])
         + gate(1) * _dot(yb, wb_ref[...])
         + gate(2) * _dot(yc_ref[...], wc_ref[...]))
    x1 = x_ref[...] + g1_ref[0] * _dot(m.astype(BF16), wo_ref[...])
    x1_ref[...] = x1
    y = x1 * lax.rsqrt(jnp.mean(x1 * x1, axis=-1, keepdims=True) + NORM_EPS)
    xn = y * n2_ref[...] * (1.0 + sc2_ref[0]) + sh2_ref[0]
    xn_ref[...] = xn.astype(BF16)
    xh, xl = _split2(xn)
    wh, wl = _split2(wr_ref[...])
    logits = _dot_nt(wh, xh) + (_dot_nt(wh, xl) + _dot_nt(wl, xh))
    e = jnp.exp(logits - logits.max(axis=0, keepdims=True))
    aff_ref[...] = e / e.sum(axis=0, keepdims=True)


def _merge(YA, O, Bq, YC, G, X, wa, wb, wc, wo, seg, gg, g1, n2, sc2, sh2, wrT, *, B, S, T, tm):
    N, D = X.shape
    P = S + T
    NT = P // tm
    ns = S // tm
    j0 = 0
    E = wrT.shape[0]

    def row(b, j):
        return (b * NT + j + j0, 0)

    def grp(b, j):
        return (jnp.where(j + j0 < ns, B, b), 0, 0)

    const = lambda b, j: (0, 0)
    return pl.pallas_call(
        _merge_kernel,
        grid=(B, NT - j0),
        in_specs=[pl.BlockSpec((tm, HW), row),
                  pl.BlockSpec((1, tm, HW), lambda b, j: (0, b * NT + j + j0, 0)),
                  pl.BlockSpec((1, tm, HW), lambda b, j: (1, b * NT + j + j0, 0)),
                  pl.BlockSpec((tm, HW), lambda b, j: (b * NT + j + j0, 2)),
                  pl.BlockSpec((tm, HW), row),
                  pl.BlockSpec((tm, 3072), row),
                  pl.BlockSpec((tm, D), row),
                  pl.BlockSpec((HW, D), const), pl.BlockSpec((HW, D), const), pl.BlockSpec((HW, D), const),
                  pl.BlockSpec((D, D), const),
                  pl.BlockSpec((HW, HW), const),
                  pl.BlockSpec((1, HW), const),
                  pl.BlockSpec((1, 1, D), grp),
                  pl.BlockSpec((1, D), const),
                  pl.BlockSpec((1, 1, D), grp),
                  pl.BlockSpec((1, 1, D), grp),
                  pl.BlockSpec((E, D), const)],
        out_specs=[pl.BlockSpec((tm, D), row), pl.BlockSpec((tm, D), row),
                   pl.BlockSpec((E, tm), lambda b, j: (0, b * NT + j + j0))],
        out_shape=[jax.ShapeDtypeStruct((N, D), F32), jax.ShapeDtypeStruct((N, D), BF16),
                   jax.ShapeDtypeStruct((E, N), F32)],
        compiler_params=_cparams(2),
        name="merge",
    )(YA, O, O, Bq, YC, G, X, wa, wb, wc, wo, seg, gg, g1, n2, sc2, sh2, wrT)


def _cumsum_lanes(m01, upper, n):
    outs = []
    carry = jnp.zeros((m01.shape[0], 1), F32)
    for c in range(n // LANES):
        blk = _dot(m01[:, c * LANES:(c + 1) * LANES], upper) + carry
        outs.append(blk)
        carry = blk[:, LANES - 1:LANES]
    return jnp.concatenate(outs, axis=1)


def _topk_mask(aff, cap, upper):
    n = aff.shape[1]
    key = lax.bitcast_convert_type(aff, I32)

    def step(i, theta):
        cand = theta | jnp.left_shift(jnp.int32(1), 30 - i)
        cnt = jnp.sum(jnp.where(key >= cand, 1.0, 0.0), axis=1, keepdims=True)
        return jnp.where(cnt >= cap, cand, theta)

    theta = lax.fori_loop(0, 31, step, jnp.zeros((aff.shape[0], 1), I32))
    gt = key > theta
    eq = key == theta
    need = cap - jnp.sum(jnp.where(gt, 1.0, 0.0), axis=1, keepdims=True)
    eq01 = jnp.where(eq, 1.0, 0.0)
    rank = _cumsum_lanes(eq01.astype(BF16), upper, n)
    take = eq & (rank <= need)
    return jnp.where(gt | take, 1.0, 0.0)


def _select_kernel(aff_ref, upper_ref, ind_ref, slot_ref, w_ref, off_ref, *, S, T, cap_c, cap_l):
    aff = aff_ref[...]
    upper = upper_ref[...]
    sel = jnp.concatenate([_topk_mask(aff[:, 0:S], cap_c, upper),
                           _topk_mask(aff[:, S:S + T], cap_l, upper)], axis=1)
    sel16 = sel.astype(BF16)
    pos = _cumsum_lanes(sel16, upper, S + T) - sel
    chosen = sel > 0.5
    slot_ref[0] = jnp.where(chosen, pos, -1.0).astype(I32)
    w_ref[0] = jnp.where(chosen, aff, 0.0)
    off_ref[0] = _dot(sel16, ind_ref[...]).astype(I32)


def _select(AFF, upper, ind, *, B, S, T, cap_c, cap_l):
    E = AFF.shape[0]
    P = S + T
    return pl.pallas_call(
        functools.partial(_select_kernel, S=S, T=T, cap_c=cap_c, cap_l=cap_l),
        grid=(B,),
        in_specs=[pl.BlockSpec((E, P), lambda b: (0, b)),
                  pl.BlockSpec((LANES, LANES), lambda b: (0, 0)),
                  pl.BlockSpec((P, LANES), lambda b: (0, 0))],
        out_specs=[pl.BlockSpec((1, E, P), lambda b: (b, 0, 0)),
                   pl.BlockSpec((1, E, P), lambda b: (b, 0, 0)),
                   pl.BlockSpec((1, E, LANES), lambda b: (b, 0, 0))],
        out_shape=[jax.ShapeDtypeStruct((B, E, P), I32), jax.ShapeDtypeStruct((B, E, P), F32),
                   jax.ShapeDtypeStruct((B, E, LANES), I32)],
        compiler_params=_cparams(1),
        name="ec_select",
    )(AFF, upper, ind)


def _slot_chunks(capt):
    chunks = []
    c0 = 0
    while c0 < capt:
        cs = min(256, capt - c0)
        chunks.append((c0, cs))
        c0 += cs
    return chunks


def _gather_kernel(off_ref, x_ref, slot_ref, o_ref, acc_ref, *, E, NT, tm, capt):
    b = pl.program_id(0)
    e = pl.program_id(1)
    base = (b * E + e) * (NT + 1)
    acc_ref[...] = jnp.zeros_like(acc_ref)

    def body(j, carry):
        lo = off_ref[base + j]
        hi = off_ref[base + j + 1]
        t0 = pl.multiple_of(j * tm, tm)
        for c0, cs in _slot_chunks(capt):
            @pl.when((hi > lo) & (hi > c0) & (lo < c0 + cs))
            def _():
                slot = slot_ref[0, pl.ds(j, 1), :]
                rows = lax.broadcasted_iota(I32, (cs, tm), 0) + c0
                onehot = jnp.where(rows == slot, 1.0, 0.0).astype(BF16)
                acc_ref[c0:c0 + cs, :] += _dot(onehot, x_ref[pl.ds(t0, tm), :])
        return carry

    lax.fori_loop(0, NT, body, 0)
    o_ref[0] = acc_ref[...].astype(BF16)


def _gather(off, XN, slot, *, B, E, S, T, tm, capt):
    N, D = XN.shape
    P = S + T
    NT = P // tm
    return pl.pallas_call(
        functools.partial(_gather_kernel, E=E, NT=NT, tm=tm, capt=capt),
        grid_spec=pltpu.PrefetchScalarGridSpec(
            num_scalar_prefetch=1,
            grid=(B, E),
            in_specs=[pl.BlockSpec((P, D), lambda b, e, off: (b, 0)),
                      pl.BlockSpec((1, NT, tm), lambda b, e, off: (b * E + e, 0, 0))],
            out_specs=pl.BlockSpec((1, capt, D), lambda b, e, off: (e, b, 0)),
            scratch_shapes=[pltpu.VMEM((capt, D), F32)]),
        out_shape=jax.ShapeDtypeStruct((E, B * capt, D), BF16),
        compiler_params=_cparams(2),
        name="ec_gather",
    )(off, XN, slot.reshape(B * E, NT, tm))


def _ffn_kernel(x_ref, wg_ref, wu_ref, wd_ref, o_ref, acc_ref):
    f = pl.program_id(2)

    @pl.when(f == 0)
    def _():
        acc_ref[...] = jnp.zeros_like(acc_ref)

    x = x_ref[0]
    g = _dot(x, wg_ref[0, 0].astype(BF16))
    u = _dot(x, wu_ref[0, 0].astype(BF16))
    h = (_silu(g) * u).astype(BF16)
    acc_ref[...] += _dot(h, wd_ref[0, 0].astype(BF16))

    @pl.when(f == pl.num_programs(2) - 1)
    def _():
        o_ref[0] = acc_ref[...].astype(BF16)


def _ffn(XG, w_gate, w_up, w_down, l, *, rows, fc):
    E, R, D = XG.shape
    F = w_gate.shape[-1]
    return pl.pallas_call(
        _ffn_kernel,
        grid=(E, R // rows, F // fc),
        in_specs=[pl.BlockSpec((1, rows, D), lambda e, r, f: (e, r, 0)),
                  pl.BlockSpec((1, 1, D, fc), lambda e, r, f: (l, e, 0, f)),
                  pl.BlockSpec((1, 1, D, fc), lambda e, r, f: (l, e, 0, f)),
                  pl.BlockSpec((1, 1, fc, D), lambda e, r, f: (l, e, f, 0))],
        out_specs=pl.BlockSpec((1, rows, D), lambda e, r, f: (e, r, 0)),
        out_shape=jax.ShapeDtypeStruct((E, R, D), BF16),
        scratch_shapes=[pltpu.VMEM((rows, D), F32)],
        compiler_params=_cparams(3),
        name="ec_ffn",
    )(XG, w_gate, w_up, w_down)


def _combine_kernel(off_ref, y_ref, slot_ref, w_ref, x_ref, g2_ref, fg_ref, o_ref, acc_ref,
                    *, E, NT, j0, tm, capt, final):
    b = pl.program_id(0)
    j = pl.program_id(1) + j0
    acc_ref[...] = jnp.zeros_like(acc_ref)
    for e in range(E):
        base = (b * E + e) * (NT + 1)
        lo = off_ref[base + j]
        hi = off_ref[base + j + 1]
        for c0, cs in _slot_chunks(capt):
            @pl.when((hi > lo) & (hi > c0) & (lo < c0 + cs))
            def _():
                slot = slot_ref[:, e:e + 1]
                cols = lax.broadcasted_iota(I32, (tm, cs), 1) + c0
                onehot = jnp.where(cols == slot, 1.0, 0.0).astype(BF16)
                acc_ref[...] += w_ref[:, e:e + 1] * _dot(onehot, y_ref[e, c0:c0 + cs, :])
    x2 = x_ref[...] + g2_ref[0] * acc_ref[...]
    if final:
        x2 = x2 * lax.rsqrt(jnp.mean(x2 * x2, axis=-1, keepdims=True) + NORM_EPS) * fg_ref[...]
    if final:
        o_ref[0] = x2
    else:
        o_ref[...] = x2


def _combine(off, Y, slotT, wT, X1, g2, fg, *, B, E, S, T, tm, capt, final):
    N, D = X1.shape
    P = S + T
    NT = P // tm
    ns = S // tm
    j0 = ns if final else 0

    def row(b, j, off):
        return (b * NT + j + j0, 0)

    if final:
        out_spec = pl.BlockSpec((1, tm, D), lambda b, j, off: (b, j, 0))
        out_shape = jax.ShapeDtypeStruct((B, T, D), F32)
    else:
        out_spec = pl.BlockSpec((tm, D), row)
        out_shape = jax.ShapeDtypeStruct((N, D), F32)
    return pl.pallas_call(
        functools.partial(_combine_kernel, E=E, NT=NT, j0=j0, tm=tm, capt=capt, final=final),
        grid_spec=pltpu.PrefetchScalarGridSpec(
            num_scalar_prefetch=1,
            grid=(B, NT - j0),
            in_specs=[_resident((E, capt, D), lambda b, j, off: (0, b, 0)),
                      pl.BlockSpec((tm, E), row),
                      pl.BlockSpec((tm, E), row),
                      pl.BlockSpec((tm, D), row),
                      pl.BlockSpec((1, 1, D), lambda b, j, off: (jnp.where(j + j0 < ns, B, b), 0, 0)),
                      pl.BlockSpec((1, D), lambda b, j, off: (0, 0))],
            out_specs=out_spec,
            scratch_shapes=[pltpu.VMEM((tm, D), F32)]),
        out_shape=out_shape,
        compiler_params=_cparams(2),
        name="ec_combine",
    )(off, Y, slotT, wT, X1, g2, fg)


def _permute_w_in(w):
    D = w.shape[0]
    o = np.cumsum([0, 256, 256, 256, 128, 128, 256, 256, 16, 16, 256, 128, 128, 1024, 1024, 1024])
    qa, ka, va, qb, kb, vb, rb, lf, lb, qc, kc, vc, mga, mgb, mgc = [w[:, o[i]:o[i + 1]] for i in range(15)]
    grp = GQA_HEADS // GQA_KV_HEADS

    def rep(t):
        return jnp.concatenate([t[:, (h // grp) * HEAD_DIM:(h // grp + 1) * HEAD_DIM] for h in range(GQA_HEADS)], axis=1)

    pad = jnp.zeros((D, COL_END - COL_LR - 2 * GLA_LOWRANK), w.dtype)
    return jnp.concatenate([qa, ka, va, qb, kb, vb, rb, qc, rep(kc), rep(vc), mga, mgb, mgc, lf, lb, pad],
                           axis=1).astype(BF16)


def _rope_tables(T, tm):
    t = np.arange(T)
    row = (t // GRID_W).astype(np.float32)
    col = (t % GRID_W).astype(np.float32)
    lane = np.arange(HEAD_DIM)
    is_col = lane >= HEAD_DIM // 2
    fi = lane % 16
    inv_freq = ROPE_BASE ** (-jnp.arange(16, dtype=F32) / 16)
    pos = jnp.where(jnp.asarray(is_col)[None, :], jnp.asarray(col)[:, None], jnp.asarray(row)[:, None])
    ang = pos * inv_freq[jnp.asarray(fi)][None, :]
    cos = jnp.cos(ang)
    sin = jnp.sin(ang)
    sin = jnp.where(jnp.asarray((lane % 32) < 16)[None, :], -sin, sin)
    cos = jnp.concatenate([jnp.tile(cos, (1, HW // HEAD_DIM)), jnp.ones((tm, HW), F32)], axis=0)
    sin = jnp.concatenate([jnp.tile(sin, (1, HW // HEAD_DIM)), jnp.zeros((tm, HW), F32)], axis=0)
    return cos, sin


def kernel(x, c, ctx, c_ctx, w_ada, b_ada, norm1_g, norm2_g, w_in, na_rpb, gla_wg_f, gla_bg_f, gla_wg_b,
           gla_bg_b, gla_norm_g, gqa_qn_g, gqa_kn_g, w_branch_a, w_branch_b, w_branch_c, w_out, w_router,
           w_e_gate, w_e_up, w_e_down, final_norm_g):
    B, T, D = x.shape
    S = ctx.shape[1]
    L = w_ada.shape[0]
    E = w_router.shape[-1]
    P = S + T
    tm = min(256, S)
    NT = P // tm
    assert S % tm == 0 and T % tm == 0 and S % GRID_W == 0 and T % GRID_W == 0 and P % S == 0
    cap_l = EC_CAPACITY_FACTOR * T // E
    cap_c = EC_CAPACITY_FACTOR * S // E
    capt = cap_l + cap_c
    R = T // GRID_W
    kh = min(NA_KH, R)

    seg = jnp.asarray(np.kron(np.eye(HW // HEAD_DIM), np.ones((HEAD_DIM, HEAD_DIM))), BF16)
    cos_t, sin_t = _rope_tables(T, tm)
    ii = np.arange(GLA_CHUNK)
    tril = (ii[:, None] >= ii[None, :]).astype(np.float32)
    tri = jnp.asarray(np.stack([tril, tril.T]))
    tri4 = jnp.asarray(np.stack([np.tile(tril, (1, GLA_HEADS)), np.tile(tril.T, (1, GLA_HEADS))]))
    il = np.arange(LANES)
    upper = jnp.asarray(il[:, None] <= il[None, :], BF16)
    ind = jnp.asarray(np.arange(P)[:, None] < (np.arange(LANES) * tm)[None, :], BF16)

    RP = -(-(B + 1) // 8) * 8
    cv = jnp.concatenate([c, c_ctx[None], jnp.zeros((RP - B - 1, D), F32)], axis=0)
    mod = _adaln(cv, w_ada, b_ada)[:, :B + 1].reshape(L, B + 1, 6, 1, D)

    X = jnp.concatenate([ctx, x], axis=1).reshape(B * P, D)
    out = None
    for l in range(L):
        last = l == L - 1
        sh1, sc1, g1, sh2, sc2, g2 = [mod[l, :, i] for i in range(6)]
        w_p = _permute_w_in(w_in[l])
        qn = jnp.tile(gqa_qn_g[l], HW // HEAD_DIM)[None]
        kn = jnp.tile(gqa_kn_g[l], HW // HEAD_DIM)[None]
        A, Bq, C, G, LR = _proj_in(X, norm1_g[l][None], sc1, sh1, w_p, seg, qn, kn, cos_t, sin_t,
                                   B=B, S=S, T=T, tm=tm)

        YA = _na_attention(A, _na_bias_table(na_rpb[l], kh), B=B, S=S, T=T)
        YC = _gqa_attention(C, B=B, S=S, T=T, tq=tm)
        YA, YC = _ctx_attention(A, C, YA, YC, B=B, S=S, T=T)

        zpad = jnp.zeros((LANES - 2 * GLA_LOWRANK, GLA_DK), F32)
        zlr = jnp.zeros((GLA_LOWRANK, GLA_DK), F32)
        wg_pad = jnp.stack([jnp.concatenate([gla_wg_f[l], zlr, zpad], axis=0),
                            jnp.concatenate([zlr, gla_wg_b[l], zpad], axis=0)]).astype(BF16)
        bg = jnp.stack([gla_bg_f[l], gla_bg_b[l]])[:, None, :]
        O = _gla(Bq, LR, wg_pad, bg, tri, tri4, B=B, S=S, T=T)

        gg = jnp.tile(gla_norm_g[l], GLA_HEADS)[None]
        X1, XN, AFF = _merge(YA, O, Bq, YC, G, X,
                             w_branch_a[l].astype(BF16), w_branch_b[l].astype(BF16),
                             w_branch_c[l].astype(BF16), w_out[l].astype(BF16),
                             seg, gg, g1, norm2_g[l][None], sc2, sh2, jnp.transpose(w_router[l]),
                             B=B, S=S, T=T, tm=tm)

        slot, wsel, off = _select(AFF, upper, ind, B=B, S=S, T=T, cap_c=cap_c, cap_l=cap_l)
        off_flat = off[:, :, :NT + 1].reshape(-1)
        XG = _gather(off_flat, XN, slot, B=B, E=E, S=S, T=T, tm=tm, capt=capt)
        rows = capt * (2 if B % 2 == 0 else 1)
        Y = _ffn(XG, w_e_gate, w_e_up, w_e_down, l, rows=rows, fc=min(512, w_e_gate.shape[-1]))
        slotT = jnp.swapaxes(slot, 1, 2).reshape(B * P, E)
        wT = jnp.swapaxes(wsel, 1, 2).reshape(B * P, E)
        res = _combine(off_flat, Y, slotT, wT, X1, g2, final_norm_g[None],
                       B=B, E=E, S=S, T=T, tm=tm, capt=capt, final=last)
        if last:
            out = res
        else:
            X = res
    return out
```

```python
import functools

import jax
import jax.numpy as jnp
import numpy as np
from jax import lax
from jax.experimental import pallas as pl
from jax.experimental.pallas import tpu as pltpu

F32 = jnp.float32
BF16 = jnp.bfloat16
I32 = jnp.int32

GRID_W = 64
HEAD_DIM = 64
NORM_EPS = 1e-6
NA_HEADS = 4
NA_KH = 8
NA_KW = 16
GLA_HEADS = 4
GLA_DK = 128
GLA_DV = 256
GLA_LOWRANK = 16
GLA_TAU = 16.0
GLA_CHUNK = 64
GQA_HEADS = 4
GQA_KV_HEADS = 2
ROPE_BASE = 10000.0
N_EXPERTS = 16
EC_CAPACITY_FACTOR = 2
NEG_BIG = -1e30

LANES = 128
HW = 256
VMEM_LIMIT = 56 * 1024 * 1024

COL_A = 0
COL_B = 768
COL_C = 1536
COL_G = 2304
COL_LR = 5376
COL_END = 5504


def _cparams(n_grid):
    return pltpu.CompilerParams(dimension_semantics=("arbitrary",) * n_grid,
                                vmem_limit_bytes=VMEM_LIMIT)


def _resident(block_shape, index_map):
    return pl.BlockSpec(block_shape, index_map, pipeline_mode=pl.Buffered(1))


def _dot(a, b):
    return jnp.dot(a, b, preferred_element_type=F32)


def _dot_nt(a, b):
    return lax.dot_general(a, b, (((1,), (1,)), ((), ())), preferred_element_type=F32)


def _split2(a):
    hi = a.astype(BF16)
    lo = (a - hi.astype(F32)).astype(BF16)
    return hi, lo


def _split3(a):
    h1 = a.astype(BF16)
    r1 = a - h1.astype(F32)
    h2 = r1.astype(BF16)
    h3 = (r1 - h2.astype(F32)).astype(BF16)
    return h1, h2, h3


def _seg_meansq(x, seg_ref):
    hi, lo = _split2(x * x)
    seg = seg_ref[...]
    return (_dot(hi, seg) + _dot(lo, seg)) * (1.0 / HEAD_DIM)


def _silu(x):
    return x * jax.nn.sigmoid(x)


def _adaln_kernel(cv_ref, w_ref, b_ref, o_ref):
    s = _silu(cv_ref[...])
    w = w_ref[0]
    s1, s2, s3 = _split3(s)
    w1, w2, w3 = _split3(w)
    acc = _dot(s1, w1) + (_dot(s1, w2) + _dot(s2, w1)) + (_dot(s2, w2) + _dot(s1, w3) + _dot(s3, w1))
    o_ref[0] = acc + b_ref[0]


def _adaln(cv, w_ada, b_ada):
    L, D, D6 = w_ada.shape
    R = cv.shape[0]
    tn = 1024
    return pl.pallas_call(
        _adaln_kernel,
        grid=(L, D6 // tn),
        in_specs=[pl.BlockSpec((R, D), lambda l, n: (0, 0)),
                  pl.BlockSpec((1, D, tn), lambda l, n: (l, 0, n)),
                  pl.BlockSpec((1, 1, tn), lambda l, n: (l, 0, n))],
        out_specs=pl.BlockSpec((1, R, tn), lambda l, n: (l, 0, n)),
        out_shape=jax.ShapeDtypeStruct((L, R, D6), F32),
        compiler_params=_cparams(2),
        name="adaln",
    )(cv, w_ada, b_ada.reshape(L, 1, D6))


def _swap16(y):
    lane = lax.broadcasted_iota(I32, y.shape, 1)
    first = (lane % 32) < 16
    return jnp.where(first, pltpu.roll(y, LANES - 16, 1), pltpu.roll(y, 16, 1))


def _norm_rope(x, seg_ref, g_ref, cos_ref, sin_ref, scale):
    y = x * lax.rsqrt(_seg_meansq(x, seg_ref) + NORM_EPS) * g_ref[...]
    halves = []
    for i in range(HW // LANES):
        sl = slice(i * LANES, (i + 1) * LANES)
        yh = y[:, sl]
        halves.append(yh * cos_ref[:, sl] + _swap16(yh) * sin_ref[:, sl])
    out = jnp.concatenate(halves, axis=1)
    return out * scale if scale != 1.0 else out


def _proj_in_kernel(x_ref, g_ref, sc_ref, sh_ref, scc_ref, shc_ref, w_ref, seg_ref, qn_ref, kn_ref,
                    cos_ref, sin_ref, a_ref, b_ref, c_ref, gt_ref, lr_ref, *, S):
    x = x_ref[...]
    tr = x.shape[0]
    is_ctx = (pl.program_id(1) * tr + lax.broadcasted_iota(I32, (tr, 1), 0)) < S
    sc = jnp.where(is_ctx, scc_ref[0], sc_ref[0])
    sh = jnp.where(is_ctx, shc_ref[0], sh_ref[0])
    y = x * lax.rsqrt(jnp.mean(x * x, axis=-1, keepdims=True) + NORM_EPS)
    u = (y * g_ref[...] * (1.0 + sc) + sh).astype(BF16)

    def proj(c0, c1):
        return _dot(u, w_ref[:, c0:c1])

    qscale = HEAD_DIM ** -0.5
    a_ref[:, 0:HW] = (proj(COL_A, COL_A + HW) * qscale).astype(BF16)
    a_ref[:, HW:3 * HW] = proj(COL_A + HW, COL_B).astype(BF16)
    b_ref[...] = proj(COL_B, COL_C).astype(BF16)
    qc = proj(COL_C, COL_C + HW)
    c_ref[:, 0:HW] = _norm_rope(qc, seg_ref, qn_ref, cos_ref, sin_ref, qscale).astype(BF16)
    kc = proj(COL_C + HW, COL_C + 2 * HW)
    c_ref[:, HW:2 * HW] = _norm_rope(kc, seg_ref, kn_ref, cos_ref, sin_ref, 1.0).astype(BF16)
    c_ref[:, 2 * HW:3 * HW] = proj(COL_C + 2 * HW, COL_G).astype(BF16)
    for i in range(3):
        gt_ref[:, i * 1024:(i + 1) * 1024] = proj(COL_G + i * 1024, COL_G + (i + 1) * 1024).astype(BF16)
    lr_ref[...] = proj(COL_LR, COL_END).astype(BF16)


def _row_tile(S, P):
    for k in (8, 4, 2, 1):
        if P % k == 0 and (P // k) % 16 == 0 and P // k >= S:
            return P // k
    raise ValueError("no row tile for these shapes")


def _proj_in(X, g, sc, sh, w_p, seg, qn, kn, cos_t, sin_t, *, B, S, T):
    N, D = X.shape
    P = S + T
    tm = _row_tile(S, P)
    NT = P // tm

    def row(b, j):
        return (b * NT + j, 0)

    grp = lambda b, j: (b, 0, 0)
    grp_c = lambda b, j: (B, 0, 0)
    tab = lambda b, j: (j, 0)
    const = lambda b, j: (0, 0)
    outs = [jax.ShapeDtypeStruct((N, 768), BF16), jax.ShapeDtypeStruct((N, 768), BF16),
            jax.ShapeDtypeStruct((N, 768), BF16), jax.ShapeDtypeStruct((N, 3072), BF16),
            jax.ShapeDtypeStruct((N, LANES), BF16)]
    return pl.pallas_call(
        functools.partial(_proj_in_kernel, S=S),
        grid=(B, NT),
        in_specs=[pl.BlockSpec((tm, D), row),
                  pl.BlockSpec((1, D), const),
                  pl.BlockSpec((1, 1, D), grp),
                  pl.BlockSpec((1, 1, D), grp),
                  pl.BlockSpec((1, 1, D), grp_c),
                  pl.BlockSpec((1, 1, D), grp_c),
                  _resident((D, COL_END), const),
                  pl.BlockSpec((HW, HW), const),
                  pl.BlockSpec((1, HW), const),
                  pl.BlockSpec((1, HW), const),
                  pl.BlockSpec((tm, HW), tab),
                  pl.BlockSpec((tm, HW), tab)],
        out_specs=[pl.BlockSpec((tm, 768), row), pl.BlockSpec((tm, 768), row),
                   pl.BlockSpec((tm, 768), row), pl.BlockSpec((tm, 3072), row),
                   pl.BlockSpec((tm, LANES), row)],
        out_shape=outs,
        compiler_params=_cparams(2),
        name="proj_in",
    )(X, g, sc, sh, sc, sh, w_p, seg, qn, kn, cos_t, sin_t)


def _mh_attend(q, segments):
    nq = q.shape[0]
    nh = HW // HEAD_DIM
    lane = lax.broadcasted_iota(I32, (1, HW), 1) // HEAD_DIM
    qs = jnp.concatenate([jnp.where(lane == h, q, jnp.zeros_like(q)) for h in range(nh)], axis=0)
    mx = den = o = None
    for k, v, bias in segments:
        s = _dot_nt(qs, k)
        if bias is not None:
            s = s + bias
        smax = s.max(axis=-1, keepdims=True)
        if mx is None:
            mx = smax
            p = jnp.exp(s - mx)
            den = p.sum(axis=-1, keepdims=True)
            o = _dot(p.astype(BF16), v)
        else:
            mx_new = jnp.maximum(mx, smax)
            alpha = jnp.exp(mx - mx_new)
            p = jnp.exp(s - mx_new)
            den = den * alpha + p.sum(axis=-1, keepdims=True)
            o = o * alpha + _dot(p.astype(BF16), v)
            mx = mx_new
    o = o / den
    acc = jnp.zeros(q.shape, F32)
    for h in range(nh):
        acc = acc + jnp.where(lane == h, o[h * nq:(h + 1) * nq], 0.0)
    return acc


def _na_kernel(q_ref, k_ref, v_ref, bias_ref, o_ref, *, S, R, kh, rb):
    for i in range(rb):
        r = pl.program_id(1) * rb + i
        r0 = jnp.clip(r - kh // 2, 0, R - kh)
        start = pl.multiple_of(S + r0 * GRID_W, GRID_W)
        kwin = k_ref[pl.ds(start, kh * GRID_W), :]
        vwin = v_ref[pl.ds(start, kh * GRID_W), :]
        segs = [(kwin, vwin, bias_ref[r - r0]),
                (k_ref[0:S, :], v_ref[0:S, :], None)]
        rows = slice(i * GRID_W, (i + 1) * GRID_W)
        o_ref[rows, :] = _mh_attend(q_ref[rows, :], segs).astype(BF16)


def _na_bias_table(rpb, kh):
    H = rpb.shape[0]
    col = np.arange(GRID_W)
    col_start = np.clip(col - NA_KW // 2, 0, GRID_W - NA_KW)
    kc = np.arange(GRID_W)
    inwin = (kc[None, :] >= col_start[:, None]) & (kc[None, :] < col_start[:, None] + NA_KW)
    dc = np.clip(kc[None, :] - col[:, None] + (NA_KW - 1), 0, 2 * NA_KW - 2)
    case = np.arange(kh)
    a = np.arange(kh)
    dr = np.clip(a[None, :] - case[:, None] + (NA_KH - 1), 0, 2 * NA_KH - 2)
    t = rpb[:, dr][:, :, :, dc]
    t = jnp.where(jnp.asarray(inwin)[None, None, None], t, NEG_BIG)
    t = jnp.transpose(t, (1, 0, 3, 2, 4))
    return t.reshape(kh, H * GRID_W, kh * GRID_W).astype(F32)


def _na_attention(A, bias_t, *, B, S, T):
    N = A.shape[0]
    P = S + T
    R = T // GRID_W
    kh = min(NA_KH, R)
    rb = 4 if (R % 4 == 0 and S % (4 * GRID_W) == 0) else 1
    tq = rb * GRID_W
    qoff = S // tq
    per_b = P // tq

    return pl.pallas_call(
        functools.partial(_na_kernel, S=S, R=R, kh=kh, rb=rb),
        grid=(B, R // rb),
        in_specs=[pl.BlockSpec((tq, HW), lambda b, r: (b * per_b + qoff + r, 0)),
                  pl.BlockSpec((P, HW), lambda b, r: (b, 1)),
                  pl.BlockSpec((P, HW), lambda b, r: (b, 2)),
                  pl.BlockSpec((kh, NA_HEADS * GRID_W, kh * GRID_W), lambda b, r: (0, 0, 0))],
        out_specs=pl.BlockSpec((tq, HW), lambda b, r: (b * per_b + qoff + r, 0)),
        out_shape=jax.ShapeDtypeStruct((N, HW), BF16),
        compiler_params=_cparams(2),
        name="na_attention",
    )(A, A, A, bias_t)


def _key_chunks(n, target=1024, align=256):
    nchunks = max(1, n // target)
    base = (n // nchunks) // align * align
    if base == 0:
        return [(0, n)]
    bounds = [i * base for i in range(nchunks)] + [n]
    return [(bounds[i], bounds[i + 1]) for i in range(nchunks)]


def _gqa_kernel(q_ref, k_ref, v_ref, o_ref):
    segs = [(k_ref[c0:c1, :], v_ref[c0:c1, :], None) for c0, c1 in _key_chunks(k_ref.shape[0])]
    o_ref[...] = _mh_attend(q_ref[...], segs).astype(BF16)


def _gqa_attention(C, *, B, S, T, tq):
    N = C.shape[0]
    P = S + T
    nq = T // tq
    qoff = S // tq
    per_b = P // tq
    return pl.pallas_call(
        _gqa_kernel,
        grid=(B, nq),
        in_specs=[pl.BlockSpec((tq, HW), lambda b, i: (b * per_b + qoff + i, 0)),
                  pl.BlockSpec((P, HW), lambda b, i: (b, 1)),
                  pl.BlockSpec((P, HW), lambda b, i: (b, 2))],
        out_specs=pl.BlockSpec((tq, HW), lambda b, i: (b * per_b + qoff + i, 0)),
        out_shape=jax.ShapeDtypeStruct((N, HW), BF16),
        compiler_params=_cparams(2),
        name="gqa_attention",
    )(C, C, C)


def _ctx_attn_kernel(qa_ref, ka_ref, va_ref, qc_ref, kc_ref, vc_ref, ya_in, yc_in, ya_ref, yc_ref):
    del ya_in, yc_in
    ya_ref[...] = _mh_attend(qa_ref[...], [(ka_ref[...], va_ref[...], None)]).astype(BF16)
    yc_ref[...] = _mh_attend(qc_ref[...], [(kc_ref[...], vc_ref[...], None)]).astype(BF16)


def _ctx_attention(A, C, YA, YC, *, B, S, T):
    per_b = (S + T) // S
    blk = lambda c: pl.BlockSpec((S, HW), lambda b: (b * per_b, c))
    anyspec = pl.BlockSpec(memory_space=pl.ANY)
    return pl.pallas_call(
        _ctx_attn_kernel,
        grid=(B,),
        in_specs=[blk(0), blk(1), blk(2), blk(0), blk(1), blk(2), anyspec, anyspec],
        out_specs=[blk(0), blk(0)],
        out_shape=[jax.ShapeDtypeStruct(YA.shape, YA.dtype), jax.ShapeDtypeStruct(YC.shape, YC.dtype)],
        input_output_aliases={6: 0, 7: 1},
        compiler_params=_cparams(1),
        name="ctx_attention",
    )(A, A, A, C, C, C, YA, YC)


def _gla_kernel(bq_ref, lr_ref, wg_ref, bg_ref, tri_ref, tri4_ref, o_ref, g_scr, *, S, P):
    C = GLA_CHUNK
    ncs = S // C
    nc = P // C
    dk_h = GLA_DK // GLA_HEADS
    dv_h = GLA_DV // GLA_HEADS

    for d in range(2):
        z = _dot(lr_ref[...], wg_ref[d]) + bg_ref[d]
        g_scr[d] = (jnp.minimum(z, 0.0) - jnp.log(1.0 + jnp.exp(-jnp.abs(z)))) * (1.0 / GLA_TAU)

    lane_k = lax.broadcasted_iota(I32, (1, GLA_DK), 1) // dk_h
    lane_v = lax.broadcasted_iota(I32, (1, GLA_DV), 1) // dv_h
    bd = (lax.broadcasted_iota(I32, (GLA_DV, GLA_DK), 0) // dv_h
          == lax.broadcasted_iota(I32, (GLA_DV, GLA_DK), 1) // dk_h)

    def chunk(d, ci, st):
        tri = tri_ref[d].astype(BF16)
        tri4 = tri4_ref[d]
        r0 = pl.multiple_of(ci * C, C)
        q = bq_ref[pl.ds(r0, C), 0:GLA_DK].astype(F32)
        k = bq_ref[pl.ds(r0, C), GLA_DK:2 * GLA_DK].astype(F32)
        v = bq_ref[pl.ds(r0, C), 2 * GLA_DK:2 * GLA_DK + GLA_DV]
        g = g_scr[d, pl.ds(r0, C), :]
        g1, g2, g3 = _split3(g)
        bcum = _dot(tri, g1) + _dot(tri, g2) + _dot(tri, g3)
        btot = g.sum(axis=0, keepdims=True)
        q_dec = (q * (dk_h ** -0.5) * jnp.exp(bcum)).astype(BF16)
        k_inv = k * jnp.exp(-bcum)
        k_end = (k * jnp.exp(btot - bcum)).astype(BF16)
        k_stack = jnp.concatenate(
            [jnp.where(lane_k == h, k_inv, 0.0) for h in range(GLA_HEADS)], axis=0).astype(BF16)
        a_cat = _dot_nt(q_dec, k_stack) * tri4
        v_stack = jnp.concatenate(
            [jnp.where(lane_v == h, v, jnp.zeros_like(v)) for h in range(GLA_HEADS)], axis=0)
        o = _dot(a_cat.astype(BF16), v_stack) + _dot_nt(q_dec, st.astype(BF16))
        o_ref[d, pl.ds(r0, C), :] = o
        v_t = jnp.transpose(v.astype(F32)).astype(BF16)
        ds_t = _dot(v_t, k_end)
        return st * jnp.exp(btot) + jnp.where(bd, ds_t, 0.0)

    def body(n, sts):
        cb = jnp.where(n < ncs, ncs - 1 - n, nc - 1 - (n - ncs))
        return chunk(0, n, sts[0]), chunk(1, cb, sts[1])

    zero = jnp.zeros((GLA_DV, GLA_DK), F32)
    lax.fori_loop(0, nc, body, (zero, zero))


def _gla(Bq, LR, wg_pad, bg, tri, tri4, *, B, S, T):
    N = Bq.shape[0]
    P = S + T
    return pl.pallas_call(
        functools.partial(_gla_kernel, S=S, P=P),
        grid=(B,),
        in_specs=[pl.BlockSpec((P, 768), lambda b: (b, 0)),
                  pl.BlockSpec((P, LANES), lambda b: (b, 0)),
                  pl.BlockSpec((2, LANES, GLA_DK), lambda b: (0, 0, 0)),
                  pl.BlockSpec((2, 1, GLA_DK), lambda b: (0, 0, 0)),
                  pl.BlockSpec((2, GLA_CHUNK, GLA_CHUNK), lambda b: (0, 0, 0)),
                  pl.BlockSpec((2, GLA_CHUNK, GLA_HEADS * GLA_CHUNK), lambda b: (0, 0, 0))],
        out_specs=pl.BlockSpec((2, P, GLA_DV), lambda b: (0, b, 0)),
        out_shape=jax.ShapeDtypeStruct((2, N, GLA_DV), F32),
        scratch_shapes=[pltpu.VMEM((2, P, GLA_DK), F32)],
        compiler_params=_cparams(1),
        name="gla",
    )(Bq, LR, wg_pad, bg, tri, tri4)


def _merge_kernel(ya_ref, of_ref, ob_ref, rb_ref, yc_ref, gt_ref, x_ref,
                  wa_ref, wb_ref, wc_ref, wo_ref, seg_ref, gg_ref, g1_ref,
                  n2_ref, sc2_ref, sh2_ref, wr_ref,
                  x1_ref, xn_ref, aff_ref):
    o = of_ref[0] + ob_ref[0]
    yb = o * lax.rsqrt(_seg_meansq(o, seg_ref) + NORM_EPS) * gg_ref[...]
    yb = (yb * _silu(rb_ref[...].astype(F32))).astype(BF16)

    def gate(i):
        return jax.nn.sigmoid(gt_ref[:, i * 1024:(i + 1) * 1024].astype(F32))

    m = (gate(0) * _dot(ya_ref[...], wa_ref[...])
         + gate(1) * _dot(yb, wb_ref[...])
         + gate(2) * _dot(yc_ref[...], wc_ref[...]))
    x1 = x_ref[...] + g1_ref[0] * _dot(m.astype(BF16), wo_ref[...])
    x1_ref[...] = x1
    y = x1 * lax.rsqrt(jnp.mean(x1 * x1, axis=-1, keepdims=True) + NORM_EPS)
    xn = y * n2_ref[...] * (1.0 + sc2_ref[0]) + sh2_ref[0]
    xn_ref[...] = xn.astype(BF16)
    xh, xl = _split2(xn)
    wh, wl = _split2(wr_ref[...])
    logits = _dot_nt(wh, xh) + (_dot_nt(wh, xl) + _dot_nt(wl, xh))
    e = jnp.exp(logits - logits.max(axis=0, keepdims=True))
    aff_ref[...] = e / e.sum(axis=0, keepdims=True)


def _merge(YA, O, Bq, YC, G, X, wa, wb, wc, wo, seg, gg, g1, n2, sc2, sh2, wrT, *, B, S, T, tm):
    N, D = X.shape
    P = S + T
    NT = P // tm
    ns = S // tm
    j0 = 0
    E = wrT.shape[0]

    def row(b, j):
        return (b * NT + j + j0, 0)

    def grp(b, j):
        return (jnp.where(j + j0 < ns, B, b), 0, 0)

    const = lambda b, j: (0, 0)
    return pl.pallas_call(
        _merge_kernel,
        grid=(B, NT - j0),
        in_specs=[pl.BlockSpec((tm, HW), row),
                  pl.BlockSpec((1, tm, HW), lambda b, j: (0, b * NT + j + j0, 0)),
                  pl.BlockSpec((1, tm, HW), lambda b, j: (1, b * NT + j + j0, 0)),
                  pl.BlockSpec((tm, HW), lambda b, j: (b * NT + j + j0, 2)),
                  pl.BlockSpec((tm, HW), row),
                  pl.BlockSpec((tm, 3072), row),
                  pl.BlockSpec((tm, D), row),
                  pl.BlockSpec((HW, D), const), pl.BlockSpec((HW, D), const), pl.BlockSpec((HW, D), const),
                  pl.BlockSpec((D, D), const),
                  pl.BlockSpec((HW, HW), const),
                  pl.BlockSpec((1, HW), const),
                  pl.BlockSpec((1, 1, D), grp),
                  pl.BlockSpec((1, D), const),
                  pl.BlockSpec((1, 1, D), grp),
                  pl.BlockSpec((1, 1, D), grp),
                  pl.BlockSpec((E, D), const)],
        out_specs=[pl.BlockSpec((tm, D), row), pl.BlockSpec((tm, D), row),
                   pl.BlockSpec((E, tm), lambda b, j: (0, b * NT + j + j0))],
        out_shape=[jax.ShapeDtypeStruct((N, D), F32), jax.ShapeDtypeStruct((N, D), BF16),
                   jax.ShapeDtypeStruct((E, N), F32)],
        compiler_params=_cparams(2),
        name="merge",
    )(YA, O, O, Bq, YC, G, X, wa, wb, wc, wo, seg, gg, g1, n2, sc2, sh2, wrT)


def _cumsum_lanes(m01, upper, n):
    outs = []
    carry = jnp.zeros((m01.shape[0], 1), F32)
    for c in range(n // LANES):
        blk = _dot(m01[:, c * LANES:(c + 1) * LANES], upper) + carry
        outs.append(blk)
        carry = blk[:, LANES - 1:LANES]
    return jnp.concatenate(outs, axis=1)


def _topk_mask(aff, cap, upper):
    n = aff.shape[1]
    key = lax.bitcast_convert_type(aff, I32)

    def step(i, theta):
        cand = theta | jnp.left_shift(jnp.int32(1), 30 - i)
        cnt = jnp.sum(jnp.where(key >= cand, 1.0, 0.0), axis=1, keepdims=True)
        return jnp.where(cnt >= cap, cand, theta)

    theta = lax.fori_loop(0, 31, step, jnp.zeros((aff.shape[0], 1), I32))
    gt = key > theta
    eq = key == theta
    need = cap - jnp.sum(jnp.where(gt, 1.0, 0.0), axis=1, keepdims=True)
    eq01 = jnp.where(eq, 1.0, 0.0)
    rank = _cumsum_lanes(eq01.astype(BF16), upper, n)
    take = eq & (rank <= need)
    return jnp.where(gt | take, 1.0, 0.0)


def _select_kernel(aff_ref, upper_ref, ind_ref, slot_ref, w_ref, off_ref, *, S, T, cap_c, cap_l):
    aff = aff_ref[...]
    upper = upper_ref[...]
    sel = jnp.concatenate([_topk_mask(aff[:, 0:S], cap_c, upper),
                           _topk_mask(aff[:, S:S + T], cap_l, upper)], axis=1)
    sel16 = sel.astype(BF16)
    pos = _cumsum_lanes(sel16, upper, S + T) - sel
    chosen = sel > 0.5
    slot_ref[0] = jnp.where(chosen, pos, -1.0).astype(I32)
    w_ref[0] = jnp.where(chosen, aff, 0.0)
    off_ref[0] = _dot(sel16, ind_ref[...]).astype(I32)


def _select(AFF, upper, ind, *, B, S, T, cap_c, cap_l):
    E = AFF.shape[0]
    P = S + T
    return pl.pallas_call(
        functools.partial(_select_kernel, S=S, T=T, cap_c=cap_c, cap_l=cap_l),
        grid=(B,),
        in_specs=[pl.BlockSpec((E, P), lambda b: (0, b)),
                  pl.BlockSpec((LANES, LANES), lambda b: (0, 0)),
                  pl.BlockSpec((P, LANES), lambda b: (0, 0))],
        out_specs=[pl.BlockSpec((1, E, P), lambda b: (b, 0, 0)),
                   pl.BlockSpec((1, E, P), lambda b: (b, 0, 0)),
                   pl.BlockSpec((1, E, LANES), lambda b: (b, 0, 0))],
        out_shape=[jax.ShapeDtypeStruct((B, E, P), I32), jax.ShapeDtypeStruct((B, E, P), F32),
                   jax.ShapeDtypeStruct((B, E, LANES), I32)],
        compiler_params=_cparams(1),
        name="ec_select",
    )(AFF, upper, ind)


EC_WIN = 64
SLOT_ALIGN = 16


def _win_base(lo, capt):
    return pl.multiple_of(jnp.minimum(lo // SLOT_ALIGN * SLOT_ALIGN, capt - EC_WIN), SLOT_ALIGN)


def _n_windows(tm):
    return -(-(tm + SLOT_ALIGN - 1) // EC_WIN)


def _gather_kernel(off_ref, x_ref, wc_ref, slot_ref, xg_ref, wg_ref, *, E, NT, tm, capt):
    b = pl.program_id(0)
    j = pl.program_id(1)
    W = EC_WIN

    @pl.when(j == 0)
    def _():
        xg_ref[...] = jnp.zeros_like(xg_ref)
        wg_ref[...] = jnp.zeros_like(wg_ref)

    def offs(e):
        i = (b * E + e) * (NT + 1) + j
        return off_ref[i], off_ref[i + 1]

    rows = lax.broadcasted_iota(I32, (W, tm), 0)
    x = x_ref[...]
    wc = wc_ref[...]
    bases = [_win_base(offs(e)[0], capt) for e in range(E)]
    lhs = jnp.concatenate(
        [jnp.where(rows == slot_ref[0, e:e + 1, :] - bases[e], 1.0, 0.0).astype(BF16) for e in range(E)], axis=0)
    rx = _dot(lhs, x).astype(BF16)
    rw = _dot(lhs, wc)
    for e in range(E):
        xg_ref[e, pl.ds(bases[e], W), :] += rx[e * W:(e + 1) * W]
        wg_ref[e, pl.ds(bases[e], W), :] += rw[e * W:(e + 1) * W]

    def extra(e, carry):
        lo, hi = offs(e)
        base = _win_base(lo, capt)
        for w in range(1, _n_windows(tm)):
            first = base + w * W

            @pl.when(hi > first)
            def _():
                start = pl.multiple_of(jnp.minimum(first, capt - W), SLOT_ALIGN)
                s = slot_ref[0, pl.ds(e, 1), :]
                onehot = jnp.where(rows == jnp.where(s >= first, s - start, -1), 1.0, 0.0).astype(BF16)
                xg_ref[e, pl.ds(start, W), :] += _dot(onehot, x).astype(BF16)
                wg_ref[e, pl.ds(start, W), :] += _dot(onehot, wc)
        return carry

    lax.fori_loop(0, E, extra, 0)


def _gather(off, XN, WC, slot, *, B, E, S, T, tm, capt):
    N, D = XN.shape
    P = S + T
    NT = P // tm
    row = lambda b, j, off: (b * NT + j, 0)
    return pl.pallas_call(
        functools.partial(_gather_kernel, E=E, NT=NT, tm=tm, capt=capt),
        grid_spec=pltpu.PrefetchScalarGridSpec(
            num_scalar_prefetch=1,
            grid=(B, NT),
            in_specs=[pl.BlockSpec((tm, D), row),
                      pl.BlockSpec((tm, LANES), row),
                      pl.BlockSpec((1, E, tm), lambda b, j, off: (b, 0, j))],
            out_specs=[pl.BlockSpec((E, capt, D), lambda b, j, off: (0, b, 0)),
                       pl.BlockSpec((E, capt, LANES), lambda b, j, off: (0, b, 0))]),
        out_shape=[jax.ShapeDtypeStruct((E, B * capt, D), BF16),
                   jax.ShapeDtypeStruct((E, B * capt, LANES), F32)],
        compiler_params=_cparams(2),
        name="ec_gather",
    )(off, XN, WC, slot)


def _ffn_kernel(x_ref, val_ref, wg_ref, wu_ref, wd_ref, o_ref, acc_ref, *, E):
    f = pl.program_id(2)

    @pl.when(f == 0)
    def _():
        acc_ref[...] = jnp.zeros_like(acc_ref)

    x = x_ref[0]
    g = _dot(x, wg_ref[0, 0].astype(BF16))
    u = _dot(x, wu_ref[0, 0].astype(BF16))
    h = (_silu(g) * u).astype(BF16)
    acc_ref[...] += _dot(h, wd_ref[0, 0].astype(BF16))

    @pl.when(f == pl.num_programs(2) - 1)
    def _():
        lane = lax.broadcasted_iota(I32, (1, LANES), 1)
        mine = (lane % E == pl.program_id(0)) & (lane < 3 * E)
        val = jnp.sum(jnp.where(mine, val_ref[0], 0.0), axis=1, keepdims=True)
        o_ref[0] = (acc_ref[...] * val).astype(BF16)


def _ffn(XG, WG, w_gate, w_up, w_down, l, *, rows, fc):
    E, R, D = XG.shape
    F = w_gate.shape[-1]
    return pl.pallas_call(
        functools.partial(_ffn_kernel, E=E),
        grid=(E, R // rows, F // fc),
        in_specs=[pl.BlockSpec((1, rows, D), lambda e, r, f: (e, r, 0)),
                  pl.BlockSpec((1, rows, LANES), lambda e, r, f: (e, r, 0)),
                  pl.BlockSpec((1, 1, D, fc), lambda e, r, f: (l, e, 0, f)),
                  pl.BlockSpec((1, 1, D, fc), lambda e, r, f: (l, e, 0, f)),
                  pl.BlockSpec((1, 1, fc, D), lambda e, r, f: (l, e, f, 0))],
        out_specs=pl.BlockSpec((1, rows, D), lambda e, r, f: (e, r, 0)),
        out_shape=jax.ShapeDtypeStruct((E, R, D), BF16),
        scratch_shapes=[pltpu.VMEM((rows, D), F32)],
        compiler_params=_cparams(3),
        name="ec_ffn",
    )(XG, WG, w_gate, w_up, w_down)


def _combine_kernel(off_ref, y_ref, slott_ref, slot_ref, x_ref, g2_ref, fg_ref, o_ref, acc_ref,
                    *, E, NT, j0, tm, capt, final):
    b = pl.program_id(0)
    j = pl.program_id(1) + j0
    W = EC_WIN
    per_piece = LANES // W

    def offs(e):
        i = (b * E + e) * (NT + 1) + j
        return off_ref[i], off_ref[i + 1]

    cols = lax.broadcasted_iota(I32, (tm, LANES), 1)
    bases = [_win_base(offs(e)[0], capt) for e in range(E)]
    pieces = []
    for p0 in range(0, E, per_piece):
        hit = None
        for i in range(per_piece):
            e = p0 + i
            rel = slott_ref[:, e:e + 1] - bases[e]
            rel = jnp.where((rel >= 0) & (rel < W), rel + i * W, -1)
            m = cols == rel
            hit = m if hit is None else (hit | m)
        pieces.append(jnp.where(hit, 1.0, 0.0).astype(BF16))
    lhs = jnp.concatenate(pieces, axis=1)
    rhs = jnp.concatenate([y_ref[e, pl.ds(bases[e], W), :] for e in range(E)], axis=0)
    acc_ref[...] = _dot(lhs, rhs)

    rows = lax.broadcasted_iota(I32, (W, tm), 0)

    def extra(e, carry):
        lo, hi = offs(e)
        base = _win_base(lo, capt)
        for w in range(1, _n_windows(tm)):
            first = base + w * W

            @pl.when(hi > first)
            def _():
                start = pl.multiple_of(jnp.minimum(first, capt - W), SLOT_ALIGN)
                s = slot_ref[0, pl.ds(e, 1), :]
                onehot_t = jnp.where(rows == jnp.where(s >= first, s - start, -1), 1.0, 0.0)
                onehot = jnp.transpose(onehot_t).astype(BF16)
                acc_ref[...] += _dot(onehot, y_ref[e, pl.ds(start, W), :])
        return carry

    lax.fori_loop(0, E, extra, 0)
    x2 = x_ref[...] + g2_ref[0] * acc_ref[...]
    if final:
        x2 = x2 * lax.rsqrt(jnp.mean(x2 * x2, axis=-1, keepdims=True) + NORM_EPS) * fg_ref[...]
    if final:
        o_ref[0] = x2
    else:
        o_ref[...] = x2


def _combine(off, Y, slotT, slot, X1, g2, fg, *, B, E, S, T, tm, capt, final):
    N, D = X1.shape
    P = S + T
    NT = P // tm
    ns = S // tm
    j0 = ns if final else 0

    def row(b, j, off):
        return (b * NT + j + j0, 0)

    if final:
        out_spec = pl.BlockSpec((1, tm, D), lambda b, j, off: (b, j, 0))
        out_shape = jax.ShapeDtypeStruct((B, T, D), F32)
    else:
        out_spec = pl.BlockSpec((tm, D), row)
        out_shape = jax.ShapeDtypeStruct((N, D), F32)
    return pl.pallas_call(
        functools.partial(_combine_kernel, E=E, NT=NT, j0=j0, tm=tm, capt=capt, final=final),
        grid_spec=pltpu.PrefetchScalarGridSpec(
            num_scalar_prefetch=1,
            grid=(B, NT - j0),
            in_specs=[_resident((E, capt, D), lambda b, j, off: (0, b, 0)),
                      pl.BlockSpec((tm, E), row),
                      pl.BlockSpec((1, E, tm), lambda b, j, off: (b, 0, j + j0)),
                      pl.BlockSpec((tm, D), row),
                      pl.BlockSpec((1, 1, D), lambda b, j, off: (jnp.where(j + j0 < ns, B, b), 0, 0)),
                      pl.BlockSpec((1, D), lambda b, j, off: (0, 0))],
            out_specs=out_spec,
            scratch_shapes=[pltpu.VMEM((tm, D), F32)]),
        out_shape=out_shape,
        compiler_params=_cparams(2),
        name="ec_combine",
    )(off, Y, slotT, slot, X1, g2, fg)


def _permute_w_in(w):
    D = w.shape[0]
    o = np.cumsum([0, 256, 256, 256, 128, 128, 256, 256, 16, 16, 256, 128, 128, 1024, 1024, 1024])
    qa, ka, va, qb, kb, vb, rb, lf, lb, qc, kc, vc, mga, mgb, mgc = [w[:, o[i]:o[i + 1]] for i in range(15)]
    grp = GQA_HEADS // GQA_KV_HEADS

    def rep(t):
        return jnp.concatenate([t[:, (h // grp) * HEAD_DIM:(h // grp + 1) * HEAD_DIM] for h in range(GQA_HEADS)], axis=1)

    pad = jnp.zeros((D, COL_END - COL_LR - 2 * GLA_LOWRANK), w.dtype)
    return jnp.concatenate([qa, ka, va, qb, kb, vb, rb, qc, rep(kc), rep(vc), mga, mgb, mgc, lf, lb, pad],
                           axis=1).astype(BF16)


def _rope_tables(T, S):
    t = np.arange(T)
    row = (t // GRID_W).astype(np.float32)
    col = (t % GRID_W).astype(np.float32)
    lane = np.arange(HEAD_DIM)
    is_col = lane >= HEAD_DIM // 2
    fi = lane % 16
    inv_freq = ROPE_BASE ** (-jnp.arange(16, dtype=F32) / 16)
    pos = jnp.where(jnp.asarray(is_col)[None, :], jnp.asarray(col)[:, None], jnp.asarray(row)[:, None])
    ang = pos * inv_freq[jnp.asarray(fi)][None, :]
    cos = jnp.cos(ang)
    sin = jnp.sin(ang)
    sin = jnp.where(jnp.asarray((lane % 32) < 16)[None, :], -sin, sin)
    cos = jnp.concatenate([jnp.ones((S, HW), F32), jnp.tile(cos, (1, HW // HEAD_DIM))], axis=0)
    sin = jnp.concatenate([jnp.zeros((S, HW), F32), jnp.tile(sin, (1, HW // HEAD_DIM))], axis=0)
    return cos, sin


def kernel(x, c, ctx, c_ctx, w_ada, b_ada, norm1_g, norm2_g, w_in, na_rpb, gla_wg_f, gla_bg_f, gla_wg_b,
           gla_bg_b, gla_norm_g, gqa_qn_g, gqa_kn_g, w_branch_a, w_branch_b, w_branch_c, w_out, w_router,
           w_e_gate, w_e_up, w_e_down, final_norm_g):
    B, T, D = x.shape
    S = ctx.shape[1]
    L = w_ada.shape[0]
    E = w_router.shape[-1]
    P = S + T
    tm = min(256, S)
    NT = P // tm
    assert S % tm == 0 and T % tm == 0 and S % GRID_W == 0 and T % GRID_W == 0 and P % S == 0
    cap_l = EC_CAPACITY_FACTOR * T // E
    cap_c = EC_CAPACITY_FACTOR * S // E
    capt = cap_l + cap_c
    assert capt % SLOT_ALIGN == 0 and capt >= EC_WIN and 3 * E <= LANES and E % (LANES // EC_WIN) == 0
    R = T // GRID_W
    kh = min(NA_KH, R)

    seg = jnp.asarray(np.kron(np.eye(HW // HEAD_DIM), np.ones((HEAD_DIM, HEAD_DIM))), BF16)
    cos_t, sin_t = _rope_tables(T, S)
    ii = np.arange(GLA_CHUNK)
    tril = (ii[:, None] >= ii[None, :]).astype(np.float32)
    tri = jnp.asarray(np.stack([tril, tril.T]))
    tri4 = jnp.asarray(np.stack([np.tile(tril, (1, GLA_HEADS)), np.tile(tril.T, (1, GLA_HEADS))]))
    il = np.arange(LANES)
    upper = jnp.asarray(il[:, None] <= il[None, :], BF16)
    ind = jnp.asarray(np.arange(P)[:, None] < (np.arange(LANES) * tm)[None, :], BF16)

    RP = -(-(B + 1) // 8) * 8
    cv = jnp.concatenate([c, c_ctx[None], jnp.zeros((RP - B - 1, D), F32)], axis=0)
    mod = _adaln(cv, w_ada, b_ada)[:, :B + 1].reshape(L, B + 1, 6, 1, D)

    X = jnp.concatenate([ctx, x], axis=1).reshape(B * P, D)
    out = None
    for l in range(L):
        last = l == L - 1
        sh1, sc1, g1, sh2, sc2, g2 = [mod[l, :, i] for i in range(6)]
        w_p = _permute_w_in(w_in[l])
        qn = jnp.tile(gqa_qn_g[l], HW // HEAD_DIM)[None]
        kn = jnp.tile(gqa_kn_g[l], HW // HEAD_DIM)[None]
        A, Bq, C, G, LR = _proj_in(X, norm1_g[l][None], sc1, sh1, w_p, seg, qn, kn, cos_t, sin_t,
                                   B=B, S=S, T=T)

        YA = _na_attention(A, _na_bias_table(na_rpb[l], kh), B=B, S=S, T=T)
        YC = _gqa_attention(C, B=B, S=S, T=T, tq=min(256, S))
        YA, YC = _ctx_attention(A, C, YA, YC, B=B, S=S, T=T)

        zpad = jnp.zeros((LANES - 2 * GLA_LOWRANK, GLA_DK), F32)
        zlr = jnp.zeros((GLA_LOWRANK, GLA_DK), F32)
        wg_pad = jnp.stack([jnp.concatenate([gla_wg_f[l], zlr, zpad], axis=0),
                            jnp.concatenate([zlr, gla_wg_b[l], zpad], axis=0)]).astype(BF16)
        bg = jnp.stack([gla_bg_f[l], gla_bg_b[l]])[:, None, :]
        O = _gla(Bq, LR, wg_pad, bg, tri, tri4, B=B, S=S, T=T)

        gg = jnp.tile(gla_norm_g[l], GLA_HEADS)[None]
        X1, XN, AFF = _merge(YA, O, Bq, YC, G, X,
                             w_branch_a[l].astype(BF16), w_branch_b[l].astype(BF16),
                             w_branch_c[l].astype(BF16), w_out[l].astype(BF16),
                             seg, gg, g1, norm2_g[l][None], sc2, sh2, jnp.transpose(w_router[l]),
                             B=B, S=S, T=T, tm=tm)

        slot, wsel, off = _select(AFF, upper, ind, B=B, S=S, T=T, cap_c=cap_c, cap_l=cap_l)
        off_flat = off[:, :, :NT + 1].reshape(-1)
        slotT = jnp.swapaxes(slot, 1, 2).reshape(B * P, E)
        wT = jnp.swapaxes(wsel, 1, 2).reshape(B * P, E)
        WC = jnp.concatenate(list(_split3(wT)) + [jnp.zeros((B * P, LANES - 3 * E), BF16)], axis=1)
        XG, WG = _gather(off_flat, XN, WC, slot, B=B, E=E, S=S, T=T, tm=tm, capt=capt)
        rows = capt * (2 if B % 2 == 0 else 1)
        Y = _ffn(XG, WG, w_e_gate, w_e_up, w_e_down, l, rows=rows, fc=min(512, w_e_gate.shape[-1]))
        res = _combine(off_flat, Y, slotT, slot, X1, g2, final_norm_g[None],
                       B=B, E=E, S=S, T=T, tm=tm, capt=capt, final=last)
        if last:
            out = res
        else:
            X = res
    return out
```

```python
import functools

import jax
import jax.numpy as jnp
import numpy as np
from jax import lax
from jax.experimental import pallas as pl
from jax.experimental.pallas import tpu as pltpu

F32 = jnp.float32
BF16 = jnp.bfloat16
I32 = jnp.int32

GRID_W = 64
HEAD_DIM = 64
NORM_EPS = 1e-6
NA_HEADS = 4
NA_KH = 8
NA_KW = 16
GLA_HEADS = 4
GLA_DK = 128
GLA_DV = 256
GLA_LOWRANK = 16
GLA_TAU = 16.0
GLA_CHUNK = 64
GQA_HEADS = 4
GQA_KV_HEADS = 2
ROPE_BASE = 10000.0
N_EXPERTS = 16
EC_CAPACITY_FACTOR = 2
NEG_BIG = -1e30

LANES = 128
HW = 256
VMEM_LIMIT = 56 * 1024 * 1024

COL_A = 0
COL_B = 768
COL_C = 1536
COL_G = 2304
COL_LR = 5376
COL_END = 5504


def _cparams(n_grid):
    return pltpu.CompilerParams(dimension_semantics=("arbitrary",) * n_grid,
                                vmem_limit_bytes=VMEM_LIMIT)


def _resident(block_shape, index_map):
    return pl.BlockSpec(block_shape, index_map, pipeline_mode=pl.Buffered(1))


def _dot(a, b):
    return jnp.dot(a, b, preferred_element_type=F32)


def _dot_nt(a, b):
    return lax.dot_general(a, b, (((1,), (1,)), ((), ())), preferred_element_type=F32)


def _split2(a):
    hi = a.astype(BF16)
    lo = (a - hi.astype(F32)).astype(BF16)
    return hi, lo


def _split3(a):
    h1 = a.astype(BF16)
    r1 = a - h1.astype(F32)
    h2 = r1.astype(BF16)
    h3 = (r1 - h2.astype(F32)).astype(BF16)
    return h1, h2, h3


def _seg_meansq(x, seg_ref):
    hi, lo = _split2(x * x)
    seg = seg_ref[...]
    return (_dot(hi, seg) + _dot(lo, seg)) * (1.0 / HEAD_DIM)


def _sigmoid(x):
    return 0.5 * jnp.tanh(0.5 * x) + 0.5


def _silu(x):
    return x * _sigmoid(x)


def _adaln_kernel(cv_ref, w_ref, b_ref, o_ref):
    s = _silu(cv_ref[...])
    w = w_ref[0]
    s1, s2, s3 = _split3(s)
    w1, w2, w3 = _split3(w)
    acc = _dot(s1, w1) + (_dot(s1, w2) + _dot(s2, w1)) + (_dot(s2, w2) + _dot(s1, w3) + _dot(s3, w1))
    o_ref[0] = acc + b_ref[0]


def _adaln(cv, w_ada, b_ada):
    L, D, D6 = w_ada.shape
    R = cv.shape[0]
    tn = 1024
    return pl.pallas_call(
        _adaln_kernel,
        grid=(L, D6 // tn),
        in_specs=[pl.BlockSpec((R, D), lambda l, n: (0, 0)),
                  pl.BlockSpec((1, D, tn), lambda l, n: (l, 0, n)),
                  pl.BlockSpec((1, 1, tn), lambda l, n: (l, 0, n))],
        out_specs=pl.BlockSpec((1, R, tn), lambda l, n: (l, 0, n)),
        out_shape=jax.ShapeDtypeStruct((L, R, D6), F32),
        compiler_params=_cparams(2),
        name="adaln",
    )(cv, w_ada, b_ada.reshape(L, 1, D6))


def _swap16(y):
    lane = lax.broadcasted_iota(I32, y.shape, 1)
    first = (lane % 32) < 16
    return jnp.where(first, pltpu.roll(y, LANES - 16, 1), pltpu.roll(y, 16, 1))


def _norm_rope(x, seg_ref, g_ref, cos_ref, sin_ref, scale):
    y = x * lax.rsqrt(_seg_meansq(x, seg_ref) + NORM_EPS) * g_ref[...]
    halves = []
    for i in range(HW // LANES):
        sl = slice(i * LANES, (i + 1) * LANES)
        yh = y[:, sl]
        halves.append(yh * cos_ref[:, sl] + _swap16(yh) * sin_ref[:, sl])
    out = jnp.concatenate(halves, axis=1)
    return out * scale if scale != 1.0 else out


def _proj_in_kernel(x_ref, g_ref, sc_ref, sh_ref, scc_ref, shc_ref, w_ref, seg_ref, qn_ref, kn_ref,
                    cos_ref, sin_ref, a_ref, b_ref, c_ref, gt_ref, lr_ref, *, S):
    x = x_ref[...]
    tr = x.shape[0]
    is_ctx = (pl.program_id(1) * tr + lax.broadcasted_iota(I32, (tr, 1), 0)) < S
    sc = jnp.where(is_ctx, scc_ref[0], sc_ref[0])
    sh = jnp.where(is_ctx, shc_ref[0], sh_ref[0])
    y = x * lax.rsqrt(jnp.mean(x * x, axis=-1, keepdims=True) + NORM_EPS)
    u = (y * g_ref[...] * (1.0 + sc) + sh).astype(BF16)

    def proj(c0, c1):
        return _dot(u, w_ref[:, c0:c1])

    qscale = HEAD_DIM ** -0.5
    a_ref[:, 0:HW] = (proj(COL_A, COL_A + HW) * qscale).astype(BF16)
    a_ref[:, HW:3 * HW] = proj(COL_A + HW, COL_B).astype(BF16)
    b_ref[...] = proj(COL_B, COL_C).astype(BF16)
    qc = proj(COL_C, COL_C + HW)
    c_ref[:, 0:HW] = _norm_rope(qc, seg_ref, qn_ref, cos_ref, sin_ref, qscale).astype(BF16)
    kc = proj(COL_C + HW, COL_C + 2 * HW)
    c_ref[:, HW:2 * HW] = _norm_rope(kc, seg_ref, kn_ref, cos_ref, sin_ref, 1.0).astype(BF16)
    c_ref[:, 2 * HW:3 * HW] = proj(COL_C + 2 * HW, COL_G).astype(BF16)
    for i in range(3):
        gt_ref[:, i * 1024:(i + 1) * 1024] = proj(COL_G + i * 1024, COL_G + (i + 1) * 1024).astype(BF16)
    lr_ref[...] = proj(COL_LR, COL_END).astype(BF16)


def _row_tile(S, P):
    for k in (8, 4, 2, 1):
        if P % k == 0 and (P // k) % 16 == 0 and P // k >= S:
            return P // k
    raise ValueError("no row tile for these shapes")


def _proj_in(X, g, sc, sh, w_p, seg, qn, kn, cos_t, sin_t, *, B, S, T):
    N, D = X.shape
    P = S + T
    tm = _row_tile(S, P)
    NT = P // tm

    def row(b, j):
        return (b * NT + j, 0)

    grp = lambda b, j: (b, 0, 0)
    grp_c = lambda b, j: (B, 0, 0)
    tab = lambda b, j: (j, 0)
    const = lambda b, j: (0, 0)
    outs = [jax.ShapeDtypeStruct((N, 768), BF16), jax.ShapeDtypeStruct((N, 768), BF16),
            jax.ShapeDtypeStruct((N, 768), BF16), jax.ShapeDtypeStruct((N, 3072), BF16),
            jax.ShapeDtypeStruct((N, LANES), BF16)]
    return pl.pallas_call(
        functools.partial(_proj_in_kernel, S=S),
        grid=(B, NT),
        in_specs=[pl.BlockSpec((tm, D), row),
                  pl.BlockSpec((1, D), const),
                  pl.BlockSpec((1, 1, D), grp),
                  pl.BlockSpec((1, 1, D), grp),
                  pl.BlockSpec((1, 1, D), grp_c),
                  pl.BlockSpec((1, 1, D), grp_c),
                  _resident((D, COL_END), const),
                  pl.BlockSpec((HW, HW), const),
                  pl.BlockSpec((1, HW), const),
                  pl.BlockSpec((1, HW), const),
                  pl.BlockSpec((tm, HW), tab),
                  pl.BlockSpec((tm, HW), tab)],
        out_specs=[pl.BlockSpec((tm, 768), row), pl.BlockSpec((tm, 768), row),
                   pl.BlockSpec((tm, 768), row), pl.BlockSpec((tm, 3072), row),
                   pl.BlockSpec((tm, LANES), row)],
        out_shape=outs,
        compiler_params=_cparams(2),
        name="proj_in",
    )(X, g, sc, sh, sc, sh, w_p, seg, qn, kn, cos_t, sin_t)


def _mh_attend(q, segments):
    nq = q.shape[0]
    nh = HW // HEAD_DIM
    lane = lax.broadcasted_iota(I32, (1, HW), 1) // HEAD_DIM
    qs = jnp.concatenate([jnp.where(lane == h, q, jnp.zeros_like(q)) for h in range(nh)], axis=0)
    mx = den = o = None
    for k, v, bias in segments:
        s = _dot_nt(qs, k)
        if bias is not None:
            s = s + bias
        smax = s.max(axis=-1, keepdims=True)
        if mx is None:
            mx = smax
            p = jnp.exp(s - mx)
            den = p.sum(axis=-1, keepdims=True)
            o = _dot(p.astype(BF16), v)
        else:
            mx_new = jnp.maximum(mx, smax)
            alpha = jnp.exp(mx - mx_new)
            p = jnp.exp(s - mx_new)
            den = den * alpha + p.sum(axis=-1, keepdims=True)
            o = o * alpha + _dot(p.astype(BF16), v)
            mx = mx_new
    o = o / den
    acc = jnp.zeros(q.shape, F32)
    for h in range(nh):
        acc = acc + jnp.where(lane == h, o[h * nq:(h + 1) * nq], 0.0)
    return acc


def _na_kernel(q_ref, k_ref, v_ref, bias_ref, o_ref, *, S, R, kh, rb):
    for i in range(rb):
        r = pl.program_id(1) * rb + i
        r0 = jnp.clip(r - kh // 2, 0, R - kh)
        start = pl.multiple_of(S + r0 * GRID_W, GRID_W)
        kwin = k_ref[pl.ds(start, kh * GRID_W), :]
        vwin = v_ref[pl.ds(start, kh * GRID_W), :]
        segs = [(kwin, vwin, bias_ref[r - r0]),
                (k_ref[0:S, :], v_ref[0:S, :], None)]
        rows = slice(i * GRID_W, (i + 1) * GRID_W)
        o_ref[rows, :] = _mh_attend(q_ref[rows, :], segs).astype(BF16)


def _na_bias_table(rpb, kh):
    H = rpb.shape[0]
    W = GRID_W
    col = np.arange(W)
    col_start = np.clip(col - NA_KW // 2, 0, W - NA_KW)
    inwin = (col[None, :] >= col_start[:, None]) & (col[None, :] < col_start[:, None] + NA_KW)
    rp = jnp.pad(rpb.astype(F32), ((0, 0), (0, 0), (W, W)))
    toe = jnp.stack([rp[:, :, W + NA_KW - 1 - q:2 * W + NA_KW - 1 - q] for q in range(W)], axis=2)
    toe = jnp.where(jnp.asarray(inwin)[None, None], toe, NEG_BIG)
    t = jnp.stack([toe[:, NA_KH - 1 - c:NA_KH - 1 - c + kh] for c in range(kh)], axis=0)
    t = jnp.transpose(t, (0, 1, 3, 2, 4))
    return t.reshape(kh, H * W, kh * W)


def _na_attention(A, bias_t, *, B, S, T):
    N = A.shape[0]
    P = S + T
    R = T // GRID_W
    kh = min(NA_KH, R)
    rb = 4 if (R % 4 == 0 and S % (4 * GRID_W) == 0) else 1
    tq = rb * GRID_W
    qoff = S // tq
    per_b = P // tq

    return pl.pallas_call(
        functools.partial(_na_kernel, S=S, R=R, kh=kh, rb=rb),
        grid=(B, R // rb),
        in_specs=[pl.BlockSpec((tq, HW), lambda b, r: (b * per_b + qoff + r, 0)),
                  pl.BlockSpec((P, HW), lambda b, r: (b, 1)),
                  pl.BlockSpec((P, HW), lambda b, r: (b, 2)),
                  pl.BlockSpec((kh, NA_HEADS * GRID_W, kh * GRID_W), lambda b, r: (0, 0, 0))],
        out_specs=pl.BlockSpec((tq, HW), lambda b, r: (b * per_b + qoff + r, 0)),
        out_shape=jax.ShapeDtypeStruct((N, HW), BF16),
        compiler_params=_cparams(2),
        name="na_attention",
    )(A, A, A, bias_t)


def _key_chunks(n, target=1024, align=256):
    nchunks = max(1, n // target)
    base = (n // nchunks) // align * align
    if base == 0:
        return [(0, n)]
    bounds = [i * base for i in range(nchunks)] + [n]
    return [(bounds[i], bounds[i + 1]) for i in range(nchunks)]


def _gqa_kernel(q_ref, k_ref, v_ref, o_ref):
    segs = [(k_ref[c0:c1, :], v_ref[c0:c1, :], None) for c0, c1 in _key_chunks(k_ref.shape[0])]
    o_ref[...] = _mh_attend(q_ref[...], segs).astype(BF16)


def _gqa_attention(C, *, B, S, T, tq):
    N = C.shape[0]
    P = S + T
    nq = T // tq
    qoff = S // tq
    per_b = P // tq
    return pl.pallas_call(
        _gqa_kernel,
        grid=(B, nq),
        in_specs=[pl.BlockSpec((tq, HW), lambda b, i: (b * per_b + qoff + i, 0)),
                  pl.BlockSpec((P, HW), lambda b, i: (b, 1)),
                  pl.BlockSpec((P, HW), lambda b, i: (b, 2))],
        out_specs=pl.BlockSpec((tq, HW), lambda b, i: (b * per_b + qoff + i, 0)),
        out_shape=jax.ShapeDtypeStruct((N, HW), BF16),
        compiler_params=_cparams(2),
        name="gqa_attention",
    )(C, C, C)


def _ctx_attn_kernel(qa_ref, ka_ref, va_ref, qc_ref, kc_ref, vc_ref, ya_in, yc_in, ya_ref, yc_ref):
    del ya_in, yc_in
    ya_ref[...] = _mh_attend(qa_ref[...], [(ka_ref[...], va_ref[...], None)]).astype(BF16)
    yc_ref[...] = _mh_attend(qc_ref[...], [(kc_ref[...], vc_ref[...], None)]).astype(BF16)


def _ctx_attention(A, C, YA, YC, *, B, S, T):
    per_b = (S + T) // S
    blk = lambda c: pl.BlockSpec((S, HW), lambda b: (b * per_b, c))
    anyspec = pl.BlockSpec(memory_space=pl.ANY)
    return pl.pallas_call(
        _ctx_attn_kernel,
        grid=(B,),
        in_specs=[blk(0), blk(1), blk(2), blk(0), blk(1), blk(2), anyspec, anyspec],
        out_specs=[blk(0), blk(0)],
        out_shape=[jax.ShapeDtypeStruct(YA.shape, YA.dtype), jax.ShapeDtypeStruct(YC.shape, YC.dtype)],
        input_output_aliases={6: 0, 7: 1},
        compiler_params=_cparams(1),
        name="ctx_attention",
    )(A, A, A, C, C, C, YA, YC)


def _gla_kernel(bq_ref, lr_ref, wg_ref, bg_ref, tri_ref, tri4_ref, o_ref, g_scr, *, S, P):
    C = GLA_CHUNK
    ncs = S // C
    nc = P // C
    dk_h = GLA_DK // GLA_HEADS
    dv_h = GLA_DV // GLA_HEADS

    for d in range(2):
        z = _dot(lr_ref[...], wg_ref[d]) + bg_ref[d]
        g_scr[d] = (jnp.minimum(z, 0.0) - jnp.log(1.0 + jnp.exp(-jnp.abs(z)))) * (1.0 / GLA_TAU)

    lane_k = lax.broadcasted_iota(I32, (1, GLA_DK), 1) // dk_h
    lane_v = lax.broadcasted_iota(I32, (1, GLA_DV), 1) // dv_h
    bd = (lax.broadcasted_iota(I32, (GLA_DV, GLA_DK), 0) // dv_h
          == lax.broadcasted_iota(I32, (GLA_DV, GLA_DK), 1) // dk_h)

    def chunk(d, ci, st):
        tri = tri_ref[d].astype(BF16)
        tri4 = tri4_ref[d]
        r0 = pl.multiple_of(ci * C, C)
        q = bq_ref[pl.ds(r0, C), 0:GLA_DK].astype(F32)
        k = bq_ref[pl.ds(r0, C), GLA_DK:2 * GLA_DK].astype(F32)
        v = bq_ref[pl.ds(r0, C), 2 * GLA_DK:2 * GLA_DK + GLA_DV]
        g = g_scr[d, pl.ds(r0, C), :]
        g1, g2, g3 = _split3(g)
        bcum = _dot(tri, g1) + _dot(tri, g2) + _dot(tri, g3)
        btot = g.sum(axis=0, keepdims=True)
        q_dec = (q * (dk_h ** -0.5) * jnp.exp(bcum)).astype(BF16)
        k_inv = k * jnp.exp(-bcum)
        k_end = (k * jnp.exp(btot - bcum)).astype(BF16)
        k_stack = jnp.concatenate(
            [jnp.where(lane_k == h, k_inv, 0.0) for h in range(GLA_HEADS)], axis=0).astype(BF16)
        a_cat = _dot_nt(q_dec, k_stack) * tri4
        v_stack = jnp.concatenate(
            [jnp.where(lane_v == h, v, jnp.zeros_like(v)) for h in range(GLA_HEADS)], axis=0)
        o = _dot(a_cat.astype(BF16), v_stack) + _dot_nt(q_dec, st.astype(BF16))
        o_ref[d, pl.ds(r0, C), :] = o
        v_t = jnp.transpose(v.astype(F32)).astype(BF16)
        ds_t = _dot(v_t, k_end)
        return st * jnp.exp(btot) + jnp.where(bd, ds_t, 0.0)

    def body(n, sts):
        cb = jnp.where(n < ncs, ncs - 1 - n, nc - 1 - (n - ncs))
        return chunk(0, n, sts[0]), chunk(1, cb, sts[1])

    zero = jnp.zeros((GLA_DV, GLA_DK), F32)
    lax.fori_loop(0, nc, body, (zero, zero))


def _gla(Bq, LR, wg_pad, bg, tri, tri4, *, B, S, T):
    N = Bq.shape[0]
    P = S + T
    return pl.pallas_call(
        functools.partial(_gla_kernel, S=S, P=P),
        grid=(B,),
        in_specs=[pl.BlockSpec((P, 768), lambda b: (b, 0)),
                  pl.BlockSpec((P, LANES), lambda b: (b, 0)),
                  pl.BlockSpec((2, LANES, GLA_DK), lambda b: (0, 0, 0)),
                  pl.BlockSpec((2, 1, GLA_DK), lambda b: (0, 0, 0)),
                  pl.BlockSpec((2, GLA_CHUNK, GLA_CHUNK), lambda b: (0, 0, 0)),
                  pl.BlockSpec((2, GLA_CHUNK, GLA_HEADS * GLA_CHUNK), lambda b: (0, 0, 0))],
        out_specs=pl.BlockSpec((2, P, GLA_DV), lambda b: (0, b, 0)),
        out_shape=jax.ShapeDtypeStruct((2, N, GLA_DV), F32),
        scratch_shapes=[pltpu.VMEM((2, P, GLA_DK), F32)],
        compiler_params=_cparams(1),
        name="gla",
    )(Bq, LR, wg_pad, bg, tri, tri4)


def _merge_kernel(ya_ref, of_ref, ob_ref, rb_ref, yc_ref, gt_ref, x_ref,
                  wa_ref, wb_ref, wc_ref, wo_ref, seg_ref, gg_ref, n2_ref, wr_ref,
                  g1_ref, sc2_ref, sh2_ref, g1c_ref, sc2c_ref, sh2c_ref,
                  x1_ref, xn_ref, aff_ref, *, S):
    tr = x_ref.shape[0]
    first = pl.program_id(1) == 0
    parts = [(slice(0, S), jnp.where(first, g1c_ref[0], g1_ref[0]),
              jnp.where(first, sc2c_ref[0], sc2_ref[0]), jnp.where(first, sh2c_ref[0], sh2_ref[0]))]
    if S < tr:
        parts.append((slice(S, tr), g1_ref[0], sc2_ref[0], sh2_ref[0]))

    def by_rows(fn):
        return jnp.concatenate([fn(*p) for p in parts], axis=0)

    o = of_ref[0] + ob_ref[0]
    yb = o * lax.rsqrt(_seg_meansq(o, seg_ref) + NORM_EPS) * gg_ref[...]
    yb = (yb * _silu(rb_ref[...].astype(F32))).astype(BF16)

    def gated(i, d):
        return jnp.tanh(gt_ref[:, i * 1024:(i + 1) * 1024].astype(F32)) * d + d

    m2 = (gated(0, _dot(ya_ref[...], wa_ref[...]))
          + gated(1, _dot(yb, wb_ref[...]))
          + gated(2, _dot(yc_ref[...], wc_ref[...])))
    mo = _dot(m2.astype(BF16), wo_ref[...])
    x1 = by_rows(lambda r, g1, sc2, sh2: x_ref[r, :] + (0.5 * g1) * mo[r])
    x1_ref[...] = x1
    y = x1 * lax.rsqrt(jnp.mean(x1 * x1, axis=-1, keepdims=True) + NORM_EPS)
    xn = by_rows(lambda r, g1, sc2, sh2: y[r] * (n2_ref[...] * (1.0 + sc2)) + sh2)
    xn_ref[...] = xn.astype(BF16)
    xh, xl = _split2(xn)
    wh, wl = _split2(wr_ref[...])
    logits = _dot(xh, wh) + (_dot(xl, wh) + _dot(xh, wl))
    e = jnp.exp(logits - logits.max(axis=-1, keepdims=True))
    aff_ref[...] = e / e.sum(axis=-1, keepdims=True)


def _merge(YA, O, Bq, YC, G, X, wa, wb, wc, wo, seg, gg, g1, n2, sc2, sh2, wr, *, B, S, T):
    N, D = X.shape
    P = S + T
    tm = _row_tile(S, P)
    NT = P // tm
    E = wr.shape[1]

    def row(b, j):
        return (b * NT + j, 0)

    grp = lambda b, j: (b, 0, 0)
    grp_c = lambda b, j: (B, 0, 0)
    const = lambda b, j: (0, 0)
    mod = lambda m: pl.BlockSpec((1, 1, D), m)
    return pl.pallas_call(
        functools.partial(_merge_kernel, S=S),
        grid=(B, NT),
        in_specs=[pl.BlockSpec((tm, HW), row),
                  pl.BlockSpec((1, tm, HW), lambda b, j: (0, b * NT + j, 0)),
                  pl.BlockSpec((1, tm, HW), lambda b, j: (1, b * NT + j, 0)),
                  pl.BlockSpec((tm, HW), lambda b, j: (b * NT + j, 2)),
                  pl.BlockSpec((tm, HW), row),
                  pl.BlockSpec((tm, 3072), row),
                  pl.BlockSpec((tm, D), row),
                  pl.BlockSpec((HW, D), const), pl.BlockSpec((HW, D), const), pl.BlockSpec((HW, D), const),
                  pl.BlockSpec((D, D), const),
                  pl.BlockSpec((HW, HW), const),
                  pl.BlockSpec((1, HW), const),
                  pl.BlockSpec((1, D), const),
                  pl.BlockSpec((D, E), const),
                  mod(grp), mod(grp), mod(grp), mod(grp_c), mod(grp_c), mod(grp_c)],
        out_specs=[pl.BlockSpec((tm, D), row), pl.BlockSpec((tm, D), row), pl.BlockSpec((tm, E), row)],
        out_shape=[jax.ShapeDtypeStruct((N, D), F32), jax.ShapeDtypeStruct((N, D), BF16),
                   jax.ShapeDtypeStruct((N, E), F32)],
        compiler_params=_cparams(2),
        name="merge",
    )(YA, O, O, Bq, YC, G, X, wa, wb, wc, wo, seg, gg, n2, wr, g1, sc2, sh2, g1, sc2, sh2)


def _cumsum_lanes(m01, upper, n):
    outs = []
    carry = jnp.zeros((m01.shape[0], 1), F32)
    for c in range(n // LANES):
        blk = _dot(m01[:, c * LANES:(c + 1) * LANES], upper) + carry
        outs.append(blk)
        carry = blk[:, LANES - 1:LANES]
    return jnp.concatenate(outs, axis=1)


def _topk_mask(aff, cap, upper):
    n = aff.shape[1]
    key = lax.bitcast_convert_type(aff, I32)

    def step(i, theta):
        cand = theta | jnp.left_shift(jnp.int32(1), 30 - i)
        cnt = jnp.sum(jnp.where(key >= cand, 1.0, 0.0), axis=1, keepdims=True)
        return jnp.where(cnt >= cap, cand, theta)

    theta = lax.fori_loop(0, 31, step, jnp.zeros((aff.shape[0], 1), I32))
    gt = key > theta
    eq = key == theta
    need = cap - jnp.sum(jnp.where(gt, 1.0, 0.0), axis=1, keepdims=True)
    eq01 = jnp.where(eq, 1.0, 0.0)
    rank = _cumsum_lanes(eq01.astype(BF16), upper, n)
    take = eq & (rank <= need)
    return jnp.where(gt | take, 1.0, 0.0)


def _select_kernel(aff_ref, upper_ref, ind_ref, slot_ref, w_ref, off_ref, *, S, T, cap_c, cap_l):
    aff = aff_ref[...]
    upper = upper_ref[...]
    sel = jnp.concatenate([_topk_mask(aff[:, 0:S], cap_c, upper),
                           _topk_mask(aff[:, S:S + T], cap_l, upper)], axis=1)
    sel16 = sel.astype(BF16)
    pos = _cumsum_lanes(sel16, upper, S + T) - sel
    chosen = sel > 0.5
    slot_ref[0] = jnp.where(chosen, pos, -1.0).astype(I32)
    w_ref[0] = jnp.where(chosen, aff, 0.0)
    off_ref[0] = _dot(sel16, ind_ref[...]).astype(I32)


def _select(AFF, upper, ind, *, B, S, T, cap_c, cap_l):
    E = AFF.shape[0]
    P = S + T
    return pl.pallas_call(
        functools.partial(_select_kernel, S=S, T=T, cap_c=cap_c, cap_l=cap_l),
        grid=(B,),
        in_specs=[pl.BlockSpec((E, P), lambda b: (0, b)),
                  pl.BlockSpec((LANES, LANES), lambda b: (0, 0)),
                  pl.BlockSpec((P, LANES), lambda b: (0, 0))],
        out_specs=[pl.BlockSpec((1, E, P), lambda b: (b, 0, 0)),
                   pl.BlockSpec((1, E, P), lambda b: (b, 0, 0)),
                   pl.BlockSpec((1, E, LANES), lambda b: (b, 0, 0))],
        out_shape=[jax.ShapeDtypeStruct((B, E, P), I32), jax.ShapeDtypeStruct((B, E, P), F32),
                   jax.ShapeDtypeStruct((B, E, LANES), I32)],
        compiler_params=_cparams(1),
        name="ec_select",
    )(AFF, upper, ind)


EC_WIN = 64
SLOT_ALIGN = 16


def _win_base(lo, capt):
    return pl.multiple_of(jnp.minimum(lo // SLOT_ALIGN * SLOT_ALIGN, capt - EC_WIN), SLOT_ALIGN)


def _n_windows(tm):
    return -(-(tm + SLOT_ALIGN - 1) // EC_WIN)


def _any_overflow(offs, bases, E):
    over = None
    for e in range(E):
        c = offs(e)[1] > bases[e] + EC_WIN
        over = c if over is None else (over | c)
    return over


def _gather_kernel(off_ref, x_ref, wc_ref, slot_ref, xg_ref, wg_ref, *, E, NT, tm, capt):
    b = pl.program_id(0)
    j = pl.program_id(1)
    W = EC_WIN

    @pl.when(j == 0)
    def _():
        xg_ref[...] = jnp.zeros_like(xg_ref)
        wg_ref[...] = jnp.zeros_like(wg_ref)

    def offs(e):
        i = (b * E + e) * (NT + 1) + j
        return off_ref[i], off_ref[i + 1]

    rows = lax.broadcasted_iota(I32, (W, tm), 0)
    x = x_ref[...]
    wc = wc_ref[...]
    bases = [_win_base(offs(e)[0], capt) for e in range(E)]
    lhs = jnp.concatenate(
        [jnp.where(rows == slot_ref[0, e:e + 1, :] - bases[e], 1.0, 0.0).astype(BF16) for e in range(E)], axis=0)
    rx = _dot(lhs, x).astype(BF16)
    rw = _dot(lhs, wc)
    for e in range(E):
        xg_ref[e, pl.ds(bases[e], W), :] += rx[e * W:(e + 1) * W]
        wg_ref[e, pl.ds(bases[e], W), :] += rw[e * W:(e + 1) * W]

    def extra(e, carry):
        lo, hi = offs(e)
        base = _win_base(lo, capt)
        for w in range(1, _n_windows(tm)):
            first = base + w * W

            @pl.when(hi > first)
            def _():
                start = pl.multiple_of(jnp.minimum(first, capt - W), SLOT_ALIGN)
                s = slot_ref[0, pl.ds(e, 1), :]
                onehot = jnp.where(rows == jnp.where(s >= first, s - start, -1), 1.0, 0.0).astype(BF16)
                xg_ref[e, pl.ds(start, W), :] += _dot(onehot, x).astype(BF16)
                wg_ref[e, pl.ds(start, W), :] += _dot(onehot, wc)
        return carry

    @pl.when(_any_overflow(offs, bases, E))
    def _():
        lax.fori_loop(0, E, extra, 0)


def _gather(off, XN, WC, slot, *, B, E, S, T, tm, capt):
    N, D = XN.shape
    P = S + T
    NT = P // tm
    row = lambda b, j, off: (b * NT + j, 0)
    return pl.pallas_call(
        functools.partial(_gather_kernel, E=E, NT=NT, tm=tm, capt=capt),
        grid_spec=pltpu.PrefetchScalarGridSpec(
            num_scalar_prefetch=1,
            grid=(B, NT),
            in_specs=[pl.BlockSpec((tm, D), row),
                      pl.BlockSpec((tm, LANES), row),
                      pl.BlockSpec((1, E, tm), lambda b, j, off: (b, 0, j))],
            out_specs=[pl.BlockSpec((E, capt, D), lambda b, j, off: (0, b, 0)),
                       pl.BlockSpec((E, capt, LANES), lambda b, j, off: (0, b, 0))]),
        out_shape=[jax.ShapeDtypeStruct((E, B * capt, D), BF16),
                   jax.ShapeDtypeStruct((E, B * capt, LANES), F32)],
        compiler_params=_cparams(2),
        name="ec_gather",
    )(off, XN, WC, slot)


def _ffn_kernel(x_ref, val_ref, wg_ref, wu_ref, wd_ref, o_ref, acc_ref, *, E):
    f = pl.program_id(2)

    @pl.when(f == 0)
    def _():
        acc_ref[...] = jnp.zeros_like(acc_ref)

    x = x_ref[0]
    g = _dot(x, wg_ref[0, 0].astype(BF16))
    u = _dot(x, wu_ref[0, 0].astype(BF16))
    h = (_silu(g) * u).astype(BF16)
    acc_ref[...] += _dot(h, wd_ref[0, 0].astype(BF16))

    @pl.when(f == pl.num_programs(2) - 1)
    def _():
        lane = lax.broadcasted_iota(I32, (1, LANES), 1)
        mine = (lane % E == pl.program_id(0)) & (lane < 3 * E)
        val = jnp.sum(jnp.where(mine, val_ref[0], 0.0), axis=1, keepdims=True)
        o_ref[0] = (acc_ref[...] * val).astype(BF16)


def _ffn(XG, WG, w_gate, w_up, w_down, l, *, rows, fc):
    E, R, D = XG.shape
    F = w_gate.shape[-1]
    return pl.pallas_call(
        functools.partial(_ffn_kernel, E=E),
        grid=(E, R // rows, F // fc),
        in_specs=[pl.BlockSpec((1, rows, D), lambda e, r, f: (e, r, 0)),
                  pl.BlockSpec((1, rows, LANES), lambda e, r, f: (e, r, 0)),
                  pl.BlockSpec((1, 1, D, fc), lambda e, r, f: (l, e, 0, f)),
                  pl.BlockSpec((1, 1, D, fc), lambda e, r, f: (l, e, 0, f)),
                  pl.BlockSpec((1, 1, fc, D), lambda e, r, f: (l, e, f, 0))],
        out_specs=pl.BlockSpec((1, rows, D), lambda e, r, f: (e, r, 0)),
        out_shape=jax.ShapeDtypeStruct((E, R, D), BF16),
        scratch_shapes=[pltpu.VMEM((rows, D), F32)],
        compiler_params=_cparams(3),
        name="ec_ffn",
    )(XG, WG, w_gate, w_up, w_down)


def _combine_kernel(off_ref, y_ref, slott_ref, slot_ref, x_ref, g2_ref, fg_ref, o_ref, acc_ref,
                    *, E, NT, j0, tm, capt, final):
    b = pl.program_id(0)
    j = pl.program_id(1) + j0
    W = EC_WIN
    per_piece = LANES // W

    def offs(e):
        i = (b * E + e) * (NT + 1) + j
        return off_ref[i], off_ref[i + 1]

    cols = lax.broadcasted_iota(I32, (tm, LANES), 1)
    bases = [_win_base(offs(e)[0], capt) for e in range(E)]
    pieces = []
    for p0 in range(0, E, per_piece):
        hit = None
        for i in range(per_piece):
            e = p0 + i
            rel = slott_ref[:, e:e + 1] - bases[e]
            rel = jnp.where((rel >= 0) & (rel < W), rel + i * W, -1)
            m = cols == rel
            hit = m if hit is None else (hit | m)
        pieces.append(jnp.where(hit, 1.0, 0.0).astype(BF16))
    lhs = jnp.concatenate(pieces, axis=1)
    rhs = jnp.concatenate([y_ref[e, pl.ds(bases[e], W), :] for e in range(E)], axis=0)
    acc_ref[...] = _dot(lhs, rhs)

    rows = lax.broadcasted_iota(I32, (W, tm), 0)

    def extra(e, carry):
        lo, hi = offs(e)
        base = _win_base(lo, capt)
        for w in range(1, _n_windows(tm)):
            first = base + w * W

            @pl.when(hi > first)
            def _():
                start = pl.multiple_of(jnp.minimum(first, capt - W), SLOT_ALIGN)
                s = slot_ref[0, pl.ds(e, 1), :]
                onehot_t = jnp.where(rows == jnp.where(s >= first, s - start, -1), 1.0, 0.0)
                onehot = jnp.transpose(onehot_t).astype(BF16)
                acc_ref[...] += _dot(onehot, y_ref[e, pl.ds(start, W), :])
        return carry

    @pl.when(_any_overflow(offs, bases, E))
    def _():
        lax.fori_loop(0, E, extra, 0)
    x2 = x_ref[...] + g2_ref[0] * acc_ref[...]
    if final:
        x2 = x2 * lax.rsqrt(jnp.mean(x2 * x2, axis=-1, keepdims=True) + NORM_EPS) * fg_ref[...]
    if final:
        o_ref[0] = x2
    else:
        o_ref[...] = x2


def _combine(off, Y, slotT, slot, X1, g2, fg, *, B, E, S, T, tm, capt, final):
    N, D = X1.shape
    P = S + T
    NT = P // tm
    ns = S // tm
    j0 = ns if final else 0

    def row(b, j, off):
        return (b * NT + j + j0, 0)

    if final:
        out_spec = pl.BlockSpec((1, tm, D), lambda b, j, off: (b, j, 0))
        out_shape = jax.ShapeDtypeStruct((B, T, D), F32)
    else:
        out_spec = pl.BlockSpec((tm, D), row)
        out_shape = jax.ShapeDtypeStruct((N, D), F32)
    return pl.pallas_call(
        functools.partial(_combine_kernel, E=E, NT=NT, j0=j0, tm=tm, capt=capt, final=final),
        grid_spec=pltpu.PrefetchScalarGridSpec(
            num_scalar_prefetch=1,
            grid=(B, NT - j0),
            in_specs=[_resident((E, capt, D), lambda b, j, off: (0, b, 0)),
                      pl.BlockSpec((tm, E), row),
                      pl.BlockSpec((1, E, tm), lambda b, j, off: (b, 0, j + j0)),
                      pl.BlockSpec((tm, D), row),
                      pl.BlockSpec((1, 1, D), lambda b, j, off: (jnp.where(j + j0 < ns, B, b), 0, 0)),
                      pl.BlockSpec((1, D), lambda b, j, off: (0, 0))],
            out_specs=out_spec,
            scratch_shapes=[pltpu.VMEM((tm, D), F32)]),
        out_shape=out_shape,
        compiler_params=_cparams(2),
        name="ec_combine",
    )(off, Y, slotT, slot, X1, g2, fg)


def _permute_w_in(w):
    D = w.shape[0]
    o = np.cumsum([0, 256, 256, 256, 128, 128, 256, 256, 16, 16, 256, 128, 128, 1024, 1024, 1024])
    qa, ka, va, qb, kb, vb, rb, lf, lb, qc, kc, vc, mga, mgb, mgc = [w[:, o[i]:o[i + 1]] for i in range(15)]
    grp = GQA_HEADS // GQA_KV_HEADS

    def rep(t):
        return jnp.concatenate([t[:, (h // grp) * HEAD_DIM:(h // grp + 1) * HEAD_DIM] for h in range(GQA_HEADS)], axis=1)

    pad = jnp.zeros((D, COL_END - COL_LR - 2 * GLA_LOWRANK), w.dtype)
    gates = 0.5 * jnp.concatenate([mga, mgb, mgc], axis=1)
    return jnp.concatenate([qa, ka, va, qb, kb, vb, rb, qc, rep(kc), rep(vc), gates, lf, lb, pad],
                           axis=1).astype(BF16)


def _rope_tables(T, S):
    t = np.arange(T)
    row = (t // GRID_W).astype(np.float32)
    col = (t % GRID_W).astype(np.float32)
    lane = np.arange(HEAD_DIM)
    is_col = lane >= HEAD_DIM // 2
    fi = lane % 16
    inv_freq = ROPE_BASE ** (-jnp.arange(16, dtype=F32) / 16)
    pos = jnp.where(jnp.asarray(is_col)[None, :], jnp.asarray(col)[:, None], jnp.asarray(row)[:, None])
    ang = pos * inv_freq[jnp.asarray(fi)][None, :]
    cos = jnp.cos(ang)
    sin = jnp.sin(ang)
    sin = jnp.where(jnp.asarray((lane % 32) < 16)[None, :], -sin, sin)
    cos = jnp.concatenate([jnp.ones((S, HW), F32), jnp.tile(cos, (1, HW // HEAD_DIM))], axis=0)
    sin = jnp.concatenate([jnp.zeros((S, HW), F32), jnp.tile(sin, (1, HW // HEAD_DIM))], axis=0)
    return cos, sin


def kernel(x, c, ctx, c_ctx, w_ada, b_ada, norm1_g, norm2_g, w_in, na_rpb, gla_wg_f, gla_bg_f, gla_wg_b,
           gla_bg_b, gla_norm_g, gqa_qn_g, gqa_kn_g, w_branch_a, w_branch_b, w_branch_c, w_out, w_router,
           w_e_gate, w_e_up, w_e_down, final_norm_g):
    B, T, D = x.shape
    S = ctx.shape[1]
    L = w_ada.shape[0]
    E = w_router.shape[-1]
    P = S + T
    tm = min(256, S)
    NT = P // tm
    assert S % tm == 0 and T % tm == 0 and S % GRID_W == 0 and T % GRID_W == 0 and P % S == 0
    cap_l = EC_CAPACITY_FACTOR * T // E
    cap_c = EC_CAPACITY_FACTOR * S // E
    capt = cap_l + cap_c
    assert capt % SLOT_ALIGN == 0 and capt >= EC_WIN and 3 * E <= LANES and E % (LANES // EC_WIN) == 0
    R = T // GRID_W
    kh = min(NA_KH, R)

    seg = jnp.asarray(np.kron(np.eye(HW // HEAD_DIM), np.ones((HEAD_DIM, HEAD_DIM))), BF16)
    cos_t, sin_t = _rope_tables(T, S)
    ii = np.arange(GLA_CHUNK)
    tril = (ii[:, None] >= ii[None, :]).astype(np.float32)
    tri = jnp.asarray(np.stack([tril, tril.T]))
    tri4 = jnp.asarray(np.stack([np.tile(tril, (1, GLA_HEADS)), np.tile(tril.T, (1, GLA_HEADS))]))
    il = np.arange(LANES)
    upper = jnp.asarray(il[:, None] <= il[None, :], BF16)
    ind = jnp.asarray(np.arange(P)[:, None] < (np.arange(LANES) * tm)[None, :], BF16)

    RP = -(-(B + 1) // 8) * 8
    cv = jnp.concatenate([c, c_ctx[None], jnp.zeros((RP - B - 1, D), F32)], axis=0)
    mod = _adaln(cv, w_ada, b_ada)[:, :B + 1].reshape(L, B + 1, 6, 1, D)

    X = jnp.concatenate([ctx, x], axis=1).reshape(B * P, D)
    out = None
    for l in range(L):
        last = l == L - 1
        sh1, sc1, g1, sh2, sc2, g2 = [mod[l, :, i] for i in range(6)]
        w_p = _permute_w_in(w_in[l])
        qn = jnp.tile(gqa_qn_g[l], HW // HEAD_DIM)[None]
        kn = jnp.tile(gqa_kn_g[l], HW // HEAD_DIM)[None]
        A, Bq, C, G, LR = _proj_in(X, norm1_g[l][None], sc1, sh1, w_p, seg, qn, kn, cos_t, sin_t,
                                   B=B, S=S, T=T)

        YA = _na_attention(A, _na_bias_table(na_rpb[l], kh), B=B, S=S, T=T)
        YC = _gqa_attention(C, B=B, S=S, T=T, tq=min(256, S))
        YA, YC = _ctx_attention(A, C, YA, YC, B=B, S=S, T=T)

        zpad = jnp.zeros((LANES - 2 * GLA_LOWRANK, GLA_DK), F32)
        zlr = jnp.zeros((GLA_LOWRANK, GLA_DK), F32)
        wg_pad = jnp.stack([jnp.concatenate([gla_wg_f[l], zlr, zpad], axis=0),
                            jnp.concatenate([zlr, gla_wg_b[l], zpad], axis=0)]).astype(BF16)
        bg = jnp.stack([gla_bg_f[l], gla_bg_b[l]])[:, None, :]
        O = _gla(Bq, LR, wg_pad, bg, tri, tri4, B=B, S=S, T=T)

        gg = jnp.tile(gla_norm_g[l], GLA_HEADS)[None]
        X1, XN, AFFT = _merge(YA, O, Bq, YC, G, X,
                              w_branch_a[l].astype(BF16), w_branch_b[l].astype(BF16),
                              w_branch_c[l].astype(BF16), w_out[l].astype(BF16),
                              seg, gg, g1, norm2_g[l][None], sc2, sh2, w_router[l],
                              B=B, S=S, T=T)

        slot, wsel, off = _select(jnp.transpose(AFFT), upper, ind, B=B, S=S, T=T, cap_c=cap_c, cap_l=cap_l)
        off_flat = off[:, :, :NT + 1].reshape(-1)
        slotT = jnp.swapaxes(slot, 1, 2).reshape(B * P, E)
        wT = jnp.swapaxes(wsel, 1, 2).reshape(B * P, E)
        WC = jnp.concatenate(list(_split3(wT)) + [jnp.zeros((B * P, LANES - 3 * E), BF16)], axis=1)
        XG, WG = _gather(off_flat, XN, WC, slot, B=B, E=E, S=S, T=T, tm=tm, capt=capt)
        rows = capt * (2 if B % 2 == 0 else 1)
        Y = _ffn(XG, WG, w_e_gate, w_e_up, w_e_down, l, rows=rows, fc=min(512, w_e_gate.shape[-1]))
        res = _combine(off_flat, Y, slotT, slot, X1, g2, final_norm_g[None],
                       B=B, E=E, S=S, T=T, tm=tm, capt=capt, final=last)
        if last:
            out = res
        else:
            X = res
    return out
```

```python
import functools

import jax
import jax.numpy as jnp
import numpy as np
from jax import lax
from jax.experimental import pallas as pl
from jax.experimental.pallas import tpu as pltpu

F32 = jnp.float32
BF16 = jnp.bfloat16
I32 = jnp.int32

GRID_W = 64
HEAD_DIM = 64
NORM_EPS = 1e-6
NA_HEADS = 4
NA_KH = 8
NA_KW = 16
GLA_HEADS = 4
GLA_DK = 128
GLA_DV = 256
GLA_LOWRANK = 16
GLA_TAU = 16.0
GLA_CHUNK = 64
GQA_HEADS = 4
GQA_KV_HEADS = 2
ROPE_BASE = 10000.0
N_EXPERTS = 16
EC_CAPACITY_FACTOR = 2
NEG_BIG = -1e30

LANES = 128
HW = 256
VMEM_LIMIT = 56 * 1024 * 1024

COL_A = 0
COL_B = 768
COL_C = 1536
COL_G = 2304
COL_LR = 5376
COL_END = 5504


def _cparams(n_grid):
    return pltpu.CompilerParams(dimension_semantics=("arbitrary",) * n_grid,
                                vmem_limit_bytes=VMEM_LIMIT)


def _resident(block_shape, index_map):
    return pl.BlockSpec(block_shape, index_map, pipeline_mode=pl.Buffered(1))


def _dot(a, b):
    return jnp.dot(a, b, preferred_element_type=F32)


def _dot_nt(a, b):
    return lax.dot_general(a, b, (((1,), (1,)), ((), ())), preferred_element_type=F32)


def _split2(a):
    hi = a.astype(BF16)
    lo = (a - hi.astype(F32)).astype(BF16)
    return hi, lo


def _split3(a):
    h1 = a.astype(BF16)
    r1 = a - h1.astype(F32)
    h2 = r1.astype(BF16)
    h3 = (r1 - h2.astype(F32)).astype(BF16)
    return h1, h2, h3


def _seg_meansq(x, seg_ref):
    hi, lo = _split2(x * x)
    seg = seg_ref[...]
    return (_dot(hi, seg) + _dot(lo, seg)) * (1.0 / HEAD_DIM)


def _sigmoid(x):
    return 0.5 * jnp.tanh(0.5 * x) + 0.5


def _silu(x):
    return x * _sigmoid(x)


def _adaln_kernel(cv_ref, w_ref, b_ref, o_ref):
    s = _silu(cv_ref[...])
    w = w_ref[0]
    s1, s2, s3 = _split3(s)
    w1, w2, w3 = _split3(w)
    acc = _dot(s1, w1) + (_dot(s1, w2) + _dot(s2, w1)) + (_dot(s2, w2) + _dot(s1, w3) + _dot(s3, w1))
    o_ref[0] = acc + b_ref[0]


def _adaln(cv, w_ada, b_ada):
    L, D, D6 = w_ada.shape
    R = cv.shape[0]
    tn = 1024
    return pl.pallas_call(
        _adaln_kernel,
        grid=(L, D6 // tn),
        in_specs=[pl.BlockSpec((R, D), lambda l, n: (0, 0)),
                  pl.BlockSpec((1, D, tn), lambda l, n: (l, 0, n)),
                  pl.BlockSpec((1, 1, tn), lambda l, n: (l, 0, n))],
        out_specs=pl.BlockSpec((1, R, tn), lambda l, n: (l, 0, n)),
        out_shape=jax.ShapeDtypeStruct((L, R, D6), F32),
        compiler_params=_cparams(2),
        name="adaln",
    )(cv, w_ada, b_ada.reshape(L, 1, D6))


def _swap16(y):
    lane = lax.broadcasted_iota(I32, y.shape, 1)
    first = (lane % 32) < 16
    return jnp.where(first, pltpu.roll(y, LANES - 16, 1), pltpu.roll(y, 16, 1))


def _norm_rope(x, seg_ref, g_ref, cos_ref, sin_ref, scale):
    y = x * lax.rsqrt(_seg_meansq(x, seg_ref) + NORM_EPS) * g_ref[...]
    halves = []
    for i in range(HW // LANES):
        sl = slice(i * LANES, (i + 1) * LANES)
        yh = y[:, sl]
        halves.append(yh * cos_ref[:, sl] + _swap16(yh) * sin_ref[:, sl])
    out = jnp.concatenate(halves, axis=1)
    return out * scale if scale != 1.0 else out


def _proj_in_kernel(x_ref, g_ref, sc_ref, sh_ref, scc_ref, shc_ref, w_ref, seg_ref, qn_ref, kn_ref,
                    cos_ref, sin_ref, a_ref, b_ref, c_ref, gt_ref, lr_ref, *, S):
    x = x_ref[...]
    tr = x.shape[0]
    is_ctx = (pl.program_id(1) * tr + lax.broadcasted_iota(I32, (tr, 1), 0)) < S
    sc = jnp.where(is_ctx, scc_ref[0], sc_ref[0])
    sh = jnp.where(is_ctx, shc_ref[0], sh_ref[0])
    y = x * lax.rsqrt(jnp.mean(x * x, axis=-1, keepdims=True) + NORM_EPS)
    u = (y * g_ref[...] * (1.0 + sc) + sh).astype(BF16)

    def proj(c0, c1):
        return _dot(u, w_ref[:, c0:c1])

    qscale = HEAD_DIM ** -0.5
    a_ref[:, 0:HW] = (proj(COL_A, COL_A + HW) * qscale).astype(BF16)
    a_ref[:, HW:3 * HW] = proj(COL_A + HW, COL_B).astype(BF16)
    b_ref[...] = proj(COL_B, COL_C).astype(BF16)
    qc = proj(COL_C, COL_C + HW)
    c_ref[:, 0:HW] = _norm_rope(qc, seg_ref, qn_ref, cos_ref, sin_ref, qscale).astype(BF16)
    kc = proj(COL_C + HW, COL_C + 2 * HW)
    c_ref[:, HW:2 * HW] = _norm_rope(kc, seg_ref, kn_ref, cos_ref, sin_ref, 1.0).astype(BF16)
    c_ref[:, 2 * HW:3 * HW] = proj(COL_C + 2 * HW, COL_G).astype(BF16)
    for i in range(3):
        gt_ref[:, i * 1024:(i + 1) * 1024] = proj(COL_G + i * 1024, COL_G + (i + 1) * 1024).astype(BF16)
    lr_ref[...] = proj(COL_LR, COL_END).astype(BF16)


def _row_tile(S, P):
    for k in (8, 4, 2, 1):
        if P % k == 0 and (P // k) % 16 == 0 and P // k >= S:
            return P // k
    raise ValueError("no row tile for these shapes")


def _proj_in(X, g, sc, sh, w_p, seg, qn, kn, cos_t, sin_t, *, B, S, T):
    N, D = X.shape
    P = S + T
    tm = _row_tile(S, P)
    NT = P // tm

    def row(b, j):
        return (b * NT + j, 0)

    grp = lambda b, j: (b, 0, 0)
    grp_c = lambda b, j: (B, 0, 0)
    tab = lambda b, j: (j, 0)
    const = lambda b, j: (0, 0)
    outs = [jax.ShapeDtypeStruct((N, 768), BF16), jax.ShapeDtypeStruct((N, 768), BF16),
            jax.ShapeDtypeStruct((N, 768), BF16), jax.ShapeDtypeStruct((N, 3072), BF16),
            jax.ShapeDtypeStruct((N, LANES), BF16)]
    return pl.pallas_call(
        functools.partial(_proj_in_kernel, S=S),
        grid=(B, NT),
        in_specs=[pl.BlockSpec((tm, D), row),
                  pl.BlockSpec((1, D), const),
                  pl.BlockSpec((1, 1, D), grp),
                  pl.BlockSpec((1, 1, D), grp),
                  pl.BlockSpec((1, 1, D), grp_c),
                  pl.BlockSpec((1, 1, D), grp_c),
                  _resident((D, COL_END), const),
                  pl.BlockSpec((HW, HW), const),
                  pl.BlockSpec((1, HW), const),
                  pl.BlockSpec((1, HW), const),
                  pl.BlockSpec((tm, HW), tab),
                  pl.BlockSpec((tm, HW), tab)],
        out_specs=[pl.BlockSpec((tm, 768), row), pl.BlockSpec((tm, 768), row),
                   pl.BlockSpec((tm, 768), row), pl.BlockSpec((tm, 3072), row),
                   pl.BlockSpec((tm, LANES), row)],
        out_shape=outs,
        compiler_params=_cparams(2),
        name="proj_in",
    )(X, g, sc, sh, sc, sh, w_p, seg, qn, kn, cos_t, sin_t)


def _mh_attend(q, segments):
    nq = q.shape[0]
    nh = HW // HEAD_DIM
    lane = lax.broadcasted_iota(I32, (1, HW), 1) // HEAD_DIM
    qs = jnp.concatenate([jnp.where(lane == h, q, jnp.zeros_like(q)) for h in range(nh)], axis=0)
    mx = den = o = None
    for k, v, bias in segments:
        s = _dot_nt(qs, k)
        if bias is not None:
            s = s + bias
        smax = s.max(axis=-1, keepdims=True)
        if mx is None:
            mx = smax
            p = jnp.exp(s - mx)
            den = p.sum(axis=-1, keepdims=True)
            o = _dot(p.astype(BF16), v)
        else:
            mx_new = jnp.maximum(mx, smax)
            alpha = jnp.exp(mx - mx_new)
            p = jnp.exp(s - mx_new)
            den = den * alpha + p.sum(axis=-1, keepdims=True)
            o = o * alpha + _dot(p.astype(BF16), v)
            mx = mx_new
    o = o / den
    acc = jnp.zeros(q.shape, F32)
    for h in range(nh):
        acc = acc + jnp.where(lane == h, o[h * nq:(h + 1) * nq], 0.0)
    return acc


def _na_kernel(q_ref, k_ref, v_ref, bias_ref, o_ref, *, S, R, kh, rb):
    for i in range(rb):
        r = pl.program_id(1) * rb + i
        r0 = jnp.clip(r - kh // 2, 0, R - kh)
        start = pl.multiple_of(S + r0 * GRID_W, GRID_W)
        kwin = k_ref[pl.ds(start, kh * GRID_W), :]
        vwin = v_ref[pl.ds(start, kh * GRID_W), :]
        segs = [(kwin, vwin, bias_ref[r - r0]),
                (k_ref[0:S, :], v_ref[0:S, :], None)]
        rows = slice(i * GRID_W, (i + 1) * GRID_W)
        o_ref[rows, :] = _mh_attend(q_ref[rows, :], segs).astype(BF16)


def _na_bias_table(rpb, kh):
    H = rpb.shape[0]
    W = GRID_W
    col = np.arange(W)
    col_start = np.clip(col - NA_KW // 2, 0, W - NA_KW)
    inwin = (col[None, :] >= col_start[:, None]) & (col[None, :] < col_start[:, None] + NA_KW)
    rp = jnp.pad(rpb.astype(F32), ((0, 0), (0, 0), (W, W)))
    toe = jnp.stack([rp[:, :, W + NA_KW - 1 - q:2 * W + NA_KW - 1 - q] for q in range(W)], axis=2)
    toe = jnp.where(jnp.asarray(inwin)[None, None], toe, NEG_BIG)
    t = jnp.stack([toe[:, NA_KH - 1 - c:NA_KH - 1 - c + kh] for c in range(kh)], axis=0)
    t = jnp.transpose(t, (0, 1, 3, 2, 4))
    return t.reshape(kh, H * W, kh * W)


def _na_attention(A, bias_t, *, B, S, T):
    N = A.shape[0]
    P = S + T
    R = T // GRID_W
    kh = min(NA_KH, R)
    rb = 4 if (R % 4 == 0 and S % (4 * GRID_W) == 0) else 1
    tq = rb * GRID_W
    qoff = S // tq
    per_b = P // tq

    return pl.pallas_call(
        functools.partial(_na_kernel, S=S, R=R, kh=kh, rb=rb),
        grid=(B, R // rb),
        in_specs=[pl.BlockSpec((tq, HW), lambda b, r: (b * per_b + qoff + r, 0)),
                  pl.BlockSpec((P, HW), lambda b, r: (b, 1)),
                  pl.BlockSpec((P, HW), lambda b, r: (b, 2)),
                  pl.BlockSpec((kh, NA_HEADS * GRID_W, kh * GRID_W), lambda b, r: (0, 0, 0))],
        out_specs=pl.BlockSpec((tq, HW), lambda b, r: (b * per_b + qoff + r, 0)),
        out_shape=jax.ShapeDtypeStruct((N, HW), BF16),
        compiler_params=_cparams(2),
        name="na_attention",
    )(A, A, A, bias_t)


def _key_chunks(n, target=1024, align=256):
    nchunks = max(1, n // target)
    base = (n // nchunks) // align * align
    if base == 0:
        return [(0, n)]
    bounds = [i * base for i in range(nchunks)] + [n]
    return [(bounds[i], bounds[i + 1]) for i in range(nchunks)]


def _gqa_kernel(q_ref, k_ref, v_ref, o_ref):
    segs = [(k_ref[c0:c1, :], v_ref[c0:c1, :], None) for c0, c1 in _key_chunks(k_ref.shape[0])]
    o_ref[...] = _mh_attend(q_ref[...], segs).astype(BF16)


def _gqa_attention(C, *, B, S, T, tq):
    N = C.shape[0]
    P = S + T
    nq = T // tq
    qoff = S // tq
    per_b = P // tq
    return pl.pallas_call(
        _gqa_kernel,
        grid=(B, nq),
        in_specs=[pl.BlockSpec((tq, HW), lambda b, i: (b * per_b + qoff + i, 0)),
                  pl.BlockSpec((P, HW), lambda b, i: (b, 1)),
                  pl.BlockSpec((P, HW), lambda b, i: (b, 2))],
        out_specs=pl.BlockSpec((tq, HW), lambda b, i: (b * per_b + qoff + i, 0)),
        out_shape=jax.ShapeDtypeStruct((N, HW), BF16),
        compiler_params=_cparams(2),
        name="gqa_attention",
    )(C, C, C)


def _ctx_attn_kernel(qa_ref, ka_ref, va_ref, qc_ref, kc_ref, vc_ref, ya_in, yc_in, ya_ref, yc_ref):
    del ya_in, yc_in
    ya_ref[...] = _mh_attend(qa_ref[...], [(ka_ref[...], va_ref[...], None)]).astype(BF16)
    yc_ref[...] = _mh_attend(qc_ref[...], [(kc_ref[...], vc_ref[...], None)]).astype(BF16)


def _ctx_attention(A, C, YA, YC, *, B, S, T):
    per_b = (S + T) // S
    blk = lambda c: pl.BlockSpec((S, HW), lambda b: (b * per_b, c))
    anyspec = pl.BlockSpec(memory_space=pl.ANY)
    return pl.pallas_call(
        _ctx_attn_kernel,
        grid=(B,),
        in_specs=[blk(0), blk(1), blk(2), blk(0), blk(1), blk(2), anyspec, anyspec],
        out_specs=[blk(0), blk(0)],
        out_shape=[jax.ShapeDtypeStruct(YA.shape, YA.dtype), jax.ShapeDtypeStruct(YC.shape, YC.dtype)],
        input_output_aliases={6: 0, 7: 1},
        compiler_params=_cparams(1),
        name="ctx_attention",
    )(A, A, A, C, C, C, YA, YC)


def _gla_kernel(bq_ref, lr_ref, wg_ref, bg_ref, tblk_ref, oblk_ref, tri4_ref, o_ref,
                qd_scr, ki_scr, ke_scr, dec_scr, *, S, P, gr):
    C = GLA_CHUNK
    ncs = S // C
    nc = P // C
    dk_h = GLA_DK // GLA_HEADS
    dv_h = GLA_DV // GLA_HEADS

    def thirds(a):
        return a[:, 0:GLA_DK] + a[:, GLA_DK:2 * GLA_DK] + a[:, 2 * GLA_DK:3 * GLA_DK]

    def decays(gi, carry):
        rows = pl.ds(pl.multiple_of(gi * gr, gr), gr)
        q = bq_ref[rows, 0:GLA_DK].astype(F32)
        k = bq_ref[rows, GLA_DK:2 * GLA_DK].astype(F32)
        lr = lr_ref[rows, :]
        for d in range(2):
            z = _dot(lr, wg_ref[d]) + bg_ref[d]
            g = (jnp.minimum(z, 0.0) - jnp.log(1.0 + jnp.exp(-jnp.abs(z)))) * (1.0 / GLA_TAU)
            g3 = jnp.concatenate(_split3(g), axis=1)
            bcum = thirds(_dot(tblk_ref[d], g3))
            btot = thirds(_dot(oblk_ref[...], g3))
            qd_scr[d, rows, :] = (q * (dk_h ** -0.5) * jnp.exp(bcum)).astype(BF16)
            ki_scr[d, rows, :] = (k * jnp.exp(-bcum)).astype(BF16)
            ke_scr[d, rows, :] = (k * jnp.exp(btot - bcum)).astype(BF16)
            dec_scr[d, rows, :] = jnp.exp(btot)
        return carry

    lax.fori_loop(0, P // gr, decays, 0)

    lane_k = lax.broadcasted_iota(I32, (1, GLA_DK), 1) // dk_h
    lane_v = lax.broadcasted_iota(I32, (1, GLA_DV), 1) // dv_h
    bd = (lax.broadcasted_iota(I32, (GLA_DV, GLA_DK), 0) // dv_h
          == lax.broadcasted_iota(I32, (GLA_DV, GLA_DK), 1) // dk_h)

    def chunk(d, ci, st):
        rows = pl.ds(pl.multiple_of(ci * C, C), C)
        q_dec = qd_scr[d, rows, :]
        k_inv = ki_scr[d, rows, :]
        v = bq_ref[rows, 2 * GLA_DK:2 * GLA_DK + GLA_DV]
        k_stack = jnp.concatenate(
            [jnp.where(lane_k == h, k_inv, jnp.zeros_like(k_inv)) for h in range(GLA_HEADS)], axis=0)
        a_cat = (_dot_nt(q_dec, k_stack) * tri4_ref[d]).astype(BF16)
        v_stack = jnp.concatenate(
            [jnp.where(lane_v == h, v, jnp.zeros_like(v)) for h in range(GLA_HEADS)], axis=0)
        lhs = jnp.concatenate([a_cat, q_dec], axis=1)
        rhs = jnp.concatenate([v_stack, jnp.transpose(st).astype(BF16)], axis=0)
        o_ref[d, rows, :] = _dot(lhs, rhs)
        v_t = jnp.transpose(v.astype(F32)).astype(BF16)
        ds_t = _dot(v_t, ke_scr[d, rows, :])
        return st * dec_scr[d, pl.ds(pl.multiple_of(ci * C, C), 1), :] + jnp.where(bd, ds_t, 0.0)

    def body(n, sts):
        cb = jnp.where(n < ncs, ncs - 1 - n, nc - 1 - (n - ncs))
        return chunk(0, n, sts[0]), chunk(1, cb, sts[1])

    zero = jnp.zeros((GLA_DV, GLA_DK), F32)
    lax.fori_loop(0, nc, body, (zero, zero), unroll=4)


def _gla(Bq, LR, wg_pad, bg, tri4, *, B, S, T):
    N = Bq.shape[0]
    P = S + T
    gr = 256 if P % 256 == 0 else 2 * GLA_CHUNK
    per = gr // GLA_CHUNK
    ii = np.arange(GLA_CHUNK)
    tril = (ii[:, None] >= ii[None, :]).astype(np.float32)
    eye = np.eye(per, dtype=np.float32)
    tblk = jnp.asarray(np.stack([np.kron(eye, tril), np.kron(eye, tril.T)]), BF16)
    oblk = jnp.asarray(np.kron(eye, np.ones((GLA_CHUNK, GLA_CHUNK), np.float32)), BF16)
    full = lambda shape: pl.BlockSpec(shape, lambda b: (0,) * len(shape))
    return pl.pallas_call(
        functools.partial(_gla_kernel, S=S, P=P, gr=gr),
        grid=(B,),
        in_specs=[pl.BlockSpec((P, 768), lambda b: (b, 0)),
                  pl.BlockSpec((P, LANES), lambda b: (b, 0)),
                  full((2, LANES, GLA_DK)), full((2, 1, GLA_DK)),
                  full((2, gr, gr)), full((gr, gr)),
                  full((2, GLA_CHUNK, GLA_HEADS * GLA_CHUNK))],
        out_specs=pl.BlockSpec((2, P, GLA_DV), lambda b: (0, b, 0)),
        out_shape=jax.ShapeDtypeStruct((2, N, GLA_DV), F32),
        scratch_shapes=[pltpu.VMEM((2, P, GLA_DK), BF16), pltpu.VMEM((2, P, GLA_DK), BF16),
                        pltpu.VMEM((2, P, GLA_DK), BF16), pltpu.VMEM((2, P, GLA_DK), F32)],
        compiler_params=_cparams(1),
        name="gla",
    )(Bq, LR, wg_pad, bg, tblk, oblk, tri4)


def _merge_kernel(ya_ref, of_ref, ob_ref, rb_ref, yc_ref, gt_ref, x_ref,
                  wa_ref, wb_ref, wc_ref, wo_ref, seg_ref, gg_ref, n2_ref, wr_ref,
                  g1_ref, sc2_ref, sh2_ref, g1c_ref, sc2c_ref, sh2c_ref,
                  x1_ref, xn_ref, aff_ref, *, S):
    tr = x_ref.shape[0]
    first = pl.program_id(1) == 0
    parts = [(slice(0, S), jnp.where(first, g1c_ref[0], g1_ref[0]),
              jnp.where(first, sc2c_ref[0], sc2_ref[0]), jnp.where(first, sh2c_ref[0], sh2_ref[0]))]
    if S < tr:
        parts.append((slice(S, tr), g1_ref[0], sc2_ref[0], sh2_ref[0]))

    def by_rows(fn):
        return jnp.concatenate([fn(*p) for p in parts], axis=0)

    o = of_ref[0] + ob_ref[0]
    yb = o * lax.rsqrt(_seg_meansq(o, seg_ref) + NORM_EPS) * gg_ref[...]
    yb = (yb * _silu(rb_ref[...].astype(F32))).astype(BF16)

    def gated(i, d):
        return jnp.tanh(gt_ref[:, i * 1024:(i + 1) * 1024].astype(F32)) * d + d

    m2 = (gated(0, _dot(ya_ref[...], wa_ref[...]))
          + gated(1, _dot(yb, wb_ref[...]))
          + gated(2, _dot(yc_ref[...], wc_ref[...])))
    mo = _dot(m2.astype(BF16), wo_ref[...])
    x1 = by_rows(lambda r, g1, sc2, sh2: x_ref[r, :] + (0.5 * g1) * mo[r])
    x1_ref[...] = x1
    y = x1 * lax.rsqrt(jnp.mean(x1 * x1, axis=-1, keepdims=True) + NORM_EPS)
    xn = by_rows(lambda r, g1, sc2, sh2: y[r] * (n2_ref[...] * (1.0 + sc2)) + sh2)
    xn_ref[...] = xn.astype(BF16)
    xh, xl = _split2(xn)
    wh, wl = _split2(wr_ref[...])
    logits = _dot(xh, wh) + (_dot(xl, wh) + _dot(xh, wl))
    e = jnp.exp(logits - logits.max(axis=-1, keepdims=True))
    aff_ref[...] = e / e.sum(axis=-1, keepdims=True)


def _merge(YA, O, Bq, YC, G, X, wa, wb, wc, wo, seg, gg, g1, n2, sc2, sh2, wr, *, B, S, T):
    N, D = X.shape
    P = S + T
    tm = _row_tile(S, P)
    NT = P // tm
    E = wr.shape[1]

    def row(b, j):
        return (b * NT + j, 0)

    grp = lambda b, j: (b, 0, 0)
    grp_c = lambda b, j: (B, 0, 0)
    const = lambda b, j: (0, 0)
    mod = lambda m: pl.BlockSpec((1, 1, D), m)
    return pl.pallas_call(
        functools.partial(_merge_kernel, S=S),
        grid=(B, NT),
        in_specs=[pl.BlockSpec((tm, HW), row),
                  pl.BlockSpec((1, tm, HW), lambda b, j: (0, b * NT + j, 0)),
                  pl.BlockSpec((1, tm, HW), lambda b, j: (1, b * NT + j, 0)),
                  pl.BlockSpec((tm, HW), lambda b, j: (b * NT + j, 2)),
                  pl.BlockSpec((tm, HW), row),
                  pl.BlockSpec((tm, 3072), row),
                  pl.BlockSpec((tm, D), row),
                  pl.BlockSpec((HW, D), const), pl.BlockSpec((HW, D), const), pl.BlockSpec((HW, D), const),
                  pl.BlockSpec((D, D), const),
                  pl.BlockSpec((HW, HW), const),
                  pl.BlockSpec((1, HW), const),
                  pl.BlockSpec((1, D), const),
                  pl.BlockSpec((D, E), const),
                  mod(grp), mod(grp), mod(grp), mod(grp_c), mod(grp_c), mod(grp_c)],
        out_specs=[pl.BlockSpec((tm, D), row), pl.BlockSpec((tm, D), row), pl.BlockSpec((tm, E), row)],
        out_shape=[jax.ShapeDtypeStruct((N, D), F32), jax.ShapeDtypeStruct((N, D), BF16),
                   jax.ShapeDtypeStruct((N, E), F32)],
        compiler_params=_cparams(2),
        name="merge",
    )(YA, O, O, Bq, YC, G, X, wa, wb, wc, wo, seg, gg, n2, wr, g1, sc2, sh2, g1, sc2, sh2)


def _cumsum_lanes(m01, upper, n):
    outs = []
    carry = jnp.zeros((m01.shape[0], 1), F32)
    for c in range(n // LANES):
        blk = _dot(m01[:, c * LANES:(c + 1) * LANES], upper) + carry
        outs.append(blk)
        carry = blk[:, LANES - 1:LANES]
    return jnp.concatenate(outs, axis=1)


def _topk_mask(aff, cap, upper):
    n = aff.shape[1]
    key = lax.bitcast_convert_type(aff, I32)

    def step(i, theta):
        cand = theta | jnp.left_shift(jnp.int32(1), 30 - i)
        cnt = jnp.sum(jnp.where(key >= cand, 1.0, 0.0), axis=1, keepdims=True)
        return jnp.where(cnt >= cap, cand, theta)

    theta = lax.fori_loop(0, 31, step, jnp.zeros((aff.shape[0], 1), I32))
    gt = key > theta
    eq = key == theta
    need = cap - jnp.sum(jnp.where(gt, 1.0, 0.0), axis=1, keepdims=True)
    eq01 = jnp.where(eq, 1.0, 0.0)
    rank = _cumsum_lanes(eq01.astype(BF16), upper, n)
    take = eq & (rank <= need)
    return jnp.where(gt | take, 1.0, 0.0)


def _select_kernel(aff_ref, upper_ref, ind_ref, slot_ref, w_ref, off_ref, *, S, T, cap_c, cap_l):
    aff = aff_ref[...]
    upper = upper_ref[...]
    sel = jnp.concatenate([_topk_mask(aff[:, 0:S], cap_c, upper),
                           _topk_mask(aff[:, S:S + T], cap_l, upper)], axis=1)
    sel16 = sel.astype(BF16)
    pos = _cumsum_lanes(sel16, upper, S + T) - sel
    chosen = sel > 0.5
    slot_ref[0] = jnp.where(chosen, pos, -1.0).astype(I32)
    w_ref[0] = jnp.where(chosen, aff, 0.0)
    off_ref[0] = _dot(sel16, ind_ref[...]).astype(I32)


def _select(AFF, upper, ind, *, B, S, T, cap_c, cap_l):
    E = AFF.shape[0]
    P = S + T
    return pl.pallas_call(
        functools.partial(_select_kernel, S=S, T=T, cap_c=cap_c, cap_l=cap_l),
        grid=(B,),
        in_specs=[pl.BlockSpec((E, P), lambda b: (0, b)),
                  pl.BlockSpec((LANES, LANES), lambda b: (0, 0)),
                  pl.BlockSpec((P, LANES), lambda b: (0, 0))],
        out_specs=[pl.BlockSpec((1, E, P), lambda b: (b, 0, 0)),
                   pl.BlockSpec((1, E, P), lambda b: (b, 0, 0)),
                   pl.BlockSpec((1, E, LANES), lambda b: (b, 0, 0))],
        out_shape=[jax.ShapeDtypeStruct((B, E, P), I32), jax.ShapeDtypeStruct((B, E, P), F32),
                   jax.ShapeDtypeStruct((B, E, LANES), I32)],
        compiler_params=_cparams(1),
        name="ec_select",
    )(AFF, upper, ind)


EC_WIN = 64
SLOT_ALIGN = 16


def _win_base(lo, capt):
    return pl.multiple_of(jnp.minimum(lo // SLOT_ALIGN * SLOT_ALIGN, capt - EC_WIN), SLOT_ALIGN)


def _n_windows(tm):
    return -(-(tm + SLOT_ALIGN - 1) // EC_WIN)


def _any_overflow(offs, bases, E):
    over = None
    for e in range(E):
        c = offs(e)[1] > bases[e] + EC_WIN
        over = c if over is None else (over | c)
    return over


def _gather_kernel(off_ref, x_ref, wc_ref, slot_ref, xg_ref, wg_ref, *, E, NT, tm, capt):
    b = pl.program_id(0)
    j = pl.program_id(1)
    W = EC_WIN

    @pl.when(j == 0)
    def _():
        xg_ref[...] = jnp.zeros_like(xg_ref)
        wg_ref[...] = jnp.zeros_like(wg_ref)

    def offs(e):
        i = (b * E + e) * (NT + 1) + j
        return off_ref[i], off_ref[i + 1]

    rows = lax.broadcasted_iota(I32, (W, tm), 0)
    x = x_ref[...]
    wc = wc_ref[...]
    bases = [_win_base(offs(e)[0], capt) for e in range(E)]
    lhs = jnp.concatenate(
        [jnp.where(rows == slot_ref[0, e:e + 1, :] - bases[e], 1.0, 0.0).astype(BF16) for e in range(E)], axis=0)
    rx = _dot(lhs, x).astype(BF16)
    rw = _dot(lhs, wc)
    for e in range(E):
        xg_ref[e, pl.ds(bases[e], W), :] += rx[e * W:(e + 1) * W]
        wg_ref[e, pl.ds(bases[e], W), :] += rw[e * W:(e + 1) * W]

    def extra(e, carry):
        lo, hi = offs(e)
        base = _win_base(lo, capt)
        for w in range(1, _n_windows(tm)):
            first = base + w * W

            @pl.when(hi > first)
            def _():
                start = pl.multiple_of(jnp.minimum(first, capt - W), SLOT_ALIGN)
                s = slot_ref[0, pl.ds(e, 1), :]
                onehot = jnp.where(rows == jnp.where(s >= first, s - start, -1), 1.0, 0.0).astype(BF16)
                xg_ref[e, pl.ds(start, W), :] += _dot(onehot, x).astype(BF16)
                wg_ref[e, pl.ds(start, W), :] += _dot(onehot, wc)
        return carry

    @pl.when(_any_overflow(offs, bases, E))
    def _():
        lax.fori_loop(0, E, extra, 0)


def _gather(off, XN, WC, slot, *, B, E, S, T, tm, capt):
    N, D = XN.shape
    P = S + T
    NT = P // tm
    row = lambda b, j, off: (b * NT + j, 0)
    return pl.pallas_call(
        functools.partial(_gather_kernel, E=E, NT=NT, tm=tm, capt=capt),
        grid_spec=pltpu.PrefetchScalarGridSpec(
            num_scalar_prefetch=1,
            grid=(B, NT),
            in_specs=[pl.BlockSpec((tm, D), row),
                      pl.BlockSpec((tm, LANES), row),
                      pl.BlockSpec((1, E, tm), lambda b, j, off: (b, 0, j))],
            out_specs=[pl.BlockSpec((E, capt, D), lambda b, j, off: (0, b, 0)),
                       pl.BlockSpec((E, capt, LANES), lambda b, j, off: (0, b, 0))]),
        out_shape=[jax.ShapeDtypeStruct((E, B * capt, D), BF16),
                   jax.ShapeDtypeStruct((E, B * capt, LANES), F32)],
        compiler_params=_cparams(2),
        name="ec_gather",
    )(off, XN, WC, slot)


def _ffn_kernel(x_ref, val_ref, wg_ref, wu_ref, wd_ref, o_ref, acc_ref, *, E):
    f = pl.program_id(2)

    @pl.when(f == 0)
    def _():
        acc_ref[...] = jnp.zeros_like(acc_ref)

    x = x_ref[0]
    g = _dot(x, wg_ref[0, 0].astype(BF16))
    u = _dot(x, wu_ref[0, 0].astype(BF16))
    h = (_silu(g) * u).astype(BF16)
    acc_ref[...] += _dot(h, wd_ref[0, 0].astype(BF16))

    @pl.when(f == pl.num_programs(2) - 1)
    def _():
        lane = lax.broadcasted_iota(I32, (1, LANES), 1)
        mine = (lane % E == pl.program_id(0)) & (lane < 3 * E)
        val = jnp.sum(jnp.where(mine, val_ref[0], 0.0), axis=1, keepdims=True)
        o_ref[0] = (acc_ref[...] * val).astype(BF16)


def _ffn(XG, WG, w_gate, w_up, w_down, l, *, rows, fc):
    E, R, D = XG.shape
    F = w_gate.shape[-1]
    return pl.pallas_call(
        functools.partial(_ffn_kernel, E=E),
        grid=(E, R // rows, F // fc),
        in_specs=[pl.BlockSpec((1, rows, D), lambda e, r, f: (e, r, 0)),
                  pl.BlockSpec((1, rows, LANES), lambda e, r, f: (e, r, 0)),
                  pl.BlockSpec((1, 1, D, fc), lambda e, r, f: (l, e, 0, f)),
                  pl.BlockSpec((1, 1, D, fc), lambda e, r, f: (l, e, 0, f)),
                  pl.BlockSpec((1, 1, fc, D), lambda e, r, f: (l, e, f, 0))],
        out_specs=pl.BlockSpec((1, rows, D), lambda e, r, f: (e, r, 0)),
        out_shape=jax.ShapeDtypeStruct((E, R, D), BF16),
        scratch_shapes=[pltpu.VMEM((rows, D), F32)],
        compiler_params=_cparams(3),
        name="ec_ffn",
    )(XG, WG, w_gate, w_up, w_down)


def _combine_kernel(off_ref, y_ref, shl_ref, spread_ref, slot_ref, x_ref, g2_ref, fg_ref, o_ref, acc_ref,
                    *, E, NT, j0, tm, capt, final):
    b = pl.program_id(0)
    j = pl.program_id(1) + j0
    W = EC_WIN

    def offs(e):
        i = (b * E + e) * (NT + 1) + j
        return off_ref[i], off_ref[i + 1]

    bases = [_win_base(offs(e)[0], capt) for e in range(E)]
    col = lax.broadcasted_iota(I32, (1, E * W), 1)
    target = (col % W).astype(F32)
    for e in range(E):
        target = target + jnp.where(col // W == e, bases[e].astype(F32), 0.0)
    spread = _dot(shl_ref[...], spread_ref[...])
    lhs = jnp.where(spread == target, 1.0, 0.0).astype(BF16)
    rhs = jnp.concatenate([y_ref[e, pl.ds(bases[e], W), :] for e in range(E)], axis=0)
    acc_ref[...] = _dot(lhs, rhs)

    rows = lax.broadcasted_iota(I32, (W, tm), 0)

    def extra(e, carry):
        lo, hi = offs(e)
        base = _win_base(lo, capt)
        for w in range(1, _n_windows(tm)):
            first = base + w * W

            @pl.when(hi > first)
            def _():
                start = pl.multiple_of(jnp.minimum(first, capt - W), SLOT_ALIGN)
                s = slot_ref[0, pl.ds(e, 1), :]
                onehot_t = jnp.where(rows == jnp.where(s >= first, s - start, -1), 1.0, 0.0)
                onehot = jnp.transpose(onehot_t).astype(BF16)
                acc_ref[...] += _dot(onehot, y_ref[e, pl.ds(start, W), :])
        return carry

    @pl.when(_any_overflow(offs, bases, E))
    def _():
        lax.fori_loop(0, E, extra, 0)
    x2 = x_ref[...] + g2_ref[0] * acc_ref[...]
    if final:
        x2 = x2 * lax.rsqrt(jnp.mean(x2 * x2, axis=-1, keepdims=True) + NORM_EPS) * fg_ref[...]
    if final:
        o_ref[0] = x2
    else:
        o_ref[...] = x2


def _combine(off, Y, slotT, slot, X1, g2, fg, *, B, E, S, T, tm, capt, final):
    N, D = X1.shape
    P = S + T
    NT = P // tm
    ns = S // tm
    j0 = ns if final else 0
    shl = jnp.concatenate([slotT >> 4, slotT & 15], axis=1).astype(BF16)
    own = np.kron(np.eye(E, dtype=np.float32), np.ones((1, EC_WIN), np.float32))
    spread = jnp.asarray(np.concatenate([16.0 * own, own], axis=0), BF16)

    def row(b, j, off):
        return (b * NT + j + j0, 0)

    if final:
        out_spec = pl.BlockSpec((1, tm, D), lambda b, j, off: (b, j, 0))
        out_shape = jax.ShapeDtypeStruct((B, T, D), F32)
    else:
        out_spec = pl.BlockSpec((tm, D), row)
        out_shape = jax.ShapeDtypeStruct((N, D), F32)
    return pl.pallas_call(
        functools.partial(_combine_kernel, E=E, NT=NT, j0=j0, tm=tm, capt=capt, final=final),
        grid_spec=pltpu.PrefetchScalarGridSpec(
            num_scalar_prefetch=1,
            grid=(B, NT - j0),
            in_specs=[_resident((E, capt, D), lambda b, j, off: (0, b, 0)),
                      pl.BlockSpec((tm, 2 * E), row),
                      pl.BlockSpec((2 * E, E * EC_WIN), lambda b, j, off: (0, 0)),
                      pl.BlockSpec((1, E, tm), lambda b, j, off: (b, 0, j + j0)),
                      pl.BlockSpec((tm, D), row),
                      pl.BlockSpec((1, 1, D), lambda b, j, off: (jnp.where(j + j0 < ns, B, b), 0, 0)),
                      pl.BlockSpec((1, D), lambda b, j, off: (0, 0))],
            out_specs=out_spec,
            scratch_shapes=[pltpu.VMEM((tm, D), F32)]),
        out_shape=out_shape,
        compiler_params=_cparams(2),
        name="ec_combine",
    )(off, Y, shl, spread, slot, X1, g2, fg)


def _permute_w_in(w):
    D = w.shape[0]
    o = np.cumsum([0, 256, 256, 256, 128, 128, 256, 256, 16, 16, 256, 128, 128, 1024, 1024, 1024])
    qa, ka, va, qb, kb, vb, rb, lf, lb, qc, kc, vc, mga, mgb, mgc = [w[:, o[i]:o[i + 1]] for i in range(15)]
    grp = GQA_HEADS // GQA_KV_HEADS

    def rep(t):
        return jnp.concatenate([t[:, (h // grp) * HEAD_DIM:(h // grp + 1) * HEAD_DIM] for h in range(GQA_HEADS)], axis=1)

    pad = jnp.zeros((D, COL_END - COL_LR - 2 * GLA_LOWRANK), w.dtype)
    gates = 0.5 * jnp.concatenate([mga, mgb, mgc], axis=1)
    return jnp.concatenate([qa, ka, va, qb, kb, vb, rb, qc, rep(kc), rep(vc), gates, lf, lb, pad],
                           axis=1).astype(BF16)


def _prep_w_kernel(w_ref, o_ref):
    o_ref[...] = _permute_w_in(w_ref[0])


def _prep_w(w_in, l):
    L, D, C = w_in.shape
    tr = 256
    return pl.pallas_call(
        _prep_w_kernel,
        grid=(D // tr,),
        in_specs=[pl.BlockSpec((1, tr, C), lambda i: (l, i, 0))],
        out_specs=pl.BlockSpec((tr, COL_END), lambda i: (i, 0)),
        out_shape=jax.ShapeDtypeStruct((D, COL_END), BF16),
        compiler_params=_cparams(1),
        name="prep_w_in",
    )(w_in)


def _rope_tables(T, S):
    t = np.arange(T)
    row = (t // GRID_W).astype(np.float32)
    col = (t % GRID_W).astype(np.float32)
    lane = np.arange(HEAD_DIM)
    is_col = lane >= HEAD_DIM // 2
    fi = lane % 16
    inv_freq = ROPE_BASE ** (-jnp.arange(16, dtype=F32) / 16)
    pos = jnp.where(jnp.asarray(is_col)[None, :], jnp.asarray(col)[:, None], jnp.asarray(row)[:, None])
    ang = pos * inv_freq[jnp.asarray(fi)][None, :]
    cos = jnp.cos(ang)
    sin = jnp.sin(ang)
    sin = jnp.where(jnp.asarray((lane % 32) < 16)[None, :], -sin, sin)
    cos = jnp.concatenate([jnp.ones((S, HW), F32), jnp.tile(cos, (1, HW // HEAD_DIM))], axis=0)
    sin = jnp.concatenate([jnp.zeros((S, HW), F32), jnp.tile(sin, (1, HW // HEAD_DIM))], axis=0)
    return cos, sin


def kernel(x, c, ctx, c_ctx, w_ada, b_ada, norm1_g, norm2_g, w_in, na_rpb, gla_wg_f, gla_bg_f, gla_wg_b,
           gla_bg_b, gla_norm_g, gqa_qn_g, gqa_kn_g, w_branch_a, w_branch_b, w_branch_c, w_out, w_router,
           w_e_gate, w_e_up, w_e_down, final_norm_g):
    B, T, D = x.shape
    S = ctx.shape[1]
    L = w_ada.shape[0]
    E = w_router.shape[-1]
    P = S + T
    tm = min(256, S)
    NT = P // tm
    assert S % tm == 0 and T % tm == 0 and S % GRID_W == 0 and T % GRID_W == 0 and P % S == 0
    cap_l = EC_CAPACITY_FACTOR * T // E
    cap_c = EC_CAPACITY_FACTOR * S // E
    capt = cap_l + cap_c
    assert capt % SLOT_ALIGN == 0 and capt >= EC_WIN and 3 * E <= LANES and E % (LANES // EC_WIN) == 0
    R = T // GRID_W
    kh = min(NA_KH, R)

    seg = jnp.asarray(np.kron(np.eye(HW // HEAD_DIM), np.ones((HEAD_DIM, HEAD_DIM))), BF16)
    cos_t, sin_t = _rope_tables(T, S)
    ii = np.arange(GLA_CHUNK)
    tril = (ii[:, None] >= ii[None, :]).astype(np.float32)
    tri4 = jnp.asarray(np.stack([np.tile(tril, (1, GLA_HEADS)), np.tile(tril.T, (1, GLA_HEADS))]))
    il = np.arange(LANES)
    upper = jnp.asarray(il[:, None] <= il[None, :], BF16)
    ind = jnp.asarray(np.arange(P)[:, None] < (np.arange(LANES) * tm)[None, :], BF16)

    RP = -(-(B + 1) // 8) * 8
    cv = jnp.concatenate([c, c_ctx[None], jnp.zeros((RP - B - 1, D), F32)], axis=0)
    mod = _adaln(cv, w_ada, b_ada)[:, :B + 1].reshape(L, B + 1, 6, 1, D)

    X = jnp.concatenate([ctx, x], axis=1).reshape(B * P, D)
    out = None
    for l in range(L):
        last = l == L - 1
        sh1, sc1, g1, sh2, sc2, g2 = [mod[l, :, i] for i in range(6)]
        w_p = _prep_w(w_in, l)
        qn = jnp.tile(gqa_qn_g[l], HW // HEAD_DIM)[None]
        kn = jnp.tile(gqa_kn_g[l], HW // HEAD_DIM)[None]
        A, Bq, C, G, LR = _proj_in(X, norm1_g[l][None], sc1, sh1, w_p, seg, qn, kn, cos_t, sin_t,
                                   B=B, S=S, T=T)

        YA = _na_attention(A, _na_bias_table(na_rpb[l], kh), B=B, S=S, T=T)
        YC = _gqa_attention(C, B=B, S=S, T=T, tq=min(256, S))
        YA, YC = _ctx_attention(A, C, YA, YC, B=B, S=S, T=T)

        zpad = jnp.zeros((LANES - 2 * GLA_LOWRANK, GLA_DK), F32)
        zlr = jnp.zeros((GLA_LOWRANK, GLA_DK), F32)
        wg_pad = jnp.stack([jnp.concatenate([gla_wg_f[l], zlr, zpad], axis=0),
                            jnp.concatenate([zlr, gla_wg_b[l], zpad], axis=0)]).astype(BF16)
        bg = jnp.stack([gla_bg_f[l], gla_bg_b[l]])[:, None, :]
        O = _gla(Bq, LR, wg_pad, bg, tri4, B=B, S=S, T=T)

        gg = jnp.tile(gla_norm_g[l], GLA_HEADS)[None]
        X1, XN, AFFT = _merge(YA, O, Bq, YC, G, X,
                              w_branch_a[l].astype(BF16), w_branch_b[l].astype(BF16),
                              w_branch_c[l].astype(BF16), w_out[l].astype(BF16),
                              seg, gg, g1, norm2_g[l][None], sc2, sh2, w_router[l],
                              B=B, S=S, T=T)

        slot, wsel, off = _select(jnp.transpose(AFFT), upper, ind, B=B, S=S, T=T, cap_c=cap_c, cap_l=cap_l)
        off_flat = off[:, :, :NT + 1].reshape(-1)
        slotT = jnp.swapaxes(slot, 1, 2).reshape(B * P, E)
        wT = jnp.swapaxes(wsel, 1, 2).reshape(B * P, E)
        WC = jnp.concatenate(list(_split3(wT)) + [jnp.zeros((B * P, LANES - 3 * E), BF16)], axis=1)
        XG, WG = _gather(off_flat, XN, WC, slot, B=B, E=E, S=S, T=T, tm=tm, capt=capt)
        rows = capt * (2 if B % 2 == 0 else 1)
        Y = _ffn(XG, WG, w_e_gate, w_e_up, w_e_down, l, rows=rows, fc=min(512, w_e_gate.shape[-1]))
        res = _combine(off_flat, Y, slotT, slot, X1, g2, final_norm_g[None],
                       B=B, E=E, S=S, T=T, tm=tm, capt=capt, final=last)
        if last:
            out = res
        else:
            X = res
    return out
```

```python
import functools

import jax
import jax.numpy as jnp
import numpy as np
from jax import lax
from jax.experimental import pallas as pl
from jax.experimental.pallas import tpu as pltpu

F32 = jnp.float32
BF16 = jnp.bfloat16
I32 = jnp.int32

GRID_W = 64
HEAD_DIM = 64
NORM_EPS = 1e-6
NA_HEADS = 4
NA_KH = 8
NA_KW = 16
GLA_HEADS = 4
GLA_DK = 128
GLA_DV = 256
GLA_LOWRANK = 16
GLA_TAU = 16.0
GLA_CHUNK = 64
GQA_HEADS = 4
GQA_KV_HEADS = 2
ROPE_BASE = 10000.0
N_EXPERTS = 16
EC_CAPACITY_FACTOR = 2
NEG_BIG = -1e30

LANES = 128
HW = 256
VMEM_LIMIT = 56 * 1024 * 1024

COL_A = 0
COL_B = 768
COL_C = 1536
COL_G = 2304
COL_LR = 5376
COL_END = 5504


def _cparams(n_grid):
    return pltpu.CompilerParams(dimension_semantics=("arbitrary",) * n_grid,
                                vmem_limit_bytes=VMEM_LIMIT)


def _resident(block_shape, index_map):
    return pl.BlockSpec(block_shape, index_map, pipeline_mode=pl.Buffered(1))


def _dot(a, b):
    return jnp.dot(a, b, preferred_element_type=F32)


def _dot_nt(a, b):
    return lax.dot_general(a, b, (((1,), (1,)), ((), ())), preferred_element_type=F32)


def _split2(a):
    hi = a.astype(BF16)
    lo = (a - hi.astype(F32)).astype(BF16)
    return hi, lo


def _split3(a):
    h1 = a.astype(BF16)
    r1 = a - h1.astype(F32)
    h2 = r1.astype(BF16)
    h3 = (r1 - h2.astype(F32)).astype(BF16)
    return h1, h2, h3


def _seg_meansq(x, seg_ref):
    hi, lo = _split2(x * x)
    seg = seg_ref[...]
    return (_dot(hi, seg) + _dot(lo, seg)) * (1.0 / HEAD_DIM)


def _sigmoid(x):
    return 0.5 * jnp.tanh(0.5 * x) + 0.5


def _silu(x):
    return x * _sigmoid(x)


def _adaln_kernel(cv_ref, w_ref, b_ref, o_ref):
    s = _silu(cv_ref[...])
    w = w_ref[0]
    s1, s2, s3 = _split3(s)
    w1, w2, w3 = _split3(w)
    acc = _dot(s1, w1) + (_dot(s1, w2) + _dot(s2, w1)) + (_dot(s2, w2) + _dot(s1, w3) + _dot(s3, w1))
    o_ref[0] = acc + b_ref[0]


def _adaln(cv, w_ada, b_ada):
    L, D, D6 = w_ada.shape
    R = cv.shape[0]
    tn = 1024
    return pl.pallas_call(
        _adaln_kernel,
        grid=(L, D6 // tn),
        in_specs=[pl.BlockSpec((R, D), lambda l, n: (0, 0)),
                  pl.BlockSpec((1, D, tn), lambda l, n: (l, 0, n)),
                  pl.BlockSpec((1, 1, tn), lambda l, n: (l, 0, n))],
        out_specs=pl.BlockSpec((1, R, tn), lambda l, n: (l, 0, n)),
        out_shape=jax.ShapeDtypeStruct((L, R, D6), F32),
        compiler_params=_cparams(2),
        name="adaln",
    )(cv, w_ada, b_ada.reshape(L, 1, D6))


def _swap16(y):
    lane = lax.broadcasted_iota(I32, y.shape, 1)
    first = (lane % 32) < 16
    return jnp.where(first, pltpu.roll(y, LANES - 16, 1), pltpu.roll(y, 16, 1))


def _norm_rope(x, seg_ref, g_ref, cos_ref, sin_ref, scale):
    y = x * lax.rsqrt(_seg_meansq(x, seg_ref) + NORM_EPS) * g_ref[...]
    halves = []
    for i in range(HW // LANES):
        sl = slice(i * LANES, (i + 1) * LANES)
        yh = y[:, sl]
        halves.append(yh * cos_ref[:, sl] + _swap16(yh) * sin_ref[:, sl])
    out = jnp.concatenate(halves, axis=1)
    return out * scale if scale != 1.0 else out


def _proj_in_kernel(x_ref, g_ref, sc_ref, sh_ref, scc_ref, shc_ref, w_ref, seg_ref, qn_ref, kn_ref,
                    cos_ref, sin_ref, a_ref, b_ref, c_ref, gt_ref, lr_ref, *, S):
    x = x_ref[...]
    tr = x.shape[0]
    is_ctx = (pl.program_id(1) * tr + lax.broadcasted_iota(I32, (tr, 1), 0)) < S
    sc = jnp.where(is_ctx, scc_ref[0], sc_ref[0])
    sh = jnp.where(is_ctx, shc_ref[0], sh_ref[0])
    y = x * lax.rsqrt(jnp.mean(x * x, axis=-1, keepdims=True) + NORM_EPS)
    u = (y * g_ref[...] * (1.0 + sc) + sh).astype(BF16)

    def proj(c0, c1):
        return _dot(u, w_ref[:, c0:c1])

    qscale = HEAD_DIM ** -0.5
    a_ref[:, 0:HW] = (proj(COL_A, COL_A + HW) * qscale).astype(BF16)
    a_ref[:, HW:3 * HW] = proj(COL_A + HW, COL_B).astype(BF16)
    b_ref[...] = proj(COL_B, COL_C).astype(BF16)
    qc = proj(COL_C, COL_C + HW)
    c_ref[:, 0:HW] = _norm_rope(qc, seg_ref, qn_ref, cos_ref, sin_ref, qscale).astype(BF16)
    kc = proj(COL_C + HW, COL_C + 2 * HW)
    c_ref[:, HW:2 * HW] = _norm_rope(kc, seg_ref, kn_ref, cos_ref, sin_ref, 1.0).astype(BF16)
    c_ref[:, 2 * HW:3 * HW] = proj(COL_C + 2 * HW, COL_G).astype(BF16)
    for i in range(3):
        gt_ref[:, i * 1024:(i + 1) * 1024] = proj(COL_G + i * 1024, COL_G + (i + 1) * 1024).astype(BF16)
    lr_ref[...] = proj(COL_LR, COL_END).astype(BF16)


def _row_tile(S, P):
    for k in (8, 4, 2, 1):
        if P % k == 0 and (P // k) % 16 == 0 and P // k >= S:
            return P // k
    raise ValueError("no row tile for these shapes")


def _proj_in(X, g, sc, sh, w_p, seg, qn, kn, cos_t, sin_t, *, B, S, T):
    N, D = X.shape
    P = S + T
    tm = _row_tile(S, P)
    NT = P // tm

    def row(b, j):
        return (b * NT + j, 0)

    grp = lambda b, j: (b, 0, 0)
    grp_c = lambda b, j: (B, 0, 0)
    tab = lambda b, j: (j, 0)
    const = lambda b, j: (0, 0)
    outs = [jax.ShapeDtypeStruct((N, 768), BF16), jax.ShapeDtypeStruct((N, 768), BF16),
            jax.ShapeDtypeStruct((N, 768), BF16), jax.ShapeDtypeStruct((N, 3072), BF16),
            jax.ShapeDtypeStruct((N, LANES), BF16)]
    return pl.pallas_call(
        functools.partial(_proj_in_kernel, S=S),
        grid=(B, NT),
        in_specs=[pl.BlockSpec((tm, D), row),
                  pl.BlockSpec((1, D), const),
                  pl.BlockSpec((1, 1, D), grp),
                  pl.BlockSpec((1, 1, D), grp),
                  pl.BlockSpec((1, 1, D), grp_c),
                  pl.BlockSpec((1, 1, D), grp_c),
                  _resident((D, COL_END), const),
                  pl.BlockSpec((HW, HW), const),
                  pl.BlockSpec((1, HW), const),
                  pl.BlockSpec((1, HW), const),
                  pl.BlockSpec((tm, HW), tab),
                  pl.BlockSpec((tm, HW), tab)],
        out_specs=[pl.BlockSpec((tm, 768), row), pl.BlockSpec((tm, 768), row),
                   pl.BlockSpec((tm, 768), row), pl.BlockSpec((tm, 3072), row),
                   pl.BlockSpec((tm, LANES), row)],
        out_shape=outs,
        compiler_params=_cparams(2),
        name="proj_in",
    )(X, g, sc, sh, sc, sh, w_p, seg, qn, kn, cos_t, sin_t)


def _mh_attend(q, segments):
    nq = q.shape[0]
    nh = HW // HEAD_DIM
    lane = lax.broadcasted_iota(I32, (1, HW), 1) // HEAD_DIM
    qs = jnp.concatenate([jnp.where(lane == h, q, jnp.zeros_like(q)) for h in range(nh)], axis=0)
    mx = den = o = None
    for k, v, bias in segments:
        s = _dot_nt(qs, k)
        if bias is not None:
            s = s + bias
        smax = s.max(axis=-1, keepdims=True)
        if mx is None:
            mx = smax
            p = jnp.exp(s - mx)
            den = p.sum(axis=-1, keepdims=True)
            o = _dot(p.astype(BF16), v)
        else:
            mx_new = jnp.maximum(mx, smax)
            alpha = jnp.exp(mx - mx_new)
            p = jnp.exp(s - mx_new)
            den = den * alpha + p.sum(axis=-1, keepdims=True)
            o = o * alpha + _dot(p.astype(BF16), v)
            mx = mx_new
    o = o / den
    acc = jnp.zeros(q.shape, F32)
    for h in range(nh):
        acc = acc + jnp.where(lane == h, o[h * nq:(h + 1) * nq], 0.0)
    return acc


def _na_kernel(q_ref, k_ref, v_ref, bias_ref, o_ref, *, S, R, kh, rb):
    for i in range(rb):
        r = pl.program_id(1) * rb + i
        r0 = jnp.clip(r - kh // 2, 0, R - kh)
        start = pl.multiple_of(S + r0 * GRID_W, GRID_W)
        kwin = k_ref[pl.ds(start, kh * GRID_W), :]
        vwin = v_ref[pl.ds(start, kh * GRID_W), :]
        segs = [(kwin, vwin, bias_ref[r - r0]),
                (k_ref[0:S, :], v_ref[0:S, :], None)]
        rows = slice(i * GRID_W, (i + 1) * GRID_W)
        o_ref[rows, :] = _mh_attend(q_ref[rows, :], segs).astype(BF16)


def _na_bias_table(rpb, kh):
    H = rpb.shape[0]
    W = GRID_W
    col = np.arange(W)
    col_start = np.clip(col - NA_KW // 2, 0, W - NA_KW)
    inwin = (col[None, :] >= col_start[:, None]) & (col[None, :] < col_start[:, None] + NA_KW)
    rp = jnp.pad(rpb.astype(F32), ((0, 0), (0, 0), (W, W)))
    toe = jnp.stack([rp[:, :, W + NA_KW - 1 - q:2 * W + NA_KW - 1 - q] for q in range(W)], axis=2)
    toe = jnp.where(jnp.asarray(inwin)[None, None], toe, NEG_BIG)
    t = jnp.stack([toe[:, NA_KH - 1 - c:NA_KH - 1 - c + kh] for c in range(kh)], axis=0)
    t = jnp.transpose(t, (0, 1, 3, 2, 4))
    return t.reshape(kh, H * W, kh * W)


def _na_attention(A, bias_t, *, B, S, T):
    N = A.shape[0]
    P = S + T
    R = T // GRID_W
    kh = min(NA_KH, R)
    rb = 4 if (R % 4 == 0 and S % (4 * GRID_W) == 0) else 1
    tq = rb * GRID_W
    qoff = S // tq
    per_b = P // tq

    return pl.pallas_call(
        functools.partial(_na_kernel, S=S, R=R, kh=kh, rb=rb),
        grid=(B, R // rb),
        in_specs=[pl.BlockSpec((tq, HW), lambda b, r: (b * per_b + qoff + r, 0)),
                  pl.BlockSpec((P, HW), lambda b, r: (b, 1)),
                  pl.BlockSpec((P, HW), lambda b, r: (b, 2)),
                  pl.BlockSpec((kh, NA_HEADS * GRID_W, kh * GRID_W), lambda b, r: (0, 0, 0))],
        out_specs=pl.BlockSpec((tq, HW), lambda b, r: (b * per_b + qoff + r, 0)),
        out_shape=jax.ShapeDtypeStruct((N, HW), BF16),
        compiler_params=_cparams(2),
        name="na_attention",
    )(A, A, A, bias_t)


def _key_chunks(n, target=512, align=256):
    nchunks = max(1, n // target)
    base = (n // nchunks) // align * align
    if base == 0:
        return [(0, n)]
    bounds = [i * base for i in range(nchunks)] + [n]
    return [(bounds[i], bounds[i + 1]) for i in range(nchunks)]


def _gqa_kernel(q_ref, k_ref, v_ref, o_ref):
    segs = [(k_ref[c0:c1, :], v_ref[c0:c1, :], None) for c0, c1 in _key_chunks(k_ref.shape[0])]
    o_ref[...] = _mh_attend(q_ref[...], segs).astype(BF16)


def _gqa_attention(C, *, B, S, T, tq):
    N = C.shape[0]
    P = S + T
    nq = T // tq
    qoff = S // tq
    per_b = P // tq
    return pl.pallas_call(
        _gqa_kernel,
        grid=(B, nq),
        in_specs=[pl.BlockSpec((tq, HW), lambda b, i: (b * per_b + qoff + i, 0)),
                  pl.BlockSpec((P, HW), lambda b, i: (b, 1)),
                  pl.BlockSpec((P, HW), lambda b, i: (b, 2))],
        out_specs=pl.BlockSpec((tq, HW), lambda b, i: (b * per_b + qoff + i, 0)),
        out_shape=jax.ShapeDtypeStruct((N, HW), BF16),
        compiler_params=_cparams(2),
        name="gqa_attention",
    )(C, C, C)


def _ctx_attn_kernel(qa_ref, ka_ref, va_ref, qc_ref, kc_ref, vc_ref, ya_in, yc_in, ya_ref, yc_ref):
    del ya_in, yc_in
    ya_ref[...] = _mh_attend(qa_ref[...], [(ka_ref[...], va_ref[...], None)]).astype(BF16)
    yc_ref[...] = _mh_attend(qc_ref[...], [(kc_ref[...], vc_ref[...], None)]).astype(BF16)


def _ctx_attention(A, C, YA, YC, *, B, S, T):
    per_b = (S + T) // S
    blk = lambda c: pl.BlockSpec((S, HW), lambda b: (b * per_b, c))
    anyspec = pl.BlockSpec(memory_space=pl.ANY)
    return pl.pallas_call(
        _ctx_attn_kernel,
        grid=(B,),
        in_specs=[blk(0), blk(1), blk(2), blk(0), blk(1), blk(2), anyspec, anyspec],
        out_specs=[blk(0), blk(0)],
        out_shape=[jax.ShapeDtypeStruct(YA.shape, YA.dtype), jax.ShapeDtypeStruct(YC.shape, YC.dtype)],
        input_output_aliases={6: 0, 7: 1},
        compiler_params=_cparams(1),
        name="ctx_attention",
    )(A, A, A, C, C, C, YA, YC)


def _gla_kernel(bq_ref, lr_ref, wg_ref, bg_ref, tblk_ref, oblk_ref, tri4_ref, o_ref,
                qd_scr, ki_scr, ke_scr, dec_scr, *, S, P, gr):
    C = GLA_CHUNK
    ncs = S // C
    nc = P // C
    dk_h = GLA_DK // GLA_HEADS
    dv_h = GLA_DV // GLA_HEADS

    def thirds(a):
        return a[:, 0:GLA_DK] + a[:, GLA_DK:2 * GLA_DK] + a[:, 2 * GLA_DK:3 * GLA_DK]

    def decays(gi, carry):
        rows = pl.ds(pl.multiple_of(gi * gr, gr), gr)
        q = bq_ref[rows, 0:GLA_DK].astype(F32)
        k = bq_ref[rows, GLA_DK:2 * GLA_DK].astype(F32)
        lr = lr_ref[rows, :]
        for d in range(2):
            z = _dot(lr, wg_ref[d]) + bg_ref[d]
            g = (jnp.minimum(z, 0.0) - jnp.log(1.0 + jnp.exp(-jnp.abs(z)))) * (1.0 / GLA_TAU)
            g3 = jnp.concatenate(_split3(g), axis=1)
            bcum = thirds(_dot(tblk_ref[d], g3))
            btot = thirds(_dot(oblk_ref[...], g3))
            qd_scr[d, rows, :] = (q * (dk_h ** -0.5) * jnp.exp(bcum)).astype(BF16)
            ki_scr[d, rows, :] = (k * jnp.exp(-bcum)).astype(BF16)
            ke_scr[d, rows, :] = (k * jnp.exp(btot - bcum)).astype(BF16)
            dec_scr[d, rows, :] = jnp.exp(btot)
        return carry

    lax.fori_loop(0, P // gr, decays, 0)

    lane_k = lax.broadcasted_iota(I32, (1, GLA_DK), 1) // dk_h
    lane_v = lax.broadcasted_iota(I32, (1, GLA_DV), 1) // dv_h
    bd = (lax.broadcasted_iota(I32, (GLA_DV, GLA_DK), 0) // dv_h
          == lax.broadcasted_iota(I32, (GLA_DV, GLA_DK), 1) // dk_h)

    def chunk(d, ci, st):
        rows = pl.ds(pl.multiple_of(ci * C, C), C)
        q_dec = qd_scr[d, rows, :]
        k_inv = ki_scr[d, rows, :]
        v = bq_ref[rows, 2 * GLA_DK:2 * GLA_DK + GLA_DV]
        k_stack = jnp.concatenate(
            [jnp.where(lane_k == h, k_inv, jnp.zeros_like(k_inv)) for h in range(GLA_HEADS)], axis=0)
        a_cat = (_dot_nt(q_dec, k_stack) * tri4_ref[d]).astype(BF16)
        v_stack = jnp.concatenate(
            [jnp.where(lane_v == h, v, jnp.zeros_like(v)) for h in range(GLA_HEADS)], axis=0)
        lhs = jnp.concatenate([a_cat, q_dec], axis=1)
        rhs = jnp.concatenate([v_stack, jnp.transpose(st).astype(BF16)], axis=0)
        o_ref[d, rows, :] = _dot(lhs, rhs)
        v_t = jnp.transpose(v.astype(F32)).astype(BF16)
        ds_t = _dot(v_t, ke_scr[d, rows, :])
        return st * dec_scr[d, pl.ds(pl.multiple_of(ci * C, C), 1), :] + jnp.where(bd, ds_t, 0.0)

    def body(n, sts):
        cb = jnp.where(n < ncs, ncs - 1 - n, nc - 1 - (n - ncs))
        return chunk(0, n, sts[0]), chunk(1, cb, sts[1])

    zero = jnp.zeros((GLA_DV, GLA_DK), F32)
    lax.fori_loop(0, nc, body, (zero, zero), unroll=4)


def _gla(Bq, LR, wg_pad, bg, tri4, *, B, S, T):
    N = Bq.shape[0]
    P = S + T
    gr = 256 if P % 256 == 0 else 2 * GLA_CHUNK
    per = gr // GLA_CHUNK
    ii = np.arange(GLA_CHUNK)
    tril = (ii[:, None] >= ii[None, :]).astype(np.float32)
    eye = np.eye(per, dtype=np.float32)
    tblk = jnp.asarray(np.stack([np.kron(eye, tril), np.kron(eye, tril.T)]), BF16)
    oblk = jnp.asarray(np.kron(eye, np.ones((GLA_CHUNK, GLA_CHUNK), np.float32)), BF16)
    full = lambda shape: pl.BlockSpec(shape, lambda b: (0,) * len(shape))
    return pl.pallas_call(
        functools.partial(_gla_kernel, S=S, P=P, gr=gr),
        grid=(B,),
        in_specs=[pl.BlockSpec((P, 768), lambda b: (b, 0)),
                  pl.BlockSpec((P, LANES), lambda b: (b, 0)),
                  full((2, LANES, GLA_DK)), full((2, 1, GLA_DK)),
                  full((2, gr, gr)), full((gr, gr)),
                  full((2, GLA_CHUNK, GLA_HEADS * GLA_CHUNK))],
        out_specs=pl.BlockSpec((2, P, GLA_DV), lambda b: (0, b, 0)),
        out_shape=jax.ShapeDtypeStruct((2, N, GLA_DV), F32),
        scratch_shapes=[pltpu.VMEM((2, P, GLA_DK), BF16), pltpu.VMEM((2, P, GLA_DK), BF16),
                        pltpu.VMEM((2, P, GLA_DK), BF16), pltpu.VMEM((2, P, GLA_DK), F32)],
        compiler_params=_cparams(1),
        name="gla",
    )(Bq, LR, wg_pad, bg, tblk, oblk, tri4)


def _merge_kernel(ya_ref, of_ref, ob_ref, rb_ref, yc_ref, gt_ref, x_ref,
                  wa_ref, wb_ref, wc_ref, wo_ref, seg_ref, gg_ref, n2_ref, wr_ref,
                  g1_ref, sc2_ref, sh2_ref, g1c_ref, sc2c_ref, sh2c_ref,
                  x1_ref, xn_ref, aff_ref, *, S):
    tr = x_ref.shape[0]
    first = pl.program_id(1) == 0
    parts = [(slice(0, S), jnp.where(first, g1c_ref[0], g1_ref[0]),
              jnp.where(first, sc2c_ref[0], sc2_ref[0]), jnp.where(first, sh2c_ref[0], sh2_ref[0]))]
    if S < tr:
        parts.append((slice(S, tr), g1_ref[0], sc2_ref[0], sh2_ref[0]))

    def by_rows(fn):
        return jnp.concatenate([fn(*p) for p in parts], axis=0)

    o = of_ref[0] + ob_ref[0]
    yb = o * lax.rsqrt(_seg_meansq(o, seg_ref) + NORM_EPS) * gg_ref[...]
    yb = (yb * _silu(rb_ref[...].astype(F32))).astype(BF16)

    def gated(i, d):
        return jnp.tanh(gt_ref[:, i * 1024:(i + 1) * 1024].astype(F32)) * d + d

    m2 = (gated(0, _dot(ya_ref[...], wa_ref[...]))
          + gated(1, _dot(yb, wb_ref[...]))
          + gated(2, _dot(yc_ref[...], wc_ref[...])))
    mo = _dot(m2.astype(BF16), wo_ref[...])
    x1 = by_rows(lambda r, g1, sc2, sh2: x_ref[r, :] + (0.5 * g1) * mo[r])
    x1_ref[...] = x1
    y = x1 * lax.rsqrt(jnp.mean(x1 * x1, axis=-1, keepdims=True) + NORM_EPS)
    xn = by_rows(lambda r, g1, sc2, sh2: y[r] * (n2_ref[...] * (1.0 + sc2)) + sh2)
    xn_ref[...] = xn.astype(BF16)
    xh, xl = _split2(xn)
    wh, wl = _split2(wr_ref[...])
    logits = _dot(xh, wh) + (_dot(xl, wh) + _dot(xh, wl))
    e = jnp.exp(logits - logits.max(axis=-1, keepdims=True))
    aff_ref[...] = e / e.sum(axis=-1, keepdims=True)


def _merge(YA, O, Bq, YC, G, X, wa, wb, wc, wo, seg, gg, g1, n2, sc2, sh2, wr, *, B, S, T):
    N, D = X.shape
    P = S + T
    tm = _row_tile(S, P)
    NT = P // tm
    E = wr.shape[1]

    def row(b, j):
        return (b * NT + j, 0)

    grp = lambda b, j: (b, 0, 0)
    grp_c = lambda b, j: (B, 0, 0)
    const = lambda b, j: (0, 0)
    mod = lambda m: pl.BlockSpec((1, 1, D), m)
    return pl.pallas_call(
        functools.partial(_merge_kernel, S=S),
        grid=(B, NT),
        in_specs=[pl.BlockSpec((tm, HW), row),
                  pl.BlockSpec((1, tm, HW), lambda b, j: (0, b * NT + j, 0)),
                  pl.BlockSpec((1, tm, HW), lambda b, j: (1, b * NT + j, 0)),
                  pl.BlockSpec((tm, HW), lambda b, j: (b * NT + j, 2)),
                  pl.BlockSpec((tm, HW), row),
                  pl.BlockSpec((tm, 3072), row),
                  pl.BlockSpec((tm, D), row),
                  pl.BlockSpec((HW, D), const), pl.BlockSpec((HW, D), const), pl.BlockSpec((HW, D), const),
                  pl.BlockSpec((D, D), const),
                  pl.BlockSpec((HW, HW), const),
                  pl.BlockSpec((1, HW), const),
                  pl.BlockSpec((1, D), const),
                  pl.BlockSpec((D, E), const),
                  mod(grp), mod(grp), mod(grp), mod(grp_c), mod(grp_c), mod(grp_c)],
        out_specs=[pl.BlockSpec((tm, D), row), pl.BlockSpec((tm, D), row), pl.BlockSpec((tm, E), row)],
        out_shape=[jax.ShapeDtypeStruct((N, D), F32), jax.ShapeDtypeStruct((N, D), BF16),
                   jax.ShapeDtypeStruct((N, E), F32)],
        compiler_params=_cparams(2),
        name="merge",
    )(YA, O, O, Bq, YC, G, X, wa, wb, wc, wo, seg, gg, n2, wr, g1, sc2, sh2, g1, sc2, sh2)


def _cumsum_lanes(m01, upper, before, n):
    ahead = _dot(m01, before)
    return jnp.concatenate(
        [_dot(m01[:, c * LANES:(c + 1) * LANES], upper) + ahead[:, c:c + 1] for c in range(n // LANES)], axis=1)


def _kth_largest_keys(keys, caps):
    def step(i, thetas):
        bit = jnp.left_shift(jnp.int32(1), 30 - i)
        out = []
        for key, cap, theta in zip(keys, caps, thetas):
            cand = theta | bit
            cnt = jnp.sum(jnp.where(key >= cand, 1.0, 0.0), axis=1, keepdims=True)
            out.append(jnp.where(cnt >= cap, cand, theta))
        return tuple(out)

    zero = jnp.zeros((keys[0].shape[0], 1), I32)
    return lax.fori_loop(0, 31, step, tuple(zero for _ in keys))


def _topk_mask(key, theta, cap, upper, before):
    n = key.shape[1]
    gt = key > theta
    eq = key == theta
    need = cap - jnp.sum(jnp.where(gt, 1.0, 0.0), axis=1, keepdims=True)
    rank = _cumsum_lanes(jnp.where(eq, 1.0, 0.0).astype(BF16), upper, before[0:n], n)
    take = eq & (rank <= need)
    return jnp.where(gt | take, 1.0, 0.0)


def _select_kernel(aff_ref, upper_ref, before_ref, ind_ref, slot_ref, w_ref, off_ref, *, S, T, cap_c, cap_l):
    aff = aff_ref[...]
    upper = upper_ref[...]
    before = before_ref[...]
    key_c = lax.bitcast_convert_type(aff[:, 0:S], I32)
    key_l = lax.bitcast_convert_type(aff[:, S:S + T], I32)
    theta_c, theta_l = _kth_largest_keys((key_c, key_l), (cap_c, cap_l))
    sel = jnp.concatenate([_topk_mask(key_c, theta_c, cap_c, upper, before),
                           _topk_mask(key_l, theta_l, cap_l, upper, before)], axis=1)
    sel16 = sel.astype(BF16)
    pos = _cumsum_lanes(sel16, upper, before, S + T) - sel
    chosen = sel > 0.5
    slot_ref[0] = jnp.where(chosen, pos, -1.0).astype(I32)
    w_ref[0] = jnp.where(chosen, aff, 0.0)
    off_ref[0] = _dot(sel16, ind_ref[...]).astype(I32)


def _select(AFF, *, B, S, T, tm, cap_c, cap_l):
    E = AFF.shape[0]
    P = S + T
    il = np.arange(LANES)
    ip = np.arange(P)
    upper = jnp.asarray(il[:, None] <= il[None, :], BF16)
    before = jnp.asarray(ip[:, None] < (il * LANES)[None, :], BF16)
    ind = jnp.asarray(ip[:, None] < (il * tm)[None, :], BF16)
    return pl.pallas_call(
        functools.partial(_select_kernel, S=S, T=T, cap_c=cap_c, cap_l=cap_l),
        grid=(B,),
        in_specs=[pl.BlockSpec((E, P), lambda b: (0, b)),
                  pl.BlockSpec((LANES, LANES), lambda b: (0, 0)),
                  pl.BlockSpec((P, LANES), lambda b: (0, 0)),
                  pl.BlockSpec((P, LANES), lambda b: (0, 0))],
        out_specs=[pl.BlockSpec((1, E, P), lambda b: (b, 0, 0)),
                   pl.BlockSpec((1, E, P), lambda b: (b, 0, 0)),
                   pl.BlockSpec((1, E, LANES), lambda b: (b, 0, 0))],
        out_shape=[jax.ShapeDtypeStruct((B, E, P), I32), jax.ShapeDtypeStruct((B, E, P), F32),
                   jax.ShapeDtypeStruct((B, E, LANES), I32)],
        compiler_params=_cparams(1),
        name="ec_select",
    )(AFF, upper, before, ind)


EC_WIN = 64
SLOT_ALIGN = 16


def _win_base(lo, capt):
    return pl.multiple_of(jnp.minimum(lo // SLOT_ALIGN * SLOT_ALIGN, capt - EC_WIN), SLOT_ALIGN)


def _n_windows(tm):
    return -(-(tm + SLOT_ALIGN - 1) // EC_WIN)


def _any_overflow(offs, bases, E):
    over = None
    for e in range(E):
        c = offs(e)[1] > bases[e] + EC_WIN
        over = c if over is None else (over | c)
    return over


def _gather_kernel(off_ref, x_ref, wc_ref, slot_ref, xg_ref, wg_ref, *, E, NT, tm, capt):
    b = pl.program_id(0)
    j = pl.program_id(1)
    W = EC_WIN

    @pl.when(j == 0)
    def _():
        xg_ref[...] = jnp.zeros_like(xg_ref)
        wg_ref[...] = jnp.zeros_like(wg_ref)

    def offs(e):
        i = (b * E + e) * (NT + 1) + j
        return off_ref[i], off_ref[i + 1]

    rows = lax.broadcasted_iota(I32, (W, tm), 0)
    x = x_ref[...]
    wc = wc_ref[...]
    bases = [_win_base(offs(e)[0], capt) for e in range(E)]
    lhs = jnp.concatenate(
        [jnp.where(rows == slot_ref[0, e:e + 1, :] - bases[e], 1.0, 0.0).astype(BF16) for e in range(E)], axis=0)
    rx = _dot(lhs, x).astype(BF16)
    rw = _dot(lhs, wc)
    for e in range(E):
        xg_ref[e, pl.ds(bases[e], W), :] += rx[e * W:(e + 1) * W]
        wg_ref[e, pl.ds(bases[e], W), :] += rw[e * W:(e + 1) * W]

    def extra(e, carry):
        lo, hi = offs(e)
        base = _win_base(lo, capt)
        for w in range(1, _n_windows(tm)):
            first = base + w * W

            @pl.when(hi > first)
            def _():
                start = pl.multiple_of(jnp.minimum(first, capt - W), SLOT_ALIGN)
                s = slot_ref[0, pl.ds(e, 1), :]
                onehot = jnp.where(rows == jnp.where(s >= first, s - start, -1), 1.0, 0.0).astype(BF16)
                xg_ref[e, pl.ds(start, W), :] += _dot(onehot, x).astype(BF16)
                wg_ref[e, pl.ds(start, W), :] += _dot(onehot, wc)
        return carry

    @pl.when(_any_overflow(offs, bases, E))
    def _():
        lax.fori_loop(0, E, extra, 0)


def _gather(off, XN, WC, slot, *, B, E, S, T, tm, capt):
    N, D = XN.shape
    P = S + T
    NT = P // tm
    row = lambda b, j, off: (b * NT + j, 0)
    return pl.pallas_call(
        functools.partial(_gather_kernel, E=E, NT=NT, tm=tm, capt=capt),
        grid_spec=pltpu.PrefetchScalarGridSpec(
            num_scalar_prefetch=1,
            grid=(B, NT),
            in_specs=[pl.BlockSpec((tm, D), row),
                      pl.BlockSpec((tm, LANES), row),
                      pl.BlockSpec((1, E, tm), lambda b, j, off: (b, 0, j))],
            out_specs=[pl.BlockSpec((E, capt, D), lambda b, j, off: (0, b, 0)),
                       pl.BlockSpec((E, capt, LANES), lambda b, j, off: (0, b, 0))]),
        out_shape=[jax.ShapeDtypeStruct((E, B * capt, D), BF16),
                   jax.ShapeDtypeStruct((E, B * capt, LANES), F32)],
        compiler_params=_cparams(2),
        name="ec_gather",
    )(off, XN, WC, slot)


def _ffn_kernel(x_ref, val_ref, wg_ref, wu_ref, wd_ref, o_ref, acc_ref, *, E):
    f = pl.program_id(2)

    def partial_down():
        x = x_ref[0]
        g = _dot(x, wg_ref[0, 0].astype(BF16))
        u = _dot(x, wu_ref[0, 0].astype(BF16))
        h = (_silu(g) * u).astype(BF16)
        return _dot(h, wd_ref[0, 0].astype(BF16))

    @pl.when(f == 0)
    def _():
        acc_ref[...] = partial_down()

    @pl.when(f > 0)
    def _():
        acc_ref[...] += partial_down()

    @pl.when(f == pl.num_programs(2) - 1)
    def _():
        lane = lax.broadcasted_iota(I32, (1, LANES), 1)
        mine = (lane % E == pl.program_id(0)) & (lane < 3 * E)
        val = jnp.sum(jnp.where(mine, val_ref[0], 0.0), axis=1, keepdims=True)
        o_ref[0] = (acc_ref[...] * val).astype(BF16)


def _ffn(XG, WG, w_gate, w_up, w_down, l, *, rows, fc):
    E, R, D = XG.shape
    F = w_gate.shape[-1]
    return pl.pallas_call(
        functools.partial(_ffn_kernel, E=E),
        grid=(E, R // rows, F // fc),
        in_specs=[pl.BlockSpec((1, rows, D), lambda e, r, f: (e, r, 0)),
                  pl.BlockSpec((1, rows, LANES), lambda e, r, f: (e, r, 0)),
                  pl.BlockSpec((1, 1, D, fc), lambda e, r, f: (l, e, 0, f)),
                  pl.BlockSpec((1, 1, D, fc), lambda e, r, f: (l, e, 0, f)),
                  pl.BlockSpec((1, 1, fc, D), lambda e, r, f: (l, e, f, 0))],
        out_specs=pl.BlockSpec((1, rows, D), lambda e, r, f: (e, r, 0)),
        out_shape=jax.ShapeDtypeStruct((E, R, D), BF16),
        scratch_shapes=[pltpu.VMEM((rows, D), F32)],
        compiler_params=_cparams(3),
        name="ec_ffn",
    )(XG, WG, w_gate, w_up, w_down)


def _combine_kernel(off_ref, y_ref, shl_ref, spread_ref, slot_ref, x_ref, g2_ref, fg_ref, o_ref, acc_ref,
                    *, E, NT, j0, tm, capt, final):
    b = pl.program_id(0)
    j = pl.program_id(1) + j0
    W = EC_WIN

    def offs(e):
        i = (b * E + e) * (NT + 1) + j
        return off_ref[i], off_ref[i + 1]

    bases = [_win_base(offs(e)[0], capt) for e in range(E)]
    col = lax.broadcasted_iota(I32, (1, E * W), 1)
    target = (col % W).astype(F32)
    for e in range(E):
        target = target + jnp.where(col // W == e, bases[e].astype(F32), 0.0)
    spread = _dot(shl_ref[...], spread_ref[...])
    lhs = jnp.where(spread == target, 1.0, 0.0).astype(BF16)
    rhs = jnp.concatenate([y_ref[e, pl.ds(bases[e], W), :] for e in range(E)], axis=0)
    acc_ref[...] = _dot(lhs, rhs)

    rows = lax.broadcasted_iota(I32, (W, tm), 0)

    def extra(e, carry):
        lo, hi = offs(e)
        base = _win_base(lo, capt)
        for w in range(1, _n_windows(tm)):
            first = base + w * W

            @pl.when(hi > first)
            def _():
                start = pl.multiple_of(jnp.minimum(first, capt - W), SLOT_ALIGN)
                s = slot_ref[0, pl.ds(e, 1), :]
                onehot_t = jnp.where(rows == jnp.where(s >= first, s - start, -1), 1.0, 0.0)
                onehot = jnp.transpose(onehot_t).astype(BF16)
                acc_ref[...] += _dot(onehot, y_ref[e, pl.ds(start, W), :])
        return carry

    @pl.when(_any_overflow(offs, bases, E))
    def _():
        lax.fori_loop(0, E, extra, 0)
    x2 = x_ref[...] + g2_ref[0] * acc_ref[...]
    if final:
        x2 = x2 * lax.rsqrt(jnp.mean(x2 * x2, axis=-1, keepdims=True) + NORM_EPS) * fg_ref[...]
    if final:
        o_ref[0] = x2
    else:
        o_ref[...] = x2


def _combine(off, Y, slotT, slot, X1, g2, fg, *, B, E, S, T, tm, capt, final):
    N, D = X1.shape
    P = S + T
    NT = P // tm
    ns = S // tm
    j0 = ns if final else 0
    shl = jnp.concatenate([slotT >> 4, slotT & 15], axis=1).astype(BF16)
    own = np.kron(np.eye(E, dtype=np.float32), np.ones((1, EC_WIN), np.float32))
    spread = jnp.asarray(np.concatenate([16.0 * own, own], axis=0), BF16)

    def row(b, j, off):
        return (b * NT + j + j0, 0)

    if final:
        out_spec = pl.BlockSpec((1, tm, D), lambda b, j, off: (b, j, 0))
        out_shape = jax.ShapeDtypeStruct((B, T, D), F32)
    else:
        out_spec = pl.BlockSpec((tm, D), row)
        out_shape = jax.ShapeDtypeStruct((N, D), F32)
    return pl.pallas_call(
        functools.partial(_combine_kernel, E=E, NT=NT, j0=j0, tm=tm, capt=capt, final=final),
        grid_spec=pltpu.PrefetchScalarGridSpec(
            num_scalar_prefetch=1,
            grid=(B, NT - j0),
            in_specs=[_resident((E, capt, D), lambda b, j, off: (0, b, 0)),
                      pl.BlockSpec((tm, 2 * E), row),
                      pl.BlockSpec((2 * E, E * EC_WIN), lambda b, j, off: (0, 0)),
                      pl.BlockSpec((1, E, tm), lambda b, j, off: (b, 0, j + j0)),
                      pl.BlockSpec((tm, D), row),
                      pl.BlockSpec((1, 1, D), lambda b, j, off: (jnp.where(j + j0 < ns, B, b), 0, 0)),
                      pl.BlockSpec((1, D), lambda b, j, off: (0, 0))],
            out_specs=out_spec,
            scratch_shapes=[pltpu.VMEM((tm, D), F32)]),
        out_shape=out_shape,
        compiler_params=_cparams(2),
        name="ec_combine",
    )(off, Y, shl, spread, slot, X1, g2, fg)


def _permute_w_in(w):
    D = w.shape[0]
    o = np.cumsum([0, 256, 256, 256, 128, 128, 256, 256, 16, 16, 256, 128, 128, 1024, 1024, 1024])
    qa, ka, va, qb, kb, vb, rb, lf, lb, qc, kc, vc, mga, mgb, mgc = [w[:, o[i]:o[i + 1]] for i in range(15)]
    grp = GQA_HEADS // GQA_KV_HEADS

    def rep(t):
        return jnp.concatenate([t[:, (h // grp) * HEAD_DIM:(h // grp + 1) * HEAD_DIM] for h in range(GQA_HEADS)], axis=1)

    pad = jnp.zeros((D, COL_END - COL_LR - 2 * GLA_LOWRANK), w.dtype)
    gates = 0.5 * jnp.concatenate([mga, mgb, mgc], axis=1)
    return jnp.concatenate([qa, ka, va, qb, kb, vb, rb, qc, rep(kc), rep(vc), gates, lf, lb, pad],
                           axis=1).astype(BF16)


def _prep_w_kernel(w_ref, o_ref):
    o_ref[...] = _permute_w_in(w_ref[0])


def _prep_w(w_in, l):
    L, D, C = w_in.shape
    tr = 256
    return pl.pallas_call(
        _prep_w_kernel,
        grid=(D // tr,),
        in_specs=[pl.BlockSpec((1, tr, C), lambda i: (l, i, 0))],
        out_specs=pl.BlockSpec((tr, COL_END), lambda i: (i, 0)),
        out_shape=jax.ShapeDtypeStruct((D, COL_END), BF16),
        compiler_params=_cparams(1),
        name="prep_w_in",
    )(w_in)


def _rope_tables(T, S):
    t = np.arange(T)
    row = (t // GRID_W).astype(np.float32)
    col = (t % GRID_W).astype(np.float32)
    lane = np.arange(HEAD_DIM)
    is_col = lane >= HEAD_DIM // 2
    fi = lane % 16
    inv_freq = ROPE_BASE ** (-jnp.arange(16, dtype=F32) / 16)
    pos = jnp.where(jnp.asarray(is_col)[None, :], jnp.asarray(col)[:, None], jnp.asarray(row)[:, None])
    ang = pos * inv_freq[jnp.asarray(fi)][None, :]
    cos = jnp.cos(ang)
    sin = jnp.sin(ang)
    sin = jnp.where(jnp.asarray((lane % 32) < 16)[None, :], -sin, sin)
    cos = jnp.concatenate([jnp.ones((S, HW), F32), jnp.tile(cos, (1, HW // HEAD_DIM))], axis=0)
    sin = jnp.concatenate([jnp.zeros((S, HW), F32), jnp.tile(sin, (1, HW // HEAD_DIM))], axis=0)
    return cos, sin


def kernel(x, c, ctx, c_ctx, w_ada, b_ada, norm1_g, norm2_g, w_in, na_rpb, gla_wg_f, gla_bg_f, gla_wg_b,
           gla_bg_b, gla_norm_g, gqa_qn_g, gqa_kn_g, w_branch_a, w_branch_b, w_branch_c, w_out, w_router,
           w_e_gate, w_e_up, w_e_down, final_norm_g):
    B, T, D = x.shape
    S = ctx.shape[1]
    L = w_ada.shape[0]
    E = w_router.shape[-1]
    P = S + T
    tm = min(256, S)
    NT = P // tm
    assert S % tm == 0 and T % tm == 0 and S % GRID_W == 0 and T % GRID_W == 0 and P % S == 0
    cap_l = EC_CAPACITY_FACTOR * T // E
    cap_ctx = EC_CAPACITY_FACTOR * S // E
    assert cap_l % SLOT_ALIGN == 0 and cap_ctx % SLOT_ALIGN == 0 and cap_l >= EC_WIN
    assert 3 * E <= LANES and E % (LANES // EC_WIN) == 0
    R = T // GRID_W
    kh = min(NA_KH, R)

    seg = jnp.asarray(np.kron(np.eye(HW // HEAD_DIM), np.ones((HEAD_DIM, HEAD_DIM))), BF16)
    cos_t, sin_t = _rope_tables(T, S)
    ii = np.arange(GLA_CHUNK)
    tril = (ii[:, None] >= ii[None, :]).astype(np.float32)
    tri4 = jnp.asarray(np.stack([np.tile(tril, (1, GLA_HEADS)), np.tile(tril.T, (1, GLA_HEADS))]))

    RP = -(-(B + 1) // 8) * 8
    cv = jnp.concatenate([c, c_ctx[None], jnp.zeros((RP - B - 1, D), F32)], axis=0)
    mod = _adaln(cv, w_ada, b_ada)[:, :B + 1].reshape(L, B + 1, 6, 1, D)

    X = jnp.concatenate([ctx, x], axis=1).reshape(B * P, D)
    out = None
    for l in range(L):
        last = l == L - 1
        sh1, sc1, g1, sh2, sc2, g2 = [mod[l, :, i] for i in range(6)]
        w_p = _prep_w(w_in, l)
        qn = jnp.tile(gqa_qn_g[l], HW // HEAD_DIM)[None]
        kn = jnp.tile(gqa_kn_g[l], HW // HEAD_DIM)[None]
        A, Bq, C, G, LR = _proj_in(X, norm1_g[l][None], sc1, sh1, w_p, seg, qn, kn, cos_t, sin_t,
                                   B=B, S=S, T=T)

        YA = _na_attention(A, _na_bias_table(na_rpb[l], kh), B=B, S=S, T=T)
        YC = _gqa_attention(C, B=B, S=S, T=T, tq=min(256, S))
        YA, YC = _ctx_attention(A, C, YA, YC, B=B, S=S, T=T)

        zpad = jnp.zeros((LANES - 2 * GLA_LOWRANK, GLA_DK), F32)
        zlr = jnp.zeros((GLA_LOWRANK, GLA_DK), F32)
        wg_pad = jnp.stack([jnp.concatenate([gla_wg_f[l], zlr, zpad], axis=0),
                            jnp.concatenate([zlr, gla_wg_b[l], zpad], axis=0)]).astype(BF16)
        bg = jnp.stack([gla_bg_f[l], gla_bg_b[l]])[:, None, :]
        O = _gla(Bq, LR, wg_pad, bg, tri4, B=B, S=S, T=T)

        gg = jnp.tile(gla_norm_g[l], GLA_HEADS)[None]
        X1, XN, AFFT = _merge(YA, O, Bq, YC, G, X,
                              w_branch_a[l].astype(BF16), w_branch_b[l].astype(BF16),
                              w_branch_c[l].astype(BF16), w_out[l].astype(BF16),
                              seg, gg, g1, norm2_g[l][None], sc2, sh2, w_router[l],
                              B=B, S=S, T=T)

        cap_c = 0 if last else cap_ctx
        capt = cap_l + cap_c
        slot, wsel, off = _select(jnp.transpose(AFFT), B=B, S=S, T=T, tm=tm, cap_c=cap_c, cap_l=cap_l)
        off_flat = off[:, :, :NT + 1].reshape(-1)
        slotT = jnp.swapaxes(slot, 1, 2).reshape(B * P, E)
        wT = jnp.swapaxes(wsel, 1, 2).reshape(B * P, E)
        WC = jnp.concatenate(list(_split3(wT)) + [jnp.zeros((B * P, LANES - 3 * E), BF16)], axis=1)
        XG, WG = _gather(off_flat, XN, WC, slot, B=B, E=E, S=S, T=T, tm=tm, capt=capt)
        rows = capt * (2 if B % 2 == 0 else 1)
        Y = _ffn(XG, WG, w_e_gate, w_e_up, w_e_down, l, rows=rows, fc=min(512, w_e_gate.shape[-1]))
        res = _combine(off_flat, Y, slotT, slot, X1, g2, final_norm_g[None],
                       B=B, E=E, S=S, T=T, tm=tm, capt=capt, final=last)
        if last:
            out = res
        else:
            X = res
    return out
```

```python
import functools

import jax
import jax.numpy as jnp
import numpy as np
from jax import lax
from jax.experimental import pallas as pl
from jax.experimental.pallas import tpu as pltpu

F32 = jnp.float32
BF16 = jnp.bfloat16
I32 = jnp.int32

GRID_W = 64
HEAD_DIM = 64
NORM_EPS = 1e-6
NA_HEADS = 4
NA_KH = 8
NA_KW = 16
GLA_HEADS = 4
GLA_DK = 128
GLA_DV = 256
GLA_LOWRANK = 16
GLA_TAU = 16.0
GLA_CHUNK = 64
GQA_HEADS = 4
GQA_KV_HEADS = 2
ROPE_BASE = 10000.0
N_EXPERTS = 16
EC_CAPACITY_FACTOR = 2
NEG_BIG = -1e30

LANES = 128
HW = 256
VMEM_LIMIT = 56 * 1024 * 1024

COL_A = 0
COL_B = 768
COL_C = 1536
COL_G = 2304
COL_LR = 5376
COL_END = 5504


def _cparams(n_grid):
    return pltpu.CompilerParams(dimension_semantics=("arbitrary",) * n_grid,
                                vmem_limit_bytes=VMEM_LIMIT)


def _resident(block_shape, index_map):
    return pl.BlockSpec(block_shape, index_map, pipeline_mode=pl.Buffered(1))


def _dot(a, b):
    return jnp.dot(a, b, preferred_element_type=F32)


def _dot_nt(a, b):
    return lax.dot_general(a, b, (((1,), (1,)), ((), ())), preferred_element_type=F32)


def _split2(a):
    hi = a.astype(BF16)
    lo = (a - hi.astype(F32)).astype(BF16)
    return hi, lo


def _split3(a):
    h1 = a.astype(BF16)
    r1 = a - h1.astype(F32)
    h2 = r1.astype(BF16)
    h3 = (r1 - h2.astype(F32)).astype(BF16)
    return h1, h2, h3


def _seg_meansq(x, seg_ref):
    hi, lo = _split2(x * x)
    seg = seg_ref[...]
    return (_dot(hi, seg) + _dot(lo, seg)) * (1.0 / HEAD_DIM)


def _sigmoid(x):
    return 0.5 * jnp.tanh(0.5 * x) + 0.5


def _silu(x):
    return x * _sigmoid(x)


def _adaln_kernel(cv_ref, w_ref, b_ref, o_ref):
    s = _silu(cv_ref[...])
    w = w_ref[0]
    s1, s2, s3 = _split3(s)
    w1, w2, w3 = _split3(w)
    acc = _dot(s1, w1) + (_dot(s1, w2) + _dot(s2, w1)) + (_dot(s2, w2) + _dot(s1, w3) + _dot(s3, w1))
    o_ref[0] = acc + b_ref[0]


def _adaln(cv, w_ada, b_ada):
    L, D, D6 = w_ada.shape
    R = cv.shape[0]
    tn = 1024
    return pl.pallas_call(
        _adaln_kernel,
        grid=(L, D6 // tn),
        in_specs=[pl.BlockSpec((R, D), lambda l, n: (0, 0)),
                  pl.BlockSpec((1, D, tn), lambda l, n: (l, 0, n)),
                  pl.BlockSpec((1, 1, tn), lambda l, n: (l, 0, n))],
        out_specs=pl.BlockSpec((1, R, tn), lambda l, n: (l, 0, n)),
        out_shape=jax.ShapeDtypeStruct((L, R, D6), F32),
        compiler_params=_cparams(2),
        name="adaln",
    )(cv, w_ada, b_ada.reshape(L, 1, D6))


def _swap16(y):
    lane = lax.broadcasted_iota(I32, y.shape, 1)
    first = (lane % 32) < 16
    return jnp.where(first, pltpu.roll(y, LANES - 16, 1), pltpu.roll(y, 16, 1))


def _norm_rope(x, seg_ref, g_ref, cos_ref, sin_ref, scale):
    y = x * lax.rsqrt(_seg_meansq(x, seg_ref) + NORM_EPS) * g_ref[...]
    halves = []
    for i in range(HW // LANES):
        sl = slice(i * LANES, (i + 1) * LANES)
        yh = y[:, sl]
        halves.append(yh * cos_ref[:, sl] + _swap16(yh) * sin_ref[:, sl])
    out = jnp.concatenate(halves, axis=1)
    return out * scale if scale != 1.0 else out


def _proj_in_kernel(x_ref, g_ref, sc_ref, sh_ref, scc_ref, shc_ref, w_ref, seg_ref, qn_ref, kn_ref,
                    cos_ref, sin_ref, a_ref, b_ref, c_ref, gt_ref, lr_ref, *, S):
    tr = x_ref.shape[0]
    first = pl.program_id(1) == 0
    parts = [(slice(0, S), jnp.where(first, scc_ref[0], sc_ref[0]), jnp.where(first, shc_ref[0], sh_ref[0]))]
    if tr > S:
        parts.append((slice(S, tr), sc_ref[0], sh_ref[0]))
    qscale = HEAD_DIM ** -0.5
    for r, sc, sh in parts:
        x = x_ref[r, :]
        y = x * lax.rsqrt(jnp.mean(x * x, axis=-1, keepdims=True) + NORM_EPS)
        u = (y * (g_ref[...] * (1.0 + sc)) + sh).astype(BF16)

        def proj(c0, c1):
            return _dot(u, w_ref[:, c0:c1])

        def rope(t, gain_ref, scale):
            return _norm_rope(t, seg_ref, gain_ref, cos_ref[r, :], sin_ref[r, :], scale)

        a_ref[r, 0:HW] = (proj(COL_A, COL_A + HW) * qscale).astype(BF16)
        a_ref[r, HW:3 * HW] = proj(COL_A + HW, COL_B).astype(BF16)
        b_ref[r, :] = proj(COL_B, COL_C).astype(BF16)
        c_ref[r, 0:HW] = rope(proj(COL_C, COL_C + HW), qn_ref, qscale).astype(BF16)
        c_ref[r, HW:2 * HW] = rope(proj(COL_C + HW, COL_C + 2 * HW), kn_ref, 1.0).astype(BF16)
        c_ref[r, 2 * HW:3 * HW] = proj(COL_C + 2 * HW, COL_G).astype(BF16)
        for i in range(3):
            gt_ref[r, i * 1024:(i + 1) * 1024] = proj(COL_G + i * 1024, COL_G + (i + 1) * 1024).astype(BF16)
        lr_ref[r, :] = proj(COL_LR, COL_END).astype(BF16)


def _row_tile(S, P):
    for k in (8, 4, 2, 1):
        if P % k == 0 and (P // k) % 16 == 0 and P // k >= S:
            return P // k
    raise ValueError("no row tile for these shapes")


def _proj_in(X, g, sc, sh, w_p, seg, qn, kn, cos_t, sin_t, *, B, S, T):
    N, D = X.shape
    P = S + T
    tm = _row_tile(S, P)
    NT = P // tm

    def row(b, j):
        return (b * NT + j, 0)

    grp = lambda b, j: (b, 0, 0)
    grp_c = lambda b, j: (B, 0, 0)
    tab = lambda b, j: (j, 0)
    const = lambda b, j: (0, 0)
    outs = [jax.ShapeDtypeStruct((N, 768), BF16), jax.ShapeDtypeStruct((N, 768), BF16),
            jax.ShapeDtypeStruct((N, 768), BF16), jax.ShapeDtypeStruct((N, 3072), BF16),
            jax.ShapeDtypeStruct((N, LANES), BF16)]
    return pl.pallas_call(
        functools.partial(_proj_in_kernel, S=S),
        grid=(B, NT),
        in_specs=[pl.BlockSpec((tm, D), row),
                  pl.BlockSpec((1, D), const),
                  pl.BlockSpec((1, 1, D), grp),
                  pl.BlockSpec((1, 1, D), grp),
                  pl.BlockSpec((1, 1, D), grp_c),
                  pl.BlockSpec((1, 1, D), grp_c),
                  _resident((D, COL_END), const),
                  pl.BlockSpec((HW, HW), const),
                  pl.BlockSpec((1, HW), const),
                  pl.BlockSpec((1, HW), const),
                  pl.BlockSpec((tm, HW), tab),
                  pl.BlockSpec((tm, HW), tab)],
        out_specs=[pl.BlockSpec((tm, 768), row), pl.BlockSpec((tm, 768), row),
                   pl.BlockSpec((tm, 768), row), pl.BlockSpec((tm, 3072), row),
                   pl.BlockSpec((tm, LANES), row)],
        out_shape=outs,
        compiler_params=_cparams(2),
        name="proj_in",
    )(X, g, sc, sh, sc, sh, w_p, seg, qn, kn, cos_t, sin_t)


def _mh_attend(q, segments):
    nq = q.shape[0]
    nh = HW // HEAD_DIM
    lane = lax.broadcasted_iota(I32, (1, HW), 1) // HEAD_DIM
    qs = jnp.concatenate([jnp.where(lane == h, q, jnp.zeros_like(q)) for h in range(nh)], axis=0)
    mx = den = o = None
    for k, v, bias in segments:
        s = _dot_nt(qs, k)
        if bias is not None:
            s = s + bias
        smax = s.max(axis=-1, keepdims=True)
        if mx is None:
            mx = smax
            p = jnp.exp(s - mx)
            den = p.sum(axis=-1, keepdims=True)
            o = _dot(p.astype(BF16), v)
        else:
            mx_new = jnp.maximum(mx, smax)
            alpha = jnp.exp(mx - mx_new)
            p = jnp.exp(s - mx_new)
            den = den * alpha + p.sum(axis=-1, keepdims=True)
            o = o * alpha + _dot(p.astype(BF16), v)
            mx = mx_new
    o = o / den
    acc = jnp.zeros(q.shape, F32)
    for h in range(nh):
        acc = acc + jnp.where(lane == h, o[h * nq:(h + 1) * nq], 0.0)
    return acc


def _na_kernel(q_ref, k_ref, v_ref, bias_ref, o_ref, *, S, R, kh, rb):
    for i in range(rb):
        r = pl.program_id(1) * rb + i
        r0 = jnp.clip(r - kh // 2, 0, R - kh)
        start = pl.multiple_of(S + r0 * GRID_W, GRID_W)
        kwin = k_ref[pl.ds(start, kh * GRID_W), :]
        vwin = v_ref[pl.ds(start, kh * GRID_W), :]
        segs = [(kwin, vwin, bias_ref[r - r0]),
                (k_ref[0:S, :], v_ref[0:S, :], None)]
        rows = slice(i * GRID_W, (i + 1) * GRID_W)
        o_ref[rows, :] = _mh_attend(q_ref[rows, :], segs).astype(BF16)


def _na_bias_table(rpb, kh):
    H = rpb.shape[0]
    W = GRID_W
    col = np.arange(W)
    col_start = np.clip(col - NA_KW // 2, 0, W - NA_KW)
    inwin = (col[None, :] >= col_start[:, None]) & (col[None, :] < col_start[:, None] + NA_KW)
    rp = jnp.pad(rpb.astype(F32), ((0, 0), (0, 0), (W, W)))
    toe = jnp.stack([rp[:, :, W + NA_KW - 1 - q:2 * W + NA_KW - 1 - q] for q in range(W)], axis=2)
    toe = jnp.where(jnp.asarray(inwin)[None, None], toe, NEG_BIG)
    t = jnp.stack([toe[:, NA_KH - 1 - c:NA_KH - 1 - c + kh] for c in range(kh)], axis=0)
    t = jnp.transpose(t, (0, 1, 3, 2, 4))
    return t.reshape(kh, H * W, kh * W)


def _na_attention(A, bias_t, *, B, S, T):
    N = A.shape[0]
    P = S + T
    R = T // GRID_W
    kh = min(NA_KH, R)
    rb = 4 if (R % 4 == 0 and S % (4 * GRID_W) == 0) else 1
    tq = rb * GRID_W
    qoff = S // tq
    per_b = P // tq

    return pl.pallas_call(
        functools.partial(_na_kernel, S=S, R=R, kh=kh, rb=rb),
        grid=(B, R // rb),
        in_specs=[pl.BlockSpec((tq, HW), lambda b, r: (b * per_b + qoff + r, 0)),
                  pl.BlockSpec((P, HW), lambda b, r: (b, 1)),
                  pl.BlockSpec((P, HW), lambda b, r: (b, 2)),
                  pl.BlockSpec((kh, NA_HEADS * GRID_W, kh * GRID_W), lambda b, r: (0, 0, 0))],
        out_specs=pl.BlockSpec((tq, HW), lambda b, r: (b * per_b + qoff + r, 0)),
        out_shape=jax.ShapeDtypeStruct((N, HW), BF16),
        compiler_params=_cparams(2),
        name="na_attention",
    )(A, A, A, bias_t)


def _key_chunks(n, target=512, align=256):
    nchunks = max(1, n // target)
    base = (n // nchunks) // align * align
    if base == 0:
        return [(0, n)]
    bounds = [i * base for i in range(nchunks)] + [n]
    return [(bounds[i], bounds[i + 1]) for i in range(nchunks)]


def _gqa_kernel(q_ref, k_ref, v_ref, o_ref):
    q = q_ref[...]
    nq = q.shape[0]
    nh = HW // HEAD_DIM
    lane = lax.broadcasted_iota(I32, (1, HW), 1) // HEAD_DIM
    odd = (lane % 2) == 1
    qs = jnp.concatenate([jnp.where(lane == h, q, jnp.zeros_like(q)) for h in range(nh)], axis=0)
    mx = o = None
    for c0, c1 in _key_chunks(k_ref.shape[0]):
        v = v_ref[c0:c1, :]
        v1 = jnp.where(odd, jnp.ones_like(v), v)
        s = _dot_nt(qs, k_ref[c0:c1, :])
        smax = s.max(axis=-1, keepdims=True)
        if mx is None:
            mx = smax
            o = _dot(jnp.exp(s - mx).astype(BF16), v1)
        else:
            mx_new = jnp.maximum(mx, smax)
            o = o * jnp.exp(mx - mx_new) + _dot(jnp.exp(s - mx_new).astype(BF16), v1)
            mx = mx_new
    o = o / o[:, HEAD_DIM:HEAD_DIM + 1]
    acc = jnp.zeros(q.shape, F32)
    for h in range(nh):
        blk = o[h * nq:(h + 1) * nq]
        if h % 2 == 1:
            blk = jnp.concatenate([pltpu.roll(blk[:, i:i + LANES], HEAD_DIM, 1) for i in range(0, HW, LANES)], axis=1)
        acc = acc + jnp.where(lane == h, blk, 0.0)
    o_ref[...] = acc.astype(BF16)


def _gqa_attention(C, *, B, S, T, tq):
    N = C.shape[0]
    P = S + T
    nq = T // tq
    qoff = S // tq
    per_b = P // tq
    return pl.pallas_call(
        _gqa_kernel,
        grid=(B, nq),
        in_specs=[pl.BlockSpec((tq, HW), lambda b, i: (b * per_b + qoff + i, 0)),
                  pl.BlockSpec((P, HW), lambda b, i: (b, 1)),
                  pl.BlockSpec((P, HW), lambda b, i: (b, 2))],
        out_specs=pl.BlockSpec((tq, HW), lambda b, i: (b * per_b + qoff + i, 0)),
        out_shape=jax.ShapeDtypeStruct((N, HW), BF16),
        compiler_params=_cparams(2),
        name="gqa_attention",
    )(C, C, C)


def _ctx_attn_kernel(qa_ref, ka_ref, va_ref, qc_ref, kc_ref, vc_ref, ya_in, yc_in, ya_ref, yc_ref):
    del ya_in, yc_in
    ya_ref[...] = _mh_attend(qa_ref[...], [(ka_ref[...], va_ref[...], None)]).astype(BF16)
    yc_ref[...] = _mh_attend(qc_ref[...], [(kc_ref[...], vc_ref[...], None)]).astype(BF16)


def _ctx_attention(A, C, YA, YC, *, B, S, T):
    per_b = (S + T) // S
    blk = lambda c: pl.BlockSpec((S, HW), lambda b: (b * per_b, c))
    anyspec = pl.BlockSpec(memory_space=pl.ANY)
    return pl.pallas_call(
        _ctx_attn_kernel,
        grid=(B,),
        in_specs=[blk(0), blk(1), blk(2), blk(0), blk(1), blk(2), anyspec, anyspec],
        out_specs=[blk(0), blk(0)],
        out_shape=[jax.ShapeDtypeStruct(YA.shape, YA.dtype), jax.ShapeDtypeStruct(YC.shape, YC.dtype)],
        input_output_aliases={6: 0, 7: 1},
        compiler_params=_cparams(1),
        name="ctx_attention",
    )(A, A, A, C, C, C, YA, YC)


def _gla_kernel(bq_ref, lr_ref, wg_ref, bg_ref, tblk_ref, oblk_ref, tri4_ref, o_ref,
                qd_scr, ki_scr, ke_scr, dec_scr, *, S, P, gr):
    C = GLA_CHUNK
    ncs = S // C
    nc = P // C
    dk_h = GLA_DK // GLA_HEADS
    dv_h = GLA_DV // GLA_HEADS

    def thirds(a):
        return a[:, 0:GLA_DK] + a[:, GLA_DK:2 * GLA_DK] + a[:, 2 * GLA_DK:3 * GLA_DK]

    def decays(gi, carry):
        rows = pl.ds(pl.multiple_of(gi * gr, gr), gr)
        q = bq_ref[rows, 0:GLA_DK].astype(F32)
        k = bq_ref[rows, GLA_DK:2 * GLA_DK].astype(F32)
        lr = lr_ref[rows, :]
        for d in range(2):
            z = _dot(lr, wg_ref[d]) + bg_ref[d]
            g = (jnp.minimum(z, 0.0) - jnp.log(1.0 + jnp.exp(-jnp.abs(z)))) * (1.0 / GLA_TAU)
            g3 = jnp.concatenate(_split3(g), axis=1)
            bcum = thirds(_dot(tblk_ref[d], g3))
            btot = thirds(_dot(oblk_ref[...], g3))
            qd_scr[d, rows, :] = (q * (dk_h ** -0.5) * jnp.exp(bcum)).astype(BF16)
            ki_scr[d, rows, :] = (k * jnp.exp(-bcum)).astype(BF16)
            ke_scr[d, rows, :] = (k * jnp.exp(btot - bcum)).astype(BF16)
            dec_scr[d, rows, :] = jnp.exp(btot)
        return carry

    lax.fori_loop(0, P // gr, decays, 0, unroll=2)

    lane_k = lax.broadcasted_iota(I32, (1, GLA_DK), 1) // dk_h
    lane_v = lax.broadcasted_iota(I32, (1, GLA_DV), 1) // dv_h
    bd = (lax.broadcasted_iota(I32, (GLA_DV, GLA_DK), 0) // dv_h
          == lax.broadcasted_iota(I32, (GLA_DV, GLA_DK), 1) // dk_h)

    def chunk(d, ci, st):
        rows = pl.ds(pl.multiple_of(ci * C, C), C)
        q_dec = qd_scr[d, rows, :]
        k_inv = ki_scr[d, rows, :]
        v = bq_ref[rows, 2 * GLA_DK:2 * GLA_DK + GLA_DV]
        k_stack = jnp.concatenate(
            [jnp.where(lane_k == h, k_inv, jnp.zeros_like(k_inv)) for h in range(GLA_HEADS)], axis=0)
        a_cat = (_dot_nt(q_dec, k_stack) * tri4_ref[d]).astype(BF16)
        v_stack = jnp.concatenate(
            [jnp.where(lane_v == h, v, jnp.zeros_like(v)) for h in range(GLA_HEADS)], axis=0)
        lhs = jnp.concatenate([a_cat, q_dec], axis=1)
        rhs = jnp.concatenate([v_stack, jnp.transpose(st).astype(BF16)], axis=0)
        o_ref[d, rows, :] = _dot(lhs, rhs)
        v_t = jnp.transpose(v.astype(F32)).astype(BF16)
        ds_t = _dot(v_t, ke_scr[d, rows, :])
        return st * dec_scr[d, pl.ds(pl.multiple_of(ci * C, C), 1), :] + jnp.where(bd, ds_t, 0.0)

    def body(n, sts):
        cb = jnp.where(n < ncs, ncs - 1 - n, nc - 1 - (n - ncs))
        return chunk(0, n, sts[0]), chunk(1, cb, sts[1])

    zero = jnp.zeros((GLA_DV, GLA_DK), F32)
    lax.fori_loop(0, nc, body, (zero, zero), unroll=8)


def _gla(Bq, LR, wg_pad, bg, tri4, *, B, S, T):
    N = Bq.shape[0]
    P = S + T
    gr = 256 if P % 256 == 0 else 2 * GLA_CHUNK
    per = gr // GLA_CHUNK
    ii = np.arange(GLA_CHUNK)
    tril = (ii[:, None] >= ii[None, :]).astype(np.float32)
    eye = np.eye(per, dtype=np.float32)
    tblk = jnp.asarray(np.stack([np.kron(eye, tril), np.kron(eye, tril.T)]), BF16)
    oblk = jnp.asarray(np.kron(eye, np.ones((GLA_CHUNK, GLA_CHUNK), np.float32)), BF16)
    full = lambda shape: pl.BlockSpec(shape, lambda b: (0,) * len(shape))
    return pl.pallas_call(
        functools.partial(_gla_kernel, S=S, P=P, gr=gr),
        grid=(B,),
        in_specs=[pl.BlockSpec((P, 768), lambda b: (b, 0)),
                  pl.BlockSpec((P, LANES), lambda b: (b, 0)),
                  full((2, LANES, GLA_DK)), full((2, 1, GLA_DK)),
                  full((2, gr, gr)), full((gr, gr)),
                  full((2, GLA_CHUNK, GLA_HEADS * GLA_CHUNK))],
        out_specs=pl.BlockSpec((2, P, GLA_DV), lambda b: (0, b, 0)),
        out_shape=jax.ShapeDtypeStruct((2, N, GLA_DV), F32),
        scratch_shapes=[pltpu.VMEM((2, P, GLA_DK), BF16), pltpu.VMEM((2, P, GLA_DK), BF16),
                        pltpu.VMEM((2, P, GLA_DK), BF16), pltpu.VMEM((2, P, GLA_DK), F32)],
        compiler_params=_cparams(1),
        name="gla",
    )(Bq, LR, wg_pad, bg, tblk, oblk, tri4)


def _merge_kernel(ya_ref, of_ref, ob_ref, rb_ref, yc_ref, gt_ref, x_ref,
                  wa_ref, wb_ref, wc_ref, wo_ref, seg_ref, gg_ref, n2_ref, wr_ref,
                  g1_ref, sc2_ref, sh2_ref, g1c_ref, sc2c_ref, sh2c_ref,
                  x1_ref, xn_ref, aff_ref, *, S):
    tr = x_ref.shape[0]
    first = pl.program_id(1) == 0
    head = (jnp.where(first, g1c_ref[0], g1_ref[0]), jnp.where(first, sc2c_ref[0], sc2_ref[0]),
            jnp.where(first, sh2c_ref[0], sh2_ref[0]))
    tail = (g1_ref[0], sc2_ref[0], sh2_ref[0])

    def halves(r0, r1):
        mid = (r0 + r1) // 2
        return [(r0, mid), (mid, r1)] if (r1 - r0) % 32 == 0 and r1 - r0 >= 64 else [(r0, r1)]

    parts = [(slice(0, S),) + head]
    if tr > S:
        parts += [(slice(a, b),) + tail for a, b in halves(S, tr)]

    wh, wl = _split2(wr_ref[...])
    for r, g1, sc2, sh2 in parts:
        o = of_ref[0, r, :] + ob_ref[0, r, :]
        yb = o * lax.rsqrt(_seg_meansq(o, seg_ref) + NORM_EPS) * gg_ref[...]
        yb = (yb * _silu(rb_ref[r, :].astype(F32))).astype(BF16)

        def gated(i, d):
            return jnp.tanh(gt_ref[r, i * 1024:(i + 1) * 1024].astype(F32)) * d + d

        m2 = (gated(0, _dot(ya_ref[r, :], wa_ref[...]))
              + gated(1, _dot(yb, wb_ref[...]))
              + gated(2, _dot(yc_ref[r, :], wc_ref[...])))
        x1 = x_ref[r, :] + (0.5 * g1) * _dot(m2.astype(BF16), wo_ref[...])
        x1_ref[r, :] = x1
        y = x1 * lax.rsqrt(jnp.mean(x1 * x1, axis=-1, keepdims=True) + NORM_EPS)
        xn = y * (n2_ref[...] * (1.0 + sc2)) + sh2
        xn_ref[r, :] = xn.astype(BF16)
        xh, xl = _split2(xn)
        logits = _dot(xh, wh) + (_dot(xl, wh) + _dot(xh, wl))
        e = jnp.exp(logits - logits.max(axis=-1, keepdims=True))
        aff_ref[r, :] = e / e.sum(axis=-1, keepdims=True)


def _merge(YA, O, Bq, YC, G, X, wa, wb, wc, wo, seg, gg, g1, n2, sc2, sh2, wr, *, B, S, T):
    N, D = X.shape
    P = S + T
    tm = _row_tile(S, P)
    NT = P // tm
    E = wr.shape[1]

    def row(b, j):
        return (b * NT + j, 0)

    grp = lambda b, j: (b, 0, 0)
    grp_c = lambda b, j: (B, 0, 0)
    const = lambda b, j: (0, 0)
    mod = lambda m: pl.BlockSpec((1, 1, D), m)
    return pl.pallas_call(
        functools.partial(_merge_kernel, S=S),
        grid=(B, NT),
        in_specs=[pl.BlockSpec((tm, HW), row),
                  pl.BlockSpec((1, tm, HW), lambda b, j: (0, b * NT + j, 0)),
                  pl.BlockSpec((1, tm, HW), lambda b, j: (1, b * NT + j, 0)),
                  pl.BlockSpec((tm, HW), lambda b, j: (b * NT + j, 2)),
                  pl.BlockSpec((tm, HW), row),
                  pl.BlockSpec((tm, 3072), row),
                  pl.BlockSpec((tm, D), row),
                  pl.BlockSpec((HW, D), const), pl.BlockSpec((HW, D), const), pl.BlockSpec((HW, D), const),
                  pl.BlockSpec((D, D), const),
                  pl.BlockSpec((HW, HW), const),
                  pl.BlockSpec((1, HW), const),
                  pl.BlockSpec((1, D), const),
                  pl.BlockSpec((D, E), const),
                  mod(grp), mod(grp), mod(grp), mod(grp_c), mod(grp_c), mod(grp_c)],
        out_specs=[pl.BlockSpec((tm, D), row), pl.BlockSpec((tm, D), row), pl.BlockSpec((tm, E), row)],
        out_shape=[jax.ShapeDtypeStruct((N, D), F32), jax.ShapeDtypeStruct((N, D), BF16),
                   jax.ShapeDtypeStruct((N, E), F32)],
        compiler_params=_cparams(2),
        name="merge",
    )(YA, O, O, Bq, YC, G, X, wa, wb, wc, wo, seg, gg, n2, wr, g1, sc2, sh2, g1, sc2, sh2)


def _cumsum_lanes(m01, upper, before, n):
    ahead = _dot(m01, before)
    return jnp.concatenate(
        [_dot(m01[:, c * LANES:(c + 1) * LANES], upper) + ahead[:, c:c + 1] for c in range(n // LANES)], axis=1)


def _kth_largest_keys(keys, caps):
    def step(i, thetas):
        bit = jnp.left_shift(jnp.int32(1), 30 - i)
        out = []
        for key, cap, theta in zip(keys, caps, thetas):
            cand = theta | bit
            cnt = jnp.sum(jnp.where(key >= cand, 1.0, 0.0), axis=1, keepdims=True)
            out.append(jnp.where(cnt >= cap, cand, theta))
        return tuple(out)

    zero = jnp.zeros((keys[0].shape[0], 1), I32)
    return lax.fori_loop(0, 31, step, tuple(zero for _ in keys))


def _topk_mask(key, theta, cap, upper, before):
    n = key.shape[1]
    gt = key > theta
    eq = key == theta
    need = cap - jnp.sum(jnp.where(gt, 1.0, 0.0), axis=1, keepdims=True)
    rank = _cumsum_lanes(jnp.where(eq, 1.0, 0.0).astype(BF16), upper, before[0:n], n)
    take = eq & (rank <= need)
    return jnp.where(gt | take, 1.0, 0.0)


def _select_kernel(aff_ref, upper_ref, before_ref, ind_ref, slot_ref, w_ref, off_ref, *, S, T, cap_c, cap_l):
    aff = aff_ref[...]
    upper = upper_ref[...]
    before = before_ref[...]
    key_c = lax.bitcast_convert_type(aff[:, 0:S], I32)
    key_l = lax.bitcast_convert_type(aff[:, S:S + T], I32)
    theta_c, theta_l = _kth_largest_keys((key_c, key_l), (cap_c, cap_l))
    sel = jnp.concatenate([_topk_mask(key_c, theta_c, cap_c, upper, before),
                           _topk_mask(key_l, theta_l, cap_l, upper, before)], axis=1)
    sel16 = sel.astype(BF16)
    pos = _cumsum_lanes(sel16, upper, before, S + T) - sel
    chosen = sel > 0.5
    slot_ref[0] = jnp.where(chosen, pos, -1.0).astype(I32)
    w_ref[0] = jnp.where(chosen, aff, 0.0)
    off_ref[0] = _dot(sel16, ind_ref[...]).astype(I32)


def _select(AFF, *, B, S, T, tm, cap_c, cap_l):
    E = AFF.shape[0]
    P = S + T
    il = np.arange(LANES)
    ip = np.arange(P)
    upper = jnp.asarray(il[:, None] <= il[None, :], BF16)
    before = jnp.asarray(ip[:, None] < (il * LANES)[None, :], BF16)
    ind = jnp.asarray(ip[:, None] < (il * tm)[None, :], BF16)
    return pl.pallas_call(
        functools.partial(_select_kernel, S=S, T=T, cap_c=cap_c, cap_l=cap_l),
        grid=(B,),
        in_specs=[pl.BlockSpec((E, P), lambda b: (0, b)),
                  pl.BlockSpec((LANES, LANES), lambda b: (0, 0)),
                  pl.BlockSpec((P, LANES), lambda b: (0, 0)),
                  pl.BlockSpec((P, LANES), lambda b: (0, 0))],
        out_specs=[pl.BlockSpec((1, E, P), lambda b: (b, 0, 0)),
                   pl.BlockSpec((1, E, P), lambda b: (b, 0, 0)),
                   pl.BlockSpec((1, E, LANES), lambda b: (b, 0, 0))],
        out_shape=[jax.ShapeDtypeStruct((B, E, P), I32), jax.ShapeDtypeStruct((B, E, P), F32),
                   jax.ShapeDtypeStruct((B, E, LANES), I32)],
        compiler_params=_cparams(1),
        name="ec_select",
    )(AFF, upper, before, ind)


EC_WIN = 64
SLOT_ALIGN = 16


def _win_base(lo, capt):
    return pl.multiple_of(jnp.minimum(lo // SLOT_ALIGN * SLOT_ALIGN, capt - EC_WIN), SLOT_ALIGN)


def _n_windows(tm):
    return -(-(tm + SLOT_ALIGN - 1) // EC_WIN)


def _any_overflow(offs, bases, E):
    over = None
    for e in range(E):
        c = offs(e)[1] > bases[e] + EC_WIN
        over = c if over is None else (over | c)
    return over


def _gather_kernel(off_ref, x_ref, wc_ref, slot_ref, xg_ref, wg_ref, *, E, NT, tm, capt):
    b = pl.program_id(0)
    j = pl.program_id(1)
    W = EC_WIN

    @pl.when(j == 0)
    def _():
        xg_ref[...] = jnp.zeros_like(xg_ref)
        wg_ref[...] = jnp.zeros_like(wg_ref)

    def offs(e):
        i = (b * E + e) * (NT + 1) + j
        return off_ref[i], off_ref[i + 1]

    rows = lax.broadcasted_iota(I32, (W, tm), 0)
    x = x_ref[...]
    wc = wc_ref[...]
    bases = [_win_base(offs(e)[0], capt) for e in range(E)]
    lhs = jnp.concatenate(
        [jnp.where(rows == slot_ref[0, e:e + 1, :] - bases[e], 1.0, 0.0).astype(BF16) for e in range(E)], axis=0)
    rx = _dot(lhs, x).astype(BF16)
    rw = _dot(lhs, wc)
    for e in range(E):
        xg_ref[e, pl.ds(bases[e], W), :] += rx[e * W:(e + 1) * W]
        wg_ref[e, pl.ds(bases[e], W), :] += rw[e * W:(e + 1) * W]

    def extra(e, carry):
        lo, hi = offs(e)
        base = _win_base(lo, capt)
        for w in range(1, _n_windows(tm)):
            first = base + w * W

            @pl.when(hi > first)
            def _():
                start = pl.multiple_of(jnp.minimum(first, capt - W), SLOT_ALIGN)
                s = slot_ref[0, pl.ds(e, 1), :]
                onehot = jnp.where(rows == jnp.where(s >= first, s - start, -1), 1.0, 0.0).astype(BF16)
                xg_ref[e, pl.ds(start, W), :] += _dot(onehot, x).astype(BF16)
                wg_ref[e, pl.ds(start, W), :] += _dot(onehot, wc)
        return carry

    @pl.when(_any_overflow(offs, bases, E))
    def _():
        lax.fori_loop(0, E, extra, 0)


def _gather(off, XN, WC, slot, *, B, E, S, T, tm, capt):
    N, D = XN.shape
    P = S + T
    NT = P // tm
    row = lambda b, j, off: (b * NT + j, 0)
    return pl.pallas_call(
        functools.partial(_gather_kernel, E=E, NT=NT, tm=tm, capt=capt),
        grid_spec=pltpu.PrefetchScalarGridSpec(
            num_scalar_prefetch=1,
            grid=(B, NT),
            in_specs=[pl.BlockSpec((tm, D), row),
                      pl.BlockSpec((tm, LANES), row),
                      pl.BlockSpec((1, E, tm), lambda b, j, off: (b, 0, j))],
            out_specs=[pl.BlockSpec((E, capt, D), lambda b, j, off: (0, b, 0)),
                       pl.BlockSpec((E, capt, LANES), lambda b, j, off: (0, b, 0))]),
        out_shape=[jax.ShapeDtypeStruct((E, B * capt, D), BF16),
                   jax.ShapeDtypeStruct((E, B * capt, LANES), F32)],
        compiler_params=_cparams(2),
        name="ec_gather",
    )(off, XN, WC, slot)


def _ffn_kernel(x_ref, val_ref, wg_ref, wu_ref, wd_ref, o_ref, acc_ref, *, E):
    f = pl.program_id(2)

    def partial_down():
        x = x_ref[0]
        g = _dot(x, wg_ref[0, 0].astype(BF16))
        u = _dot(x, wu_ref[0, 0].astype(BF16))
        h = (_silu(g) * u).astype(BF16)
        return _dot(h, wd_ref[0, 0].astype(BF16))

    @pl.when(f == 0)
    def _():
        acc_ref[...] = partial_down()

    @pl.when(f > 0)
    def _():
        acc_ref[...] += partial_down()

    @pl.when(f == pl.num_programs(2) - 1)
    def _():
        lane = lax.broadcasted_iota(I32, (1, LANES), 1)
        mine = (lane % E == pl.program_id(0)) & (lane < 3 * E)
        val = jnp.sum(jnp.where(mine, val_ref[0], 0.0), axis=1, keepdims=True)
        o_ref[0] = (acc_ref[...] * val).astype(BF16)


def _ffn(XG, WG, w_gate, w_up, w_down, l, *, rows, fc):
    E, R, D = XG.shape
    F = w_gate.shape[-1]
    return pl.pallas_call(
        functools.partial(_ffn_kernel, E=E),
        grid=(E, R // rows, F // fc),
        in_specs=[pl.BlockSpec((1, rows, D), lambda e, r, f: (e, r, 0)),
                  pl.BlockSpec((1, rows, LANES), lambda e, r, f: (e, r, 0)),
                  pl.BlockSpec((1, 1, D, fc), lambda e, r, f: (l, e, 0, f)),
                  pl.BlockSpec((1, 1, D, fc), lambda e, r, f: (l, e, 0, f)),
                  pl.BlockSpec((1, 1, fc, D), lambda e, r, f: (l, e, f, 0))],
        out_specs=pl.BlockSpec((1, rows, D), lambda e, r, f: (e, r, 0)),
        out_shape=jax.ShapeDtypeStruct((E, R, D), BF16),
        scratch_shapes=[pltpu.VMEM((rows, D), F32)],
        compiler_params=_cparams(3),
        name="ec_ffn",
    )(XG, WG, w_gate, w_up, w_down)


def _combine_kernel(off_ref, y_ref, shl_ref, spread_ref, slot_ref, x_ref, g2_ref, fg_ref, o_ref, acc_ref,
                    *, E, NT, j0, tm, capt, final):
    b = pl.program_id(0)
    j = pl.program_id(1) + j0
    W = EC_WIN

    def offs(e):
        i = (b * E + e) * (NT + 1) + j
        return off_ref[i], off_ref[i + 1]

    bases = [_win_base(offs(e)[0], capt) for e in range(E)]
    col = lax.broadcasted_iota(I32, (1, E * W), 1)
    target = (col % W).astype(F32)
    for e in range(E):
        target = target + jnp.where(col // W == e, bases[e].astype(F32), 0.0)
    spread = _dot(shl_ref[...], spread_ref[...])
    lhs = jnp.where(spread == target, 1.0, 0.0).astype(BF16)
    rhs = jnp.concatenate([y_ref[e, pl.ds(bases[e], W), :] for e in range(E)], axis=0)
    acc_ref[...] = _dot(lhs, rhs)

    rows = lax.broadcasted_iota(I32, (W, tm), 0)

    def extra(e, carry):
        lo, hi = offs(e)
        base = _win_base(lo, capt)
        for w in range(1, _n_windows(tm)):
            first = base + w * W

            @pl.when(hi > first)
            def _():
                start = pl.multiple_of(jnp.minimum(first, capt - W), SLOT_ALIGN)
                s = slot_ref[0, pl.ds(e, 1), :]
                onehot_t = jnp.where(rows == jnp.where(s >= first, s - start, -1), 1.0, 0.0)
                onehot = jnp.transpose(onehot_t).astype(BF16)
                acc_ref[...] += _dot(onehot, y_ref[e, pl.ds(start, W), :])
        return carry

    @pl.when(_any_overflow(offs, bases, E))
    def _():
        lax.fori_loop(0, E, extra, 0)
    x2 = x_ref[...] + g2_ref[0] * acc_ref[...]
    if final:
        x2 = x2 * lax.rsqrt(jnp.mean(x2 * x2, axis=-1, keepdims=True) + NORM_EPS) * fg_ref[...]
    if final:
        o_ref[0] = x2
    else:
        o_ref[...] = x2


def _combine(off, Y, slotT, slot, X1, g2, fg, *, B, E, S, T, tm, capt, final):
    N, D = X1.shape
    P = S + T
    NT = P // tm
    ns = S // tm
    j0 = ns if final else 0
    shl = jnp.concatenate([slotT >> 4, slotT & 15], axis=1).astype(BF16)
    own = np.kron(np.eye(E, dtype=np.float32), np.ones((1, EC_WIN), np.float32))
    spread = jnp.asarray(np.concatenate([16.0 * own, own], axis=0), BF16)

    def row(b, j, off):
        return (b * NT + j + j0, 0)

    if final:
        out_spec = pl.BlockSpec((1, tm, D), lambda b, j, off: (b, j, 0))
        out_shape = jax.ShapeDtypeStruct((B, T, D), F32)
    else:
        out_spec = pl.BlockSpec((tm, D), row)
        out_shape = jax.ShapeDtypeStruct((N, D), F32)
    return pl.pallas_call(
        functools.partial(_combine_kernel, E=E, NT=NT, j0=j0, tm=tm, capt=capt, final=final),
        grid_spec=pltpu.PrefetchScalarGridSpec(
            num_scalar_prefetch=1,
            grid=(B, NT - j0),
            in_specs=[_resident((E, capt, D), lambda b, j, off: (0, b, 0)),
                      pl.BlockSpec((tm, 2 * E), row),
                      pl.BlockSpec((2 * E, E * EC_WIN), lambda b, j, off: (0, 0)),
                      pl.BlockSpec((1, E, tm), lambda b, j, off: (b, 0, j + j0)),
                      pl.BlockSpec((tm, D), row),
                      pl.BlockSpec((1, 1, D), lambda b, j, off: (jnp.where(j + j0 < ns, B, b), 0, 0)),
                      pl.BlockSpec((1, D), lambda b, j, off: (0, 0))],
            out_specs=out_spec,
            scratch_shapes=[pltpu.VMEM((tm, D), F32)]),
        out_shape=out_shape,
        compiler_params=_cparams(2),
        name="ec_combine",
    )(off, Y, shl, spread, slot, X1, g2, fg)


def _permute_w_in(w):
    D = w.shape[0]
    o = np.cumsum([0, 256, 256, 256, 128, 128, 256, 256, 16, 16, 256, 128, 128, 1024, 1024, 1024])
    qa, ka, va, qb, kb, vb, rb, lf, lb, qc, kc, vc, mga, mgb, mgc = [w[:, o[i]:o[i + 1]] for i in range(15)]
    grp = GQA_HEADS // GQA_KV_HEADS

    def rep(t):
        return jnp.concatenate([t[:, (h // grp) * HEAD_DIM:(h // grp + 1) * HEAD_DIM] for h in range(GQA_HEADS)], axis=1)

    pad = jnp.zeros((D, COL_END - COL_LR - 2 * GLA_LOWRANK), w.dtype)
    gates = 0.5 * jnp.concatenate([mga, mgb, mgc], axis=1)
    return jnp.concatenate([qa, ka, va, qb, kb, vb, rb, qc, rep(kc), rep(vc), gates, lf, lb, pad],
                           axis=1).astype(BF16)


def _prep_w_kernel(w_ref, o_ref):
    o_ref[...] = _permute_w_in(w_ref[0])


def _prep_w(w_in, l):
    L, D, C = w_in.shape
    tr = 256
    return pl.pallas_call(
        _prep_w_kernel,
        grid=(D // tr,),
        in_specs=[pl.BlockSpec((1, tr, C), lambda i: (l, i, 0))],
        out_specs=pl.BlockSpec((tr, COL_END), lambda i: (i, 0)),
        out_shape=jax.ShapeDtypeStruct((D, COL_END), BF16),
        compiler_params=_cparams(1),
        name="prep_w_in",
    )(w_in)


def _rope_tables(T, S):
    t = np.arange(T)
    row = (t // GRID_W).astype(np.float32)
    col = (t % GRID_W).astype(np.float32)
    lane = np.arange(HEAD_DIM)
    is_col = lane >= HEAD_DIM // 2
    fi = lane % 16
    inv_freq = ROPE_BASE ** (-jnp.arange(16, dtype=F32) / 16)
    pos = jnp.where(jnp.asarray(is_col)[None, :], jnp.asarray(col)[:, None], jnp.asarray(row)[:, None])
    ang = pos * inv_freq[jnp.asarray(fi)][None, :]
    cos = jnp.cos(ang)
    sin = jnp.sin(ang)
    sin = jnp.where(jnp.asarray((lane % 32) < 16)[None, :], -sin, sin)
    cos = jnp.concatenate([jnp.ones((S, HW), F32), jnp.tile(cos, (1, HW // HEAD_DIM))], axis=0)
    sin = jnp.concatenate([jnp.zeros((S, HW), F32), jnp.tile(sin, (1, HW // HEAD_DIM))], axis=0)
    return cos, sin


def kernel(x, c, ctx, c_ctx, w_ada, b_ada, norm1_g, norm2_g, w_in, na_rpb, gla_wg_f, gla_bg_f, gla_wg_b,
           gla_bg_b, gla_norm_g, gqa_qn_g, gqa_kn_g, w_branch_a, w_branch_b, w_branch_c, w_out, w_router,
           w_e_gate, w_e_up, w_e_down, final_norm_g):
    B, T, D = x.shape
    S = ctx.shape[1]
    L = w_ada.shape[0]
    E = w_router.shape[-1]
    P = S + T
    tm = min(256, S)
    NT = P // tm
    assert S % tm == 0 and T % tm == 0 and S % GRID_W == 0 and T % GRID_W == 0 and P % S == 0
    cap_l = EC_CAPACITY_FACTOR * T // E
    cap_ctx = EC_CAPACITY_FACTOR * S // E
    assert cap_l % SLOT_ALIGN == 0 and cap_ctx % SLOT_ALIGN == 0 and cap_l >= EC_WIN
    assert 3 * E <= LANES and E % (LANES // EC_WIN) == 0
    R = T // GRID_W
    kh = min(NA_KH, R)

    seg = jnp.asarray(np.kron(np.eye(HW // HEAD_DIM), np.ones((HEAD_DIM, HEAD_DIM))), BF16)
    cos_t, sin_t = _rope_tables(T, S)
    ii = np.arange(GLA_CHUNK)
    tril = (ii[:, None] >= ii[None, :]).astype(np.float32)
    tri4 = jnp.asarray(np.stack([np.tile(tril, (1, GLA_HEADS)), np.tile(tril.T, (1, GLA_HEADS))]))

    RP = -(-(B + 1) // 8) * 8
    cv = jnp.concatenate([c, c_ctx[None], jnp.zeros((RP - B - 1, D), F32)], axis=0)
    mod = _adaln(cv, w_ada, b_ada)[:, :B + 1].reshape(L, B + 1, 6, 1, D)

    X = jnp.concatenate([ctx, x], axis=1).reshape(B * P, D)
    out = None
    for l in range(L):
        last = l == L - 1
        sh1, sc1, g1, sh2, sc2, g2 = [mod[l, :, i] for i in range(6)]
        w_p = _prep_w(w_in, l)
        qn = jnp.tile(gqa_qn_g[l], HW // HEAD_DIM)[None]
        kn = jnp.tile(gqa_kn_g[l], HW // HEAD_DIM)[None]
        A, Bq, C, G, LR = _proj_in(X, norm1_g[l][None], sc1, sh1, w_p, seg, qn, kn, cos_t, sin_t,
                                   B=B, S=S, T=T)

        YA = _na_attention(A, _na_bias_table(na_rpb[l], kh), B=B, S=S, T=T)
        YC = _gqa_attention(C, B=B, S=S, T=T, tq=min(256, S))
        YA, YC = _ctx_attention(A, C, YA, YC, B=B, S=S, T=T)

        zpad = jnp.zeros((LANES - 2 * GLA_LOWRANK, GLA_DK), F32)
        zlr = jnp.zeros((GLA_LOWRANK, GLA_DK), F32)
        wg_pad = jnp.stack([jnp.concatenate([gla_wg_f[l], zlr, zpad], axis=0),
                            jnp.concatenate([zlr, gla_wg_b[l], zpad], axis=0)]).astype(BF16)
        bg = jnp.stack([gla_bg_f[l], gla_bg_b[l]])[:, None, :]
        O = _gla(Bq, LR, wg_pad, bg, tri4, B=B, S=S, T=T)

        gg = jnp.tile(gla_norm_g[l], GLA_HEADS)[None]
        X1, XN, AFFT = _merge(YA, O, Bq, YC, G, X,
                              w_branch_a[l].astype(BF16), w_branch_b[l].astype(BF16),
                              w_branch_c[l].astype(BF16), w_out[l].astype(BF16),
                              seg, gg, g1, norm2_g[l][None], sc2, sh2, w_router[l],
                              B=B, S=S, T=T)

        cap_c = 0 if last else cap_ctx
        capt = cap_l + cap_c
        slot, wsel, off = _select(jnp.transpose(AFFT), B=B, S=S, T=T, tm=tm, cap_c=cap_c, cap_l=cap_l)
        off_flat = off[:, :, :NT + 1].reshape(-1)
        slotT = jnp.swapaxes(slot, 1, 2).reshape(B * P, E)
        wT = jnp.swapaxes(wsel, 1, 2).reshape(B * P, E)
        WC = jnp.concatenate(list(_split3(wT)) + [jnp.zeros((B * P, LANES - 3 * E), BF16)], axis=1)
        XG, WG = _gather(off_flat, XN, WC, slot, B=B, E=E, S=S, T=T, tm=tm, capt=capt)
        rows = capt * (2 if B % 2 == 0 else 1)
        Y = _ffn(XG, WG, w_e_gate, w_e_up, w_e_down, l, rows=rows, fc=min(512, w_e_gate.shape[-1]))
        res = _combine(off_flat, Y, slotT, slot, X1, g2, final_norm_g[None],
                       B=B, E=E, S=S, T=T, tm=tm, capt=capt, final=last)
        if last:
            out = res
        else:
            X = res
    return out
```

```python
import functools

import jax
import jax.numpy as jnp
import numpy as np
from jax import lax
from jax.experimental import pallas as pl
from jax.experimental.pallas import tpu as pltpu

F32 = jnp.float32
BF16 = jnp.bfloat16
I32 = jnp.int32

GRID_W = 64
HEAD_DIM = 64
NORM_EPS = 1e-6
NA_HEADS = 4
NA_KH = 8
NA_KW = 16
GLA_HEADS = 4
GLA_DK = 128
GLA_DV = 256
GLA_LOWRANK = 16
GLA_TAU = 16.0
GLA_CHUNK = 64
GQA_HEADS = 4
GQA_KV_HEADS = 2
ROPE_BASE = 10000.0
N_EXPERTS = 16
EC_CAPACITY_FACTOR = 2
NEG_BIG = -1e30

LANES = 128
HW = 256
VMEM_LIMIT = 56 * 1024 * 1024

COL_A = 0
COL_B = 768
COL_C = 1536
COL_G = 2304
COL_LR = 5376
COL_END = 5504


def _cparams(n_grid):
    return pltpu.CompilerParams(dimension_semantics=("arbitrary",) * n_grid,
                                vmem_limit_bytes=VMEM_LIMIT)


def _resident(block_shape, index_map):
    return pl.BlockSpec(block_shape, index_map, pipeline_mode=pl.Buffered(1))


def _dot(a, b):
    return jnp.dot(a, b, preferred_element_type=F32)


def _dot_nt(a, b):
    return lax.dot_general(a, b, (((1,), (1,)), ((), ())), preferred_element_type=F32)


def _split2(a):
    hi = a.astype(BF16)
    lo = (a - hi.astype(F32)).astype(BF16)
    return hi, lo


def _split3(a):
    h1 = a.astype(BF16)
    r1 = a - h1.astype(F32)
    h2 = r1.astype(BF16)
    h3 = (r1 - h2.astype(F32)).astype(BF16)
    return h1, h2, h3


def _seg_meansq(x, seg_ref):
    hi, lo = _split2(x * x)
    seg = seg_ref[...]
    return (_dot(hi, seg) + _dot(lo, seg)) * (1.0 / HEAD_DIM)


def _sigmoid(x):
    return 0.5 * jnp.tanh(0.5 * x) + 0.5


def _silu(x):
    return x * _sigmoid(x)


def _adaln_kernel(cv_ref, w_ref, b_ref, o_ref):
    s = _silu(cv_ref[...])
    w = w_ref[0]
    s1, s2, s3 = _split3(s)
    w1, w2, w3 = _split3(w)
    acc = _dot(s1, w1) + (_dot(s1, w2) + _dot(s2, w1)) + (_dot(s2, w2) + _dot(s1, w3) + _dot(s3, w1))
    o_ref[0] = acc + b_ref[0]


def _adaln(cv, w_ada, b_ada):
    L, D, D6 = w_ada.shape
    R = cv.shape[0]
    tn = 1024
    return pl.pallas_call(
        _adaln_kernel,
        grid=(L, D6 // tn),
        in_specs=[pl.BlockSpec((R, D), lambda l, n: (0, 0)),
                  pl.BlockSpec((1, D, tn), lambda l, n: (l, 0, n)),
                  pl.BlockSpec((1, 1, tn), lambda l, n: (l, 0, n))],
        out_specs=pl.BlockSpec((1, R, tn), lambda l, n: (l, 0, n)),
        out_shape=jax.ShapeDtypeStruct((L, R, D6), F32),
        compiler_params=_cparams(2),
        name="adaln",
    )(cv, w_ada, b_ada.reshape(L, 1, D6))


def _swap16(y):
    lane = lax.broadcasted_iota(I32, y.shape, 1)
    first = (lane % 32) < 16
    return jnp.where(first, pltpu.roll(y, LANES - 16, 1), pltpu.roll(y, 16, 1))


def _norm_rope(x, seg_ref, g_ref, cos_ref, sin_ref, scale):
    y = x * lax.rsqrt(_seg_meansq(x, seg_ref) + NORM_EPS) * g_ref[...]
    halves = []
    for i in range(HW // LANES):
        sl = slice(i * LANES, (i + 1) * LANES)
        yh = y[:, sl]
        halves.append(yh * cos_ref[:, sl] + _swap16(yh) * sin_ref[:, sl])
    out = jnp.concatenate(halves, axis=1)
    return out * scale if scale != 1.0 else out


def _proj_in_kernel(x_ref, g_ref, sc_ref, sh_ref, scc_ref, shc_ref, w_ref, seg_ref, qn_ref, kn_ref,
                    cos_ref, sin_ref, a_ref, b_ref, c_ref, gt_ref, lr_ref, *, S):
    tr = x_ref.shape[0]
    first = pl.program_id(1) == 0
    parts = [(slice(0, S), jnp.where(first, scc_ref[0], sc_ref[0]), jnp.where(first, shc_ref[0], sh_ref[0]))]
    if tr > S:
        parts.append((slice(S, tr), sc_ref[0], sh_ref[0]))
    qscale = HEAD_DIM ** -0.5
    for r, sc, sh in parts:
        x = x_ref[r, :]
        y = x * lax.rsqrt(jnp.mean(x * x, axis=-1, keepdims=True) + NORM_EPS)
        u = (y * (g_ref[...] * (1.0 + sc)) + sh).astype(BF16)

        def proj(c0, c1):
            return _dot(u, w_ref[:, c0:c1])

        def rope(t, gain_ref, scale):
            return _norm_rope(t, seg_ref, gain_ref, cos_ref[r, :], sin_ref[r, :], scale)

        a_ref[r, 0:HW] = (proj(COL_A, COL_A + HW) * qscale).astype(BF16)
        a_ref[r, HW:3 * HW] = proj(COL_A + HW, COL_B).astype(BF16)
        b_ref[r, :] = proj(COL_B, COL_C).astype(BF16)
        c_ref[r, 0:HW] = rope(proj(COL_C, COL_C + HW), qn_ref, qscale).astype(BF16)
        c_ref[r, HW:2 * HW] = rope(proj(COL_C + HW, COL_C + 2 * HW), kn_ref, 1.0).astype(BF16)
        c_ref[r, 2 * HW:3 * HW] = proj(COL_C + 2 * HW, COL_G).astype(BF16)
        for i in range(3):
            gt_ref[r, i * 1024:(i + 1) * 1024] = proj(COL_G + i * 1024, COL_G + (i + 1) * 1024).astype(BF16)
        lr_ref[r, :] = proj(COL_LR, COL_END).astype(BF16)


def _row_tile(S, P):
    for k in (8, 4, 2, 1):
        if P % k == 0 and (P // k) % 16 == 0 and P // k >= S:
            return P // k
    raise ValueError("no row tile for these shapes")


def _proj_in(X, g, sc, sh, w_p, seg, qn, kn, cos_t, sin_t, *, B, S, T):
    N, D = X.shape
    P = S + T
    tm = _row_tile(S, P)
    NT = P // tm

    def row(b, j):
        return (b * NT + j, 0)

    grp = lambda b, j: (b, 0, 0)
    grp_c = lambda b, j: (B, 0, 0)
    tab = lambda b, j: (j, 0)
    const = lambda b, j: (0, 0)
    outs = [jax.ShapeDtypeStruct((N, 768), BF16), jax.ShapeDtypeStruct((N, 768), BF16),
            jax.ShapeDtypeStruct((N, 768), BF16), jax.ShapeDtypeStruct((N, 3072), BF16),
            jax.ShapeDtypeStruct((N, LANES), BF16)]
    return pl.pallas_call(
        functools.partial(_proj_in_kernel, S=S),
        grid=(B, NT),
        in_specs=[pl.BlockSpec((tm, D), row),
                  pl.BlockSpec((1, D), const),
                  pl.BlockSpec((1, 1, D), grp),
                  pl.BlockSpec((1, 1, D), grp),
                  pl.BlockSpec((1, 1, D), grp_c),
                  pl.BlockSpec((1, 1, D), grp_c),
                  _resident((D, COL_END), const),
                  pl.BlockSpec((HW, HW), const),
                  pl.BlockSpec((1, HW), const),
                  pl.BlockSpec((1, HW), const),
                  pl.BlockSpec((tm, HW), tab),
                  pl.BlockSpec((tm, HW), tab)],
        out_specs=[pl.BlockSpec((tm, 768), row), pl.BlockSpec((tm, 768), row),
                   pl.BlockSpec((tm, 768), row), pl.BlockSpec((tm, 3072), row),
                   pl.BlockSpec((tm, LANES), row)],
        out_shape=outs,
        compiler_params=_cparams(2),
        name="proj_in",
    )(X, g, sc, sh, sc, sh, w_p, seg, qn, kn, cos_t, sin_t)


def _mh_attend(q, segments):
    nq = q.shape[0]
    nh = HW // HEAD_DIM
    lane = lax.broadcasted_iota(I32, (1, HW), 1) // HEAD_DIM
    qs = jnp.concatenate([jnp.where(lane == h, q, jnp.zeros_like(q)) for h in range(nh)], axis=0)
    mx = den = o = None
    for k, v, bias in segments:
        s = _dot_nt(qs, k)
        if bias is not None:
            s = s + bias
        smax = s.max(axis=-1, keepdims=True)
        if mx is None:
            mx = smax
            p = jnp.exp(s - mx)
            den = p.sum(axis=-1, keepdims=True)
            o = _dot(p.astype(BF16), v)
        else:
            mx_new = jnp.maximum(mx, smax)
            alpha = jnp.exp(mx - mx_new)
            p = jnp.exp(s - mx_new)
            den = den * alpha + p.sum(axis=-1, keepdims=True)
            o = o * alpha + _dot(p.astype(BF16), v)
            mx = mx_new
    o = o / den
    acc = jnp.zeros(q.shape, F32)
    for h in range(nh):
        acc = acc + jnp.where(lane == h, o[h * nq:(h + 1) * nq], 0.0)
    return acc


def _na_kernel(q_ref, k_ref, v_ref, bias_ref, o_ref, *, S, R, kh, rb):
    for i in range(rb):
        r = pl.program_id(1) * rb + i
        r0 = jnp.clip(r - kh // 2, 0, R - kh)
        start = pl.multiple_of(S + r0 * GRID_W, GRID_W)
        kwin = k_ref[pl.ds(start, kh * GRID_W), :]
        vwin = v_ref[pl.ds(start, kh * GRID_W), :]
        segs = [(kwin, vwin, bias_ref[r - r0]),
                (k_ref[0:S, :], v_ref[0:S, :], None)]
        rows = slice(i * GRID_W, (i + 1) * GRID_W)
        o_ref[rows, :] = _mh_attend(q_ref[rows, :], segs).astype(BF16)


def _na_bias_table(rpb, kh):
    H = rpb.shape[0]
    W = GRID_W
    col = np.arange(W)
    col_start = np.clip(col - NA_KW // 2, 0, W - NA_KW)
    inwin = (col[None, :] >= col_start[:, None]) & (col[None, :] < col_start[:, None] + NA_KW)
    rp = jnp.pad(rpb.astype(F32), ((0, 0), (0, 0), (W, W)))
    toe = jnp.stack([rp[:, :, W + NA_KW - 1 - q:2 * W + NA_KW - 1 - q] for q in range(W)], axis=2)
    toe = jnp.where(jnp.asarray(inwin)[None, None], toe, NEG_BIG)
    t = jnp.stack([toe[:, NA_KH - 1 - c:NA_KH - 1 - c + kh] for c in range(kh)], axis=0)
    t = jnp.transpose(t, (0, 1, 3, 2, 4))
    return t.reshape(kh, H * W, kh * W)


def _na_attention(A, bias_t, *, B, S, T):
    N = A.shape[0]
    P = S + T
    R = T // GRID_W
    kh = min(NA_KH, R)
    rb = 4 if (R % 4 == 0 and S % (4 * GRID_W) == 0) else 1
    tq = rb * GRID_W
    qoff = S // tq
    per_b = P // tq

    return pl.pallas_call(
        functools.partial(_na_kernel, S=S, R=R, kh=kh, rb=rb),
        grid=(B, R // rb),
        in_specs=[pl.BlockSpec((tq, HW), lambda b, r: (b * per_b + qoff + r, 0)),
                  pl.BlockSpec((P, HW), lambda b, r: (b, 1)),
                  pl.BlockSpec((P, HW), lambda b, r: (b, 2)),
                  pl.BlockSpec((kh, NA_HEADS * GRID_W, kh * GRID_W), lambda b, r: (0, 0, 0))],
        out_specs=pl.BlockSpec((tq, HW), lambda b, r: (b * per_b + qoff + r, 0)),
        out_shape=jax.ShapeDtypeStruct((N, HW), BF16),
        compiler_params=_cparams(2),
        name="na_attention",
    )(A, A, A, bias_t)


def _key_chunks(n, target=512, align=256):
    nchunks = max(1, n // target)
    base = (n // nchunks) // align * align
    if base == 0:
        return [(0, n)]
    bounds = [i * base for i in range(nchunks)] + [n]
    return [(bounds[i], bounds[i + 1]) for i in range(nchunks)]


def _gqa_kernel(q_ref, k_ref, v_ref, o_ref):
    q = q_ref[...]
    nq = q.shape[0]
    nh = HW // HEAD_DIM
    lane = lax.broadcasted_iota(I32, (1, HW), 1) // HEAD_DIM
    odd = (lane % 2) == 1
    qs = jnp.concatenate([jnp.where(lane == h, q, jnp.zeros_like(q)) for h in range(nh)], axis=0)
    mx = o = None
    for c0, c1 in _key_chunks(k_ref.shape[0]):
        v = v_ref[c0:c1, :]
        v1 = jnp.where(odd, jnp.ones_like(v), v)
        s = _dot_nt(qs, k_ref[c0:c1, :])
        smax = s.max(axis=-1, keepdims=True)
        if mx is None:
            mx = smax
            o = _dot(jnp.exp(s - mx).astype(BF16), v1)
        else:
            mx_new = jnp.maximum(mx, smax)
            o = o * jnp.exp(mx - mx_new) + _dot(jnp.exp(s - mx_new).astype(BF16), v1)
            mx = mx_new
    o = o / o[:, HEAD_DIM:HEAD_DIM + 1]
    acc = jnp.zeros(q.shape, F32)
    for h in range(nh):
        blk = o[h * nq:(h + 1) * nq]
        if h % 2 == 1:
            blk = jnp.concatenate([pltpu.roll(blk[:, i:i + LANES], HEAD_DIM, 1) for i in range(0, HW, LANES)], axis=1)
        acc = acc + jnp.where(lane == h, blk, 0.0)
    o_ref[...] = acc.astype(BF16)


def _gqa_attention(C, *, B, S, T, tq):
    N = C.shape[0]
    P = S + T
    nq = T // tq
    qoff = S // tq
    per_b = P // tq
    return pl.pallas_call(
        _gqa_kernel,
        grid=(B, nq),
        in_specs=[pl.BlockSpec((tq, HW), lambda b, i: (b * per_b + qoff + i, 0)),
                  pl.BlockSpec((P, HW), lambda b, i: (b, 1)),
                  pl.BlockSpec((P, HW), lambda b, i: (b, 2))],
        out_specs=pl.BlockSpec((tq, HW), lambda b, i: (b * per_b + qoff + i, 0)),
        out_shape=jax.ShapeDtypeStruct((N, HW), BF16),
        compiler_params=_cparams(2),
        name="gqa_attention",
    )(C, C, C)


def _ctx_attn_kernel(qa_ref, ka_ref, va_ref, qc_ref, kc_ref, vc_ref, ya_in, yc_in, ya_ref, yc_ref):
    del ya_in, yc_in
    ya_ref[...] = _mh_attend(qa_ref[...], [(ka_ref[...], va_ref[...], None)]).astype(BF16)
    yc_ref[...] = _mh_attend(qc_ref[...], [(kc_ref[...], vc_ref[...], None)]).astype(BF16)


def _ctx_attention(A, C, YA, YC, *, B, S, T):
    per_b = (S + T) // S
    blk = lambda c: pl.BlockSpec((S, HW), lambda b: (b * per_b, c))
    anyspec = pl.BlockSpec(memory_space=pl.ANY)
    return pl.pallas_call(
        _ctx_attn_kernel,
        grid=(B,),
        in_specs=[blk(0), blk(1), blk(2), blk(0), blk(1), blk(2), anyspec, anyspec],
        out_specs=[blk(0), blk(0)],
        out_shape=[jax.ShapeDtypeStruct(YA.shape, YA.dtype), jax.ShapeDtypeStruct(YC.shape, YC.dtype)],
        input_output_aliases={6: 0, 7: 1},
        compiler_params=_cparams(1),
        name="ctx_attention",
    )(A, A, A, C, C, C, YA, YC)


def _gla_kernel(bq_ref, lr_ref, wg_ref, bg_ref, tblk_ref, oblk_ref, tri4_ref, o_ref,
                qd_scr, ki_scr, ke_scr, dec_scr, *, S, P, gr):
    C = GLA_CHUNK
    ncs = S // C
    nc = P // C
    dk_h = GLA_DK // GLA_HEADS
    dv_h = GLA_DV // GLA_HEADS

    def thirds(a):
        return a[:, 0:GLA_DK] + a[:, GLA_DK:2 * GLA_DK] + a[:, 2 * GLA_DK:3 * GLA_DK]

    def decays(gi, carry):
        rows = pl.ds(pl.multiple_of(gi * gr, gr), gr)
        q = bq_ref[rows, 0:GLA_DK].astype(F32)
        k = bq_ref[rows, GLA_DK:2 * GLA_DK].astype(F32)
        lr = lr_ref[rows, :]
        for d in range(2):
            z = _dot(lr, wg_ref[d]) + bg_ref[d]
            g = (jnp.minimum(z, 0.0) - jnp.log(1.0 + jnp.exp(-jnp.abs(z)))) * (1.0 / GLA_TAU)
            g3 = jnp.concatenate(_split3(g), axis=1)
            bcum = thirds(_dot(tblk_ref[d], g3))
            btot = thirds(_dot(oblk_ref[...], g3))
            qd_scr[d, rows, :] = (q * (dk_h ** -0.5) * jnp.exp(bcum)).astype(BF16)
            ki_scr[d, rows, :] = (k * jnp.exp(-bcum)).astype(BF16)
            ke_scr[d, rows, :] = (k * jnp.exp(btot - bcum)).astype(BF16)
            dec_scr[d, rows, :] = jnp.exp(btot)
        return carry

    lax.fori_loop(0, P // gr, decays, 0, unroll=2)

    lane_k = lax.broadcasted_iota(I32, (1, GLA_DK), 1) // dk_h
    lane_v = lax.broadcasted_iota(I32, (1, GLA_DV), 1) // dv_h
    bd = (lax.broadcasted_iota(I32, (GLA_DV, GLA_DK), 0) // dv_h
          == lax.broadcasted_iota(I32, (GLA_DV, GLA_DK), 1) // dk_h)

    def chunk(d, ci, st):
        rows = pl.ds(pl.multiple_of(ci * C, C), C)
        q_dec = qd_scr[d, rows, :]
        k_inv = ki_scr[d, rows, :]
        v = bq_ref[rows, 2 * GLA_DK:2 * GLA_DK + GLA_DV]
        k_stack = jnp.concatenate(
            [jnp.where(lane_k == h, k_inv, jnp.zeros_like(k_inv)) for h in range(GLA_HEADS)], axis=0)
        a_cat = (_dot_nt(q_dec, k_stack) * tri4_ref[d]).astype(BF16)
        v_stack = jnp.concatenate(
            [jnp.where(lane_v == h, v, jnp.zeros_like(v)) for h in range(GLA_HEADS)], axis=0)
        lhs = jnp.concatenate([a_cat, q_dec], axis=1)
        rhs = jnp.concatenate([v_stack, jnp.transpose(st).astype(BF16)], axis=0)
        o_ref[d, rows, :] = _dot(lhs, rhs)
        v_t = jnp.transpose(v.astype(F32)).astype(BF16)
        ds_t = _dot(v_t, ke_scr[d, rows, :])
        return st * dec_scr[d, pl.ds(pl.multiple_of(ci * C, C), 1), :] + jnp.where(bd, ds_t, 0.0)

    def body(n, sts):
        cb = jnp.where(n < ncs, ncs - 1 - n, nc - 1 - (n - ncs))
        return chunk(0, n, sts[0]), chunk(1, cb, sts[1])

    zero = jnp.zeros((GLA_DV, GLA_DK), F32)
    lax.fori_loop(0, nc, body, (zero, zero), unroll=8)


def _gla(Bq, LR, wg_pad, bg, tri4, *, B, S, T):
    N = Bq.shape[0]
    P = S + T
    gr = 256 if P % 256 == 0 else 2 * GLA_CHUNK
    per = gr // GLA_CHUNK
    ii = np.arange(GLA_CHUNK)
    tril = (ii[:, None] >= ii[None, :]).astype(np.float32)
    eye = np.eye(per, dtype=np.float32)
    tblk = jnp.asarray(np.stack([np.kron(eye, tril), np.kron(eye, tril.T)]), BF16)
    oblk = jnp.asarray(np.kron(eye, np.ones((GLA_CHUNK, GLA_CHUNK), np.float32)), BF16)
    full = lambda shape: pl.BlockSpec(shape, lambda b: (0,) * len(shape))
    return pl.pallas_call(
        functools.partial(_gla_kernel, S=S, P=P, gr=gr),
        grid=(B,),
        in_specs=[pl.BlockSpec((P, 768), lambda b: (b, 0)),
                  pl.BlockSpec((P, LANES), lambda b: (b, 0)),
                  full((2, LANES, GLA_DK)), full((2, 1, GLA_DK)),
                  full((2, gr, gr)), full((gr, gr)),
                  full((2, GLA_CHUNK, GLA_HEADS * GLA_CHUNK))],
        out_specs=pl.BlockSpec((2, P, GLA_DV), lambda b: (0, b, 0)),
        out_shape=jax.ShapeDtypeStruct((2, N, GLA_DV), F32),
        scratch_shapes=[pltpu.VMEM((2, P, GLA_DK), BF16), pltpu.VMEM((2, P, GLA_DK), BF16),
                        pltpu.VMEM((2, P, GLA_DK), BF16), pltpu.VMEM((2, P, GLA_DK), F32)],
        compiler_params=_cparams(1),
        name="gla",
    )(Bq, LR, wg_pad, bg, tblk, oblk, tri4)


def _merge_kernel(ya_ref, of_ref, ob_ref, rb_ref, yc_ref, gt_ref, x_ref,
                  wa_ref, wb_ref, wc_ref, wo_ref, seg_ref, gg_ref, n2_ref, wr_ref,
                  g1_ref, sc2_ref, sh2_ref, g1c_ref, sc2c_ref, sh2c_ref,
                  x1_ref, xn_ref, aff_ref, *, S):
    tr = x_ref.shape[0]
    first = pl.program_id(1) == 0
    head = (jnp.where(first, g1c_ref[0], g1_ref[0]), jnp.where(first, sc2c_ref[0], sc2_ref[0]),
            jnp.where(first, sh2c_ref[0], sh2_ref[0]))
    tail = (g1_ref[0], sc2_ref[0], sh2_ref[0])

    def halves(r0, r1):
        mid = (r0 + r1) // 2
        return [(r0, mid), (mid, r1)] if (r1 - r0) % 32 == 0 and r1 - r0 >= 64 else [(r0, r1)]

    parts = [(slice(0, S),) + head]
    if tr > S:
        parts += [(slice(a, b),) + tail for a, b in halves(S, tr)]

    wh, wl = _split2(wr_ref[...])
    for r, g1, sc2, sh2 in parts:
        o = of_ref[0, r, :] + ob_ref[0, r, :]
        yb = o * lax.rsqrt(_seg_meansq(o, seg_ref) + NORM_EPS) * gg_ref[...]
        yb = (yb * _silu(rb_ref[r, :].astype(F32))).astype(BF16)

        def gated(i, d):
            return jnp.tanh(gt_ref[r, i * 1024:(i + 1) * 1024].astype(F32)) * d + d

        m2 = (gated(0, _dot(ya_ref[r, :], wa_ref[...]))
              + gated(1, _dot(yb, wb_ref[...]))
              + gated(2, _dot(yc_ref[r, :], wc_ref[...])))
        x1 = x_ref[r, :] + (0.5 * g1) * _dot(m2.astype(BF16), wo_ref[...])
        x1_ref[r, :] = x1
        y = x1 * lax.rsqrt(jnp.mean(x1 * x1, axis=-1, keepdims=True) + NORM_EPS)
        xn = y * (n2_ref[...] * (1.0 + sc2)) + sh2
        xn_ref[r, :] = xn.astype(BF16)
        xh, xl = _split2(xn)
        logits = _dot(xh, wh) + (_dot(xl, wh) + _dot(xh, wl))
        e = jnp.exp(logits - logits.max(axis=-1, keepdims=True))
        aff_ref[r, :] = e / e.sum(axis=-1, keepdims=True)


def _merge(YA, O, Bq, YC, G, X, wa, wb, wc, wo, seg, gg, g1, n2, sc2, sh2, wr, *, B, S, T):
    N, D = X.shape
    P = S + T
    tm = _row_tile(S, P)
    NT = P // tm
    E = wr.shape[1]

    def row(b, j):
        return (b * NT + j, 0)

    grp = lambda b, j: (b, 0, 0)
    grp_c = lambda b, j: (B, 0, 0)
    const = lambda b, j: (0, 0)
    mod = lambda m: pl.BlockSpec((1, 1, D), m)
    return pl.pallas_call(
        functools.partial(_merge_kernel, S=S),
        grid=(B, NT),
        in_specs=[pl.BlockSpec((tm, HW), row),
                  pl.BlockSpec((1, tm, HW), lambda b, j: (0, b * NT + j, 0)),
                  pl.BlockSpec((1, tm, HW), lambda b, j: (1, b * NT + j, 0)),
                  pl.BlockSpec((tm, HW), lambda b, j: (b * NT + j, 2)),
                  pl.BlockSpec((tm, HW), row),
                  pl.BlockSpec((tm, 3072), row),
                  pl.BlockSpec((tm, D), row),
                  pl.BlockSpec((HW, D), const), pl.BlockSpec((HW, D), const), pl.BlockSpec((HW, D), const),
                  pl.BlockSpec((D, D), const),
                  pl.BlockSpec((HW, HW), const),
                  pl.BlockSpec((1, HW), const),
                  pl.BlockSpec((1, D), const),
                  pl.BlockSpec((D, E), const),
                  mod(grp), mod(grp), mod(grp), mod(grp_c), mod(grp_c), mod(grp_c)],
        out_specs=[pl.BlockSpec((tm, D), row), pl.BlockSpec((tm, D), row), pl.BlockSpec((tm, E), row)],
        out_shape=[jax.ShapeDtypeStruct((N, D), F32), jax.ShapeDtypeStruct((N, D), BF16),
                   jax.ShapeDtypeStruct((N, E), F32)],
        compiler_params=_cparams(2),
        name="merge",
    )(YA, O, O, Bq, YC, G, X, wa, wb, wc, wo, seg, gg, n2, wr, g1, sc2, sh2, g1, sc2, sh2)


def _cumsum_lanes(m01, upper, before, n):
    ahead = _dot(m01, before)
    return jnp.concatenate(
        [_dot(m01[:, c * LANES:(c + 1) * LANES], upper) + ahead[:, c:c + 1] for c in range(n // LANES)], axis=1)


def _kth_largest_keys(keys, caps):
    def step(i, thetas):
        bit = jnp.left_shift(jnp.int32(1), 30 - i)
        out = []
        for key, cap, theta in zip(keys, caps, thetas):
            cand = theta | bit
            cnt = jnp.sum(jnp.where(key >= cand, 1.0, 0.0), axis=1, keepdims=True)
            out.append(jnp.where(cnt >= cap, cand, theta))
        return tuple(out)

    zero = jnp.zeros((keys[0].shape[0], 1), I32)
    return lax.fori_loop(0, 31, step, tuple(zero for _ in keys))


def _topk_mask(key, theta, cap, upper, before):
    n = key.shape[1]
    gt = key > theta
    eq = key == theta
    need = cap - jnp.sum(jnp.where(gt, 1.0, 0.0), axis=1, keepdims=True)
    rank = _cumsum_lanes(jnp.where(eq, 1.0, 0.0).astype(BF16), upper, before[0:n], n)
    take = eq & (rank <= need)
    return jnp.where(gt | take, 1.0, 0.0)


ROUTE_HI, ROUTE_LO, ROUTE_W = 0, 1, 2


def _select_kernel(aff_ref, upper_ref, before_ref, ind_ref, slot_ref, tok_ref, off_ref, *, S, T, cap_c, cap_l):
    aff = aff_ref[...]
    upper = upper_ref[...]
    before = before_ref[...]
    key_c = lax.bitcast_convert_type(aff[:, 0:S], I32)
    key_l = lax.bitcast_convert_type(aff[:, S:S + T], I32)
    theta_c, theta_l = _kth_largest_keys((key_c, key_l), (cap_c, cap_l))
    sel = jnp.concatenate([_topk_mask(key_c, theta_c, cap_c, upper, before),
                           _topk_mask(key_l, theta_l, cap_l, upper, before)], axis=1)
    sel16 = sel.astype(BF16)
    pos = _cumsum_lanes(sel16, upper, before, S + T) - sel
    chosen = sel > 0.5
    slot = jnp.where(chosen, pos, -1.0).astype(I32)
    slot_ref[0] = slot
    off_ref[0] = _dot(sel16, ind_ref[...]).astype(I32)
    E = aff.shape[0]
    w1, w2, w3 = _split3(jnp.where(chosen, aff, 0.0))
    rows = [(slot >> 4).astype(F32), (slot & 15).astype(F32), w1.astype(F32), w2.astype(F32), w3.astype(F32),
            jnp.zeros((LANES - 5 * E, S + T), F32)]
    tok_ref[...] = jnp.transpose(jnp.concatenate(rows, axis=0)).astype(BF16)


def _select(AFF, *, B, S, T, tm, cap_c, cap_l):
    E = AFF.shape[0]
    P = S + T
    il = np.arange(LANES)
    ip = np.arange(P)
    upper = jnp.asarray(il[:, None] <= il[None, :], BF16)
    before = jnp.asarray(ip[:, None] < (il * LANES)[None, :], BF16)
    ind = jnp.asarray(ip[:, None] < (il * tm)[None, :], BF16)
    return pl.pallas_call(
        functools.partial(_select_kernel, S=S, T=T, cap_c=cap_c, cap_l=cap_l),
        grid=(B,),
        in_specs=[pl.BlockSpec((E, P), lambda b: (0, b)),
                  pl.BlockSpec((LANES, LANES), lambda b: (0, 0)),
                  pl.BlockSpec((P, LANES), lambda b: (0, 0)),
                  pl.BlockSpec((P, LANES), lambda b: (0, 0))],
        out_specs=[pl.BlockSpec((1, E, P), lambda b: (b, 0, 0)),
                   pl.BlockSpec((P, LANES), lambda b: (b, 0)),
                   pl.BlockSpec((1, E, LANES), lambda b: (b, 0, 0))],
        out_shape=[jax.ShapeDtypeStruct((B, E, P), I32), jax.ShapeDtypeStruct((B * P, LANES), BF16),
                   jax.ShapeDtypeStruct((B, E, LANES), I32)],
        compiler_params=_cparams(1),
        name="ec_select",
    )(AFF, upper, before, ind)


EC_WIN = 64
SLOT_ALIGN = 16


def _win_base(lo, capt):
    return pl.multiple_of(jnp.minimum(lo // SLOT_ALIGN * SLOT_ALIGN, capt - EC_WIN), SLOT_ALIGN)


def _n_windows(tm):
    return -(-(tm + SLOT_ALIGN - 1) // EC_WIN)


def _any_overflow(offs, bases, E):
    over = None
    for e in range(E):
        c = offs(e)[1] > bases[e] + EC_WIN
        over = c if over is None else (over | c)
    return over


def _gather_kernel(off_ref, x_ref, wc_ref, slot_ref, xg_ref, wg_ref, *, E, NT, tm, capt):
    b = pl.program_id(0)
    j = pl.program_id(1)
    W = EC_WIN

    @pl.when(j == 0)
    def _():
        xg_ref[...] = jnp.zeros_like(xg_ref)
        wg_ref[...] = jnp.zeros_like(wg_ref)

    def offs(e):
        i = (b * E + e) * (NT + 1) + j
        return off_ref[i], off_ref[i + 1]

    rows = lax.broadcasted_iota(I32, (W, tm), 0)
    x = x_ref[...]
    wc = wc_ref[...]
    bases = [_win_base(offs(e)[0], capt) for e in range(E)]
    lhs = jnp.concatenate(
        [jnp.where(rows == slot_ref[0, e:e + 1, :] - bases[e], 1.0, 0.0).astype(BF16) for e in range(E)], axis=0)
    rx = _dot(lhs, x).astype(BF16)
    rw = _dot(lhs, wc)
    for e in range(E):
        xg_ref[e, pl.ds(bases[e], W), :] += rx[e * W:(e + 1) * W]
        wg_ref[e, pl.ds(bases[e], W), :] += rw[e * W:(e + 1) * W]

    def extra(e, carry):
        lo, hi = offs(e)
        base = _win_base(lo, capt)
        for w in range(1, _n_windows(tm)):
            first = base + w * W

            @pl.when(hi > first)
            def _():
                start = pl.multiple_of(jnp.minimum(first, capt - W), SLOT_ALIGN)
                s = slot_ref[0, pl.ds(e, 1), :]
                onehot = jnp.where(rows == jnp.where(s >= first, s - start, -1), 1.0, 0.0).astype(BF16)
                xg_ref[e, pl.ds(start, W), :] += _dot(onehot, x).astype(BF16)
                wg_ref[e, pl.ds(start, W), :] += _dot(onehot, wc)
        return carry

    @pl.when(_any_overflow(offs, bases, E))
    def _():
        lax.fori_loop(0, E, extra, 0)


def _gather(off, XN, WC, slot, *, B, E, S, T, tm, capt):
    N, D = XN.shape
    P = S + T
    NT = P // tm
    row = lambda b, j, off: (b * NT + j, 0)
    return pl.pallas_call(
        functools.partial(_gather_kernel, E=E, NT=NT, tm=tm, capt=capt),
        grid_spec=pltpu.PrefetchScalarGridSpec(
            num_scalar_prefetch=1,
            grid=(B, NT),
            in_specs=[pl.BlockSpec((tm, D), row),
                      pl.BlockSpec((tm, LANES), row),
                      pl.BlockSpec((1, E, tm), lambda b, j, off: (b, 0, j))],
            out_specs=[pl.BlockSpec((E, capt, D), lambda b, j, off: (0, b, 0)),
                       pl.BlockSpec((E, capt, LANES), lambda b, j, off: (0, b, 0))]),
        out_shape=[jax.ShapeDtypeStruct((E, B * capt, D), BF16),
                   jax.ShapeDtypeStruct((E, B * capt, LANES), F32)],
        compiler_params=_cparams(2),
        name="ec_gather",
    )(off, XN, WC, slot)


def _ffn_kernel(x_ref, val_ref, wg_ref, wu_ref, wd_ref, o_ref, acc_ref, *, E):
    f = pl.program_id(2)

    def partial_down():
        x = x_ref[0]
        g = _dot(x, wg_ref[0, 0].astype(BF16))
        u = _dot(x, wu_ref[0, 0].astype(BF16))
        h = (_silu(g) * u).astype(BF16)
        return _dot(h, wd_ref[0, 0].astype(BF16))

    @pl.when(f == 0)
    def _():
        acc_ref[...] = partial_down()

    @pl.when(f > 0)
    def _():
        acc_ref[...] += partial_down()

    @pl.when(f == pl.num_programs(2) - 1)
    def _():
        lane = lax.broadcasted_iota(I32, (1, LANES), 1)
        mine = (lane % E == pl.program_id(0)) & (lane >= ROUTE_W * E) & (lane < (ROUTE_W + 3) * E)
        val = jnp.sum(jnp.where(mine, val_ref[0], 0.0), axis=1, keepdims=True)
        o_ref[0] = (acc_ref[...] * val).astype(BF16)


def _ffn(XG, WG, w_gate, w_up, w_down, l, *, rows, fc):
    E, R, D = XG.shape
    F = w_gate.shape[-1]
    return pl.pallas_call(
        functools.partial(_ffn_kernel, E=E),
        grid=(E, R // rows, F // fc),
        in_specs=[pl.BlockSpec((1, rows, D), lambda e, r, f: (e, r, 0)),
                  pl.BlockSpec((1, rows, LANES), lambda e, r, f: (e, r, 0)),
                  pl.BlockSpec((1, 1, D, fc), lambda e, r, f: (l, e, 0, f)),
                  pl.BlockSpec((1, 1, D, fc), lambda e, r, f: (l, e, 0, f)),
                  pl.BlockSpec((1, 1, fc, D), lambda e, r, f: (l, e, f, 0))],
        out_specs=pl.BlockSpec((1, rows, D), lambda e, r, f: (e, r, 0)),
        out_shape=jax.ShapeDtypeStruct((E, R, D), BF16),
        scratch_shapes=[pltpu.VMEM((rows, D), F32)],
        compiler_params=_cparams(3),
        name="ec_ffn",
    )(XG, WG, w_gate, w_up, w_down)


def _combine_kernel(off_ref, y_ref, tok_ref, spread_ref, slot_ref, x_ref, g2_ref, fg_ref, o_ref, acc_ref,
                    *, E, NT, j0, tm, capt, final):
    b = pl.program_id(0)
    j = pl.program_id(1) + j0
    W = EC_WIN

    def offs(e):
        i = (b * E + e) * (NT + 1) + j
        return off_ref[i], off_ref[i + 1]

    bases = [_win_base(offs(e)[0], capt) for e in range(E)]
    col = lax.broadcasted_iota(I32, (1, E * W), 1)
    target = (col % W).astype(F32)
    for e in range(E):
        target = target + jnp.where(col // W == e, bases[e].astype(F32), 0.0)
    spread = _dot(tok_ref[...], spread_ref[...])
    lhs = jnp.where(spread == target, 1.0, 0.0).astype(BF16)
    rhs = jnp.concatenate([y_ref[e, pl.ds(bases[e], W), :] for e in range(E)], axis=0)
    acc_ref[...] = _dot(lhs, rhs)

    rows = lax.broadcasted_iota(I32, (W, tm), 0)

    def extra(e, carry):
        lo, hi = offs(e)
        base = _win_base(lo, capt)
        for w in range(1, _n_windows(tm)):
            first = base + w * W

            @pl.when(hi > first)
            def _():
                start = pl.multiple_of(jnp.minimum(first, capt - W), SLOT_ALIGN)
                s = slot_ref[0, pl.ds(e, 1), :]
                onehot_t = jnp.where(rows == jnp.where(s >= first, s - start, -1), 1.0, 0.0)
                onehot = jnp.transpose(onehot_t).astype(BF16)
                acc_ref[...] += _dot(onehot, y_ref[e, pl.ds(start, W), :])
        return carry

    @pl.when(_any_overflow(offs, bases, E))
    def _():
        lax.fori_loop(0, E, extra, 0)
    x2 = x_ref[...] + g2_ref[0] * acc_ref[...]
    if final:
        x2 = x2 * lax.rsqrt(jnp.mean(x2 * x2, axis=-1, keepdims=True) + NORM_EPS) * fg_ref[...]
    if final:
        o_ref[0] = x2
    else:
        o_ref[...] = x2


def _combine(off, Y, TOK, slot, X1, g2, fg, *, B, E, S, T, tm, capt, final):
    N, D = X1.shape
    P = S + T
    NT = P // tm
    ns = S // tm
    j0 = ns if final else 0
    own = np.kron(np.eye(E, dtype=np.float32), np.ones((1, EC_WIN), np.float32))
    spread = np.zeros((LANES, E * EC_WIN), np.float32)
    spread[ROUTE_HI * E:(ROUTE_HI + 1) * E] = 16.0 * own
    spread[ROUTE_LO * E:(ROUTE_LO + 1) * E] = own
    spread = jnp.asarray(spread, BF16)

    def row(b, j, off):
        return (b * NT + j + j0, 0)

    if final:
        out_spec = pl.BlockSpec((1, tm, D), lambda b, j, off: (b, j, 0))
        out_shape = jax.ShapeDtypeStruct((B, T, D), F32)
    else:
        out_spec = pl.BlockSpec((tm, D), row)
        out_shape = jax.ShapeDtypeStruct((N, D), F32)
    return pl.pallas_call(
        functools.partial(_combine_kernel, E=E, NT=NT, j0=j0, tm=tm, capt=capt, final=final),
        grid_spec=pltpu.PrefetchScalarGridSpec(
            num_scalar_prefetch=1,
            grid=(B, NT - j0),
            in_specs=[_resident((E, capt, D), lambda b, j, off: (0, b, 0)),
                      pl.BlockSpec((tm, LANES), row),
                      pl.BlockSpec((LANES, E * EC_WIN), lambda b, j, off: (0, 0)),
                      pl.BlockSpec((1, E, tm), lambda b, j, off: (b, 0, j + j0)),
                      pl.BlockSpec((tm, D), row),
                      pl.BlockSpec((1, 1, D), lambda b, j, off: (jnp.where(j + j0 < ns, B, b), 0, 0)),
                      pl.BlockSpec((1, D), lambda b, j, off: (0, 0))],
            out_specs=out_spec,
            scratch_shapes=[pltpu.VMEM((tm, D), F32)]),
        out_shape=out_shape,
        compiler_params=_cparams(2),
        name="ec_combine",
    )(off, Y, TOK, spread, slot, X1, g2, fg)


def _permute_w_in(w):
    D = w.shape[0]
    o = np.cumsum([0, 256, 256, 256, 128, 128, 256, 256, 16, 16, 256, 128, 128, 1024, 1024, 1024])
    qa, ka, va, qb, kb, vb, rb, lf, lb, qc, kc, vc, mga, mgb, mgc = [w[:, o[i]:o[i + 1]] for i in range(15)]
    grp = GQA_HEADS // GQA_KV_HEADS

    def rep(t):
        return jnp.concatenate([t[:, (h // grp) * HEAD_DIM:(h // grp + 1) * HEAD_DIM] for h in range(GQA_HEADS)], axis=1)

    pad = jnp.zeros((D, COL_END - COL_LR - 2 * GLA_LOWRANK), w.dtype)
    gates = 0.5 * jnp.concatenate([mga, mgb, mgc], axis=1)
    return jnp.concatenate([qa, ka, va, qb, kb, vb, rb, qc, rep(kc), rep(vc), gates, lf, lb, pad],
                           axis=1).astype(BF16)


def _prep_w_kernel(w_ref, o_ref):
    o_ref[...] = _permute_w_in(w_ref[0])


def _prep_w(w_in, l):
    L, D, C = w_in.shape
    tr = 256
    return pl.pallas_call(
        _prep_w_kernel,
        grid=(D // tr,),
        in_specs=[pl.BlockSpec((1, tr, C), lambda i: (l, i, 0))],
        out_specs=pl.BlockSpec((tr, COL_END), lambda i: (i, 0)),
        out_shape=jax.ShapeDtypeStruct((D, COL_END), BF16),
        compiler_params=_cparams(1),
        name="prep_w_in",
    )(w_in)


def _rope_tables(T, S):
    t = np.arange(T)
    row = (t // GRID_W).astype(np.float32)
    col = (t % GRID_W).astype(np.float32)
    lane = np.arange(HEAD_DIM)
    is_col = lane >= HEAD_DIM // 2
    fi = lane % 16
    inv_freq = ROPE_BASE ** (-jnp.arange(16, dtype=F32) / 16)
    pos = jnp.where(jnp.asarray(is_col)[None, :], jnp.asarray(col)[:, None], jnp.asarray(row)[:, None])
    ang = pos * inv_freq[jnp.asarray(fi)][None, :]
    cos = jnp.cos(ang)
    sin = jnp.sin(ang)
    sin = jnp.where(jnp.asarray((lane % 32) < 16)[None, :], -sin, sin)
    cos = jnp.concatenate([jnp.ones((S, HW), F32), jnp.tile(cos, (1, HW // HEAD_DIM))], axis=0)
    sin = jnp.concatenate([jnp.zeros((S, HW), F32), jnp.tile(sin, (1, HW // HEAD_DIM))], axis=0)
    return cos, sin


def kernel(x, c, ctx, c_ctx, w_ada, b_ada, norm1_g, norm2_g, w_in, na_rpb, gla_wg_f, gla_bg_f, gla_wg_b,
           gla_bg_b, gla_norm_g, gqa_qn_g, gqa_kn_g, w_branch_a, w_branch_b, w_branch_c, w_out, w_router,
           w_e_gate, w_e_up, w_e_down, final_norm_g):
    B, T, D = x.shape
    S = ctx.shape[1]
    L = w_ada.shape[0]
    E = w_router.shape[-1]
    P = S + T
    tm = min(256, S)
    NT = P // tm
    assert S % tm == 0 and T % tm == 0 and S % GRID_W == 0 and T % GRID_W == 0 and P % S == 0
    cap_l = EC_CAPACITY_FACTOR * T // E
    cap_ctx = EC_CAPACITY_FACTOR * S // E
    assert cap_l % SLOT_ALIGN == 0 and cap_ctx % SLOT_ALIGN == 0 and cap_l >= EC_WIN
    assert 5 * E <= LANES
    R = T // GRID_W
    kh = min(NA_KH, R)

    seg = jnp.asarray(np.kron(np.eye(HW // HEAD_DIM), np.ones((HEAD_DIM, HEAD_DIM))), BF16)
    cos_t, sin_t = _rope_tables(T, S)
    ii = np.arange(GLA_CHUNK)
    tril = (ii[:, None] >= ii[None, :]).astype(np.float32)
    tri4 = jnp.asarray(np.stack([np.tile(tril, (1, GLA_HEADS)), np.tile(tril.T, (1, GLA_HEADS))]))

    RP = -(-(B + 1) // 8) * 8
    cv = jnp.concatenate([c, c_ctx[None], jnp.zeros((RP - B - 1, D), F32)], axis=0)
    mod = _adaln(cv, w_ada, b_ada)[:, :B + 1].reshape(L, B + 1, 6, 1, D)

    X = jnp.concatenate([ctx, x], axis=1).reshape(B * P, D)
    out = None
    for l in range(L):
        last = l == L - 1
        sh1, sc1, g1, sh2, sc2, g2 = [mod[l, :, i] for i in range(6)]
        w_p = _prep_w(w_in, l)
        qn = jnp.tile(gqa_qn_g[l], HW // HEAD_DIM)[None]
        kn = jnp.tile(gqa_kn_g[l], HW // HEAD_DIM)[None]
        A, Bq, C, G, LR = _proj_in(X, norm1_g[l][None], sc1, sh1, w_p, seg, qn, kn, cos_t, sin_t,
                                   B=B, S=S, T=T)

        YA = _na_attention(A, _na_bias_table(na_rpb[l], kh), B=B, S=S, T=T)
        YC = _gqa_attention(C, B=B, S=S, T=T, tq=min(256, S))
        YA, YC = _ctx_attention(A, C, YA, YC, B=B, S=S, T=T)

        zpad = jnp.zeros((LANES - 2 * GLA_LOWRANK, GLA_DK), F32)
        zlr = jnp.zeros((GLA_LOWRANK, GLA_DK), F32)
        wg_pad = jnp.stack([jnp.concatenate([gla_wg_f[l], zlr, zpad], axis=0),
                            jnp.concatenate([zlr, gla_wg_b[l], zpad], axis=0)]).astype(BF16)
        bg = jnp.stack([gla_bg_f[l], gla_bg_b[l]])[:, None, :]
        O = _gla(Bq, LR, wg_pad, bg, tri4, B=B, S=S, T=T)

        gg = jnp.tile(gla_norm_g[l], GLA_HEADS)[None]
        X1, XN, AFFT = _merge(YA, O, Bq, YC, G, X,
                              w_branch_a[l].astype(BF16), w_branch_b[l].astype(BF16),
                              w_branch_c[l].astype(BF16), w_out[l].astype(BF16),
                              seg, gg, g1, norm2_g[l][None], sc2, sh2, w_router[l],
                              B=B, S=S, T=T)

        cap_c = 0 if last else cap_ctx
        capt = cap_l + cap_c
        slot, TOK, off = _select(jnp.transpose(AFFT), B=B, S=S, T=T, tm=tm, cap_c=cap_c, cap_l=cap_l)
        off_flat = off[:, :, :NT + 1].reshape(-1)
        XG, WG = _gather(off_flat, XN, TOK, slot, B=B, E=E, S=S, T=T, tm=tm, capt=capt)
        rows = capt * (2 if B % 2 == 0 else 1)
        Y = _ffn(XG, WG, w_e_gate, w_e_up, w_e_down, l, rows=rows, fc=min(512, w_e_gate.shape[-1]))
        res = _combine(off_flat, Y, TOK, slot, X1, g2, final_norm_g[None],
                       B=B, E=E, S=S, T=T, tm=tm, capt=capt, final=last)
        if last:
            out = res
        else:
            X = res
    return out
```

```python
import functools

import jax
import jax.numpy as jnp
import numpy as np
from jax import lax
from jax.experimental import pallas as pl
from jax.experimental.pallas import tpu as pltpu

F32 = jnp.float32
BF16 = jnp.bfloat16
I32 = jnp.int32

GRID_W = 64
HEAD_DIM = 64
NORM_EPS = 1e-6
NA_HEADS = 4
NA_KH = 8
NA_KW = 16
GLA_HEADS = 4
GLA_DK = 128
GLA_DV = 256
GLA_LOWRANK = 16
GLA_TAU = 16.0
GLA_CHUNK = 64
GQA_HEADS = 4
GQA_KV_HEADS = 2
ROPE_BASE = 10000.0
N_EXPERTS = 16
EC_CAPACITY_FACTOR = 2
NEG_BIG = -1e30

LANES = 128
HW = 256
VMEM_LIMIT = 56 * 1024 * 1024

COL_A = 0
COL_B = 768
COL_C = 1536
COL_G = 2304
COL_LR = 5376
COL_END = 5504


def _cparams(n_grid):
    return pltpu.CompilerParams(dimension_semantics=("arbitrary",) * n_grid,
                                vmem_limit_bytes=VMEM_LIMIT)


def _resident(block_shape, index_map):
    return pl.BlockSpec(block_shape, index_map, pipeline_mode=pl.Buffered(1))


def _dot(a, b):
    return jnp.dot(a, b, preferred_element_type=F32)


def _dot_nt(a, b):
    return lax.dot_general(a, b, (((1,), (1,)), ((), ())), preferred_element_type=F32)


def _split2(a):
    hi = a.astype(BF16)
    lo = (a - hi.astype(F32)).astype(BF16)
    return hi, lo


def _split3(a):
    h1 = a.astype(BF16)
    r1 = a - h1.astype(F32)
    h2 = r1.astype(BF16)
    h3 = (r1 - h2.astype(F32)).astype(BF16)
    return h1, h2, h3


def _seg_meansq(x, seg_ref):
    hi, lo = _split2(x * x)
    seg = seg_ref[...]
    return (_dot(hi, seg) + _dot(lo, seg)) * (1.0 / HEAD_DIM)


def _sigmoid(x):
    return 0.5 * jnp.tanh(0.5 * x) + 0.5


def _silu(x):
    return x * _sigmoid(x)


def _adaln_kernel(cv_ref, w_ref, b_ref, o_ref):
    s = _silu(cv_ref[...])
    w = w_ref[0]
    s1, s2, s3 = _split3(s)
    w1, w2, w3 = _split3(w)
    acc = _dot(s1, w1) + (_dot(s1, w2) + _dot(s2, w1)) + (_dot(s2, w2) + _dot(s1, w3) + _dot(s3, w1))
    o_ref[0] = acc + b_ref[0]


def _adaln(cv, w_ada, b_ada):
    L, D, D6 = w_ada.shape
    R = cv.shape[0]
    tn = 1024
    return pl.pallas_call(
        _adaln_kernel,
        grid=(L, D6 // tn),
        in_specs=[pl.BlockSpec((R, D), lambda l, n: (0, 0)),
                  pl.BlockSpec((1, D, tn), lambda l, n: (l, 0, n)),
                  pl.BlockSpec((1, 1, tn), lambda l, n: (l, 0, n))],
        out_specs=pl.BlockSpec((1, R, tn), lambda l, n: (l, 0, n)),
        out_shape=jax.ShapeDtypeStruct((L, R, D6), F32),
        compiler_params=_cparams(2),
        name="adaln",
    )(cv, w_ada, b_ada.reshape(L, 1, D6))


def _swap16(y):
    lane = lax.broadcasted_iota(I32, y.shape, 1)
    first = (lane % 32) < 16
    return jnp.where(first, pltpu.roll(y, LANES - 16, 1), pltpu.roll(y, 16, 1))


def _norm_rope(x, seg_ref, g_ref, cos_ref, sin_ref, scale):
    y = x * lax.rsqrt(_seg_meansq(x, seg_ref) + NORM_EPS) * g_ref[...]
    halves = []
    for i in range(HW // LANES):
        sl = slice(i * LANES, (i + 1) * LANES)
        yh = y[:, sl]
        halves.append(yh * cos_ref[:, sl] + _swap16(yh) * sin_ref[:, sl])
    out = jnp.concatenate(halves, axis=1)
    return out * scale if scale != 1.0 else out


def _proj_in_kernel(x_ref, g_ref, sc_ref, sh_ref, scc_ref, shc_ref, w_ref, seg_ref, qn_ref, kn_ref,
                    cos_ref, sin_ref, a_ref, b_ref, c_ref, gt_ref, lr_ref, *, S):
    tr = x_ref.shape[0]
    first = pl.program_id(1) == 0
    parts = [(slice(0, S), jnp.where(first, scc_ref[0], sc_ref[0]), jnp.where(first, shc_ref[0], sh_ref[0]))]
    if tr > S:
        parts.append((slice(S, tr), sc_ref[0], sh_ref[0]))
    qscale = HEAD_DIM ** -0.5
    for r, sc, sh in parts:
        x = x_ref[r, :]
        y = x * lax.rsqrt(jnp.mean(x * x, axis=-1, keepdims=True) + NORM_EPS)
        u = (y * (g_ref[...] * (1.0 + sc)) + sh).astype(BF16)

        def proj(c0, c1):
            return _dot(u, w_ref[:, c0:c1])

        def rope(t, gain_ref, scale):
            return _norm_rope(t, seg_ref, gain_ref, cos_ref[r, :], sin_ref[r, :], scale)

        a_ref[r, 0:HW] = (proj(COL_A, COL_A + HW) * qscale).astype(BF16)
        a_ref[r, HW:3 * HW] = proj(COL_A + HW, COL_B).astype(BF16)
        b_ref[r, :] = proj(COL_B, COL_C).astype(BF16)
        c_ref[r, 0:HW] = rope(proj(COL_C, COL_C + HW), qn_ref, qscale).astype(BF16)
        c_ref[r, HW:2 * HW] = rope(proj(COL_C + HW, COL_C + 2 * HW), kn_ref, 1.0).astype(BF16)
        c_ref[r, 2 * HW:3 * HW] = proj(COL_C + 2 * HW, COL_G).astype(BF16)
        for i in range(3):
            gt_ref[r, i * 1024:(i + 1) * 1024] = proj(COL_G + i * 1024, COL_G + (i + 1) * 1024).astype(BF16)
        lr_ref[r, :] = proj(COL_LR, COL_END).astype(BF16)


def _row_tile(S, P):
    for k in (8, 4, 2, 1):
        if P % k == 0 and (P // k) % 16 == 0 and P // k >= S:
            return P // k
    raise ValueError("no row tile for these shapes")


def _proj_in(X, g, sc, sh, w_p, seg, qn, kn, cos_t, sin_t, *, B, S, T):
    N, D = X.shape
    P = S + T
    tm = _row_tile(S, P)
    if P % (2 * tm) == 0:
        tm = 2 * tm
    NT = P // tm

    def row(b, j):
        return (b * NT + j, 0)

    grp = lambda b, j: (b, 0, 0)
    grp_c = lambda b, j: (B, 0, 0)
    tab = lambda b, j: (j, 0)
    const = lambda b, j: (0, 0)
    outs = [jax.ShapeDtypeStruct((N, 768), BF16), jax.ShapeDtypeStruct((N, 768), BF16),
            jax.ShapeDtypeStruct((N, 768), BF16), jax.ShapeDtypeStruct((N, 3072), BF16),
            jax.ShapeDtypeStruct((N, LANES), BF16)]
    return pl.pallas_call(
        functools.partial(_proj_in_kernel, S=S),
        grid=(B, NT),
        in_specs=[pl.BlockSpec((tm, D), row),
                  pl.BlockSpec((1, D), const),
                  pl.BlockSpec((1, 1, D), grp),
                  pl.BlockSpec((1, 1, D), grp),
                  pl.BlockSpec((1, 1, D), grp_c),
                  pl.BlockSpec((1, 1, D), grp_c),
                  _resident((D, COL_END), const),
                  pl.BlockSpec((HW, HW), const),
                  pl.BlockSpec((1, HW), const),
                  pl.BlockSpec((1, HW), const),
                  pl.BlockSpec((tm, HW), tab),
                  pl.BlockSpec((tm, HW), tab)],
        out_specs=[pl.BlockSpec((tm, 768), row), pl.BlockSpec((tm, 768), row),
                   pl.BlockSpec((tm, 768), row), pl.BlockSpec((tm, 3072), row),
                   pl.BlockSpec((tm, LANES), row)],
        out_shape=outs,
        compiler_params=_cparams(2),
        name="proj_in",
    )(X, g, sc, sh, sc, sh, w_p, seg, qn, kn, cos_t, sin_t)


def _mh_attend(q, segments):
    nq = q.shape[0]
    nh = HW // HEAD_DIM
    lane = lax.broadcasted_iota(I32, (1, HW), 1) // HEAD_DIM
    qs = jnp.concatenate([jnp.where(lane == h, q, jnp.zeros_like(q)) for h in range(nh)], axis=0)
    mx = den = o = None
    for k, v, bias in segments:
        s = _dot_nt(qs, k)
        if bias is not None:
            s = s + bias
        smax = s.max(axis=-1, keepdims=True)
        if mx is None:
            mx = smax
            p = jnp.exp(s - mx)
            den = p.sum(axis=-1, keepdims=True)
            o = _dot(p.astype(BF16), v)
        else:
            mx_new = jnp.maximum(mx, smax)
            alpha = jnp.exp(mx - mx_new)
            p = jnp.exp(s - mx_new)
            den = den * alpha + p.sum(axis=-1, keepdims=True)
            o = o * alpha + _dot(p.astype(BF16), v)
            mx = mx_new
    o = o / den
    acc = jnp.zeros(q.shape, F32)
    for h in range(nh):
        acc = acc + jnp.where(lane == h, o[h * nq:(h + 1) * nq], 0.0)
    return acc


def _na_kernel(q_ref, k_ref, v_ref, bias_ref, o_ref, *, S, R, kh, rb):
    for i in range(rb):
        r = pl.program_id(1) * rb + i
        r0 = jnp.clip(r - kh // 2, 0, R - kh)
        start = pl.multiple_of(S + r0 * GRID_W, GRID_W)
        kwin = k_ref[pl.ds(start, kh * GRID_W), :]
        vwin = v_ref[pl.ds(start, kh * GRID_W), :]
        segs = [(kwin, vwin, bias_ref[r - r0]),
                (k_ref[0:S, :], v_ref[0:S, :], None)]
        rows = slice(i * GRID_W, (i + 1) * GRID_W)
        o_ref[rows, :] = _mh_attend(q_ref[rows, :], segs).astype(BF16)


def _na_bias_table(rpb, kh):
    H = rpb.shape[0]
    W = GRID_W
    col = np.arange(W)
    col_start = np.clip(col - NA_KW // 2, 0, W - NA_KW)
    inwin = (col[None, :] >= col_start[:, None]) & (col[None, :] < col_start[:, None] + NA_KW)
    rp = jnp.pad(rpb.astype(F32), ((0, 0), (0, 0), (W, W)))
    toe = jnp.stack([rp[:, :, W + NA_KW - 1 - q:2 * W + NA_KW - 1 - q] for q in range(W)], axis=2)
    toe = jnp.where(jnp.asarray(inwin)[None, None], toe, NEG_BIG)
    t = jnp.stack([toe[:, NA_KH - 1 - c:NA_KH - 1 - c + kh] for c in range(kh)], axis=0)
    t = jnp.transpose(t, (0, 1, 3, 2, 4))
    return t.reshape(kh, H * W, kh * W)


def _na_attention(A, bias_t, *, B, S, T):
    N = A.shape[0]
    P = S + T
    R = T // GRID_W
    kh = min(NA_KH, R)
    rb = 4 if (R % 4 == 0 and S % (4 * GRID_W) == 0) else 1
    tq = rb * GRID_W
    qoff = S // tq
    per_b = P // tq

    return pl.pallas_call(
        functools.partial(_na_kernel, S=S, R=R, kh=kh, rb=rb),
        grid=(B, R // rb),
        in_specs=[pl.BlockSpec((tq, HW), lambda b, r: (b * per_b + qoff + r, 0)),
                  pl.BlockSpec((P, HW), lambda b, r: (b, 1)),
                  pl.BlockSpec((P, HW), lambda b, r: (b, 2)),
                  pl.BlockSpec((kh, NA_HEADS * GRID_W, kh * GRID_W), lambda b, r: (0, 0, 0))],
        out_specs=pl.BlockSpec((tq, HW), lambda b, r: (b * per_b + qoff + r, 0)),
        out_shape=jax.ShapeDtypeStruct((N, HW), BF16),
        compiler_params=_cparams(2),
        name="na_attention",
    )(A, A, A, bias_t)


def _key_chunks(n, target=512, align=256):
    nchunks = max(1, n // target)
    base = (n // nchunks) // align * align
    if base == 0:
        return [(0, n)]
    bounds = [i * base for i in range(nchunks)] + [n]
    return [(bounds[i], bounds[i + 1]) for i in range(nchunks)]


def _gqa_kernel(q_ref, k_ref, v_ref, o_ref):
    q = q_ref[...]
    nq = q.shape[0]
    nh = HW // HEAD_DIM
    lane = lax.broadcasted_iota(I32, (1, HW), 1) // HEAD_DIM
    odd = (lane % 2) == 1
    qs = jnp.concatenate([jnp.where(lane == h, q, jnp.zeros_like(q)) for h in range(nh)], axis=0)
    mx = o = None
    for c0, c1 in _key_chunks(k_ref.shape[0]):
        v = v_ref[c0:c1, :]
        v1 = jnp.where(odd, jnp.ones_like(v), v)
        s = _dot_nt(qs, k_ref[c0:c1, :])
        smax = s.max(axis=-1, keepdims=True)
        if mx is None:
            mx = smax
            o = _dot(jnp.exp(s - mx).astype(BF16), v1)
        else:
            mx_new = jnp.maximum(mx, smax)
            o = o * jnp.exp(mx - mx_new) + _dot(jnp.exp(s - mx_new).astype(BF16), v1)
            mx = mx_new
    o = o / o[:, HEAD_DIM:HEAD_DIM + 1]
    acc = jnp.zeros(q.shape, F32)
    for h in range(nh):
        blk = o[h * nq:(h + 1) * nq]
        if h % 2 == 1:
            blk = jnp.concatenate([pltpu.roll(blk[:, i:i + LANES], HEAD_DIM, 1) for i in range(0, HW, LANES)], axis=1)
        acc = acc + jnp.where(lane == h, blk, 0.0)
    o_ref[...] = acc.astype(BF16)


def _gqa_attention(C, *, B, S, T, tq):
    N = C.shape[0]
    P = S + T
    nq = T // tq
    qoff = S // tq
    per_b = P // tq
    return pl.pallas_call(
        _gqa_kernel,
        grid=(B, nq),
        in_specs=[pl.BlockSpec((tq, HW), lambda b, i: (b * per_b + qoff + i, 0)),
                  pl.BlockSpec((P, HW), lambda b, i: (b, 1)),
                  pl.BlockSpec((P, HW), lambda b, i: (b, 2))],
        out_specs=pl.BlockSpec((tq, HW), lambda b, i: (b * per_b + qoff + i, 0)),
        out_shape=jax.ShapeDtypeStruct((N, HW), BF16),
        compiler_params=_cparams(2),
        name="gqa_attention",
    )(C, C, C)


def _ctx_attn_kernel(qa_ref, ka_ref, va_ref, qc_ref, kc_ref, vc_ref, ya_in, yc_in, ya_ref, yc_ref):
    del ya_in, yc_in
    ya_ref[...] = _mh_attend(qa_ref[...], [(ka_ref[...], va_ref[...], None)]).astype(BF16)
    yc_ref[...] = _mh_attend(qc_ref[...], [(kc_ref[...], vc_ref[...], None)]).astype(BF16)


def _ctx_attention(A, C, YA, YC, *, B, S, T):
    per_b = (S + T) // S
    blk = lambda c: pl.BlockSpec((S, HW), lambda b: (b * per_b, c))
    anyspec = pl.BlockSpec(memory_space=pl.ANY)
    return pl.pallas_call(
        _ctx_attn_kernel,
        grid=(B,),
        in_specs=[blk(0), blk(1), blk(2), blk(0), blk(1), blk(2), anyspec, anyspec],
        out_specs=[blk(0), blk(0)],
        out_shape=[jax.ShapeDtypeStruct(YA.shape, YA.dtype), jax.ShapeDtypeStruct(YC.shape, YC.dtype)],
        input_output_aliases={6: 0, 7: 1},
        compiler_params=_cparams(1),
        name="ctx_attention",
    )(A, A, A, C, C, C, YA, YC)


def _gla_kernel(bq_ref, lr_ref, wg_ref, bg_ref, tblk_ref, oblk_ref, tri4_ref, o_ref,
                qd_scr, ki_scr, ke_scr, dec_scr, *, S, P, gr):
    C = GLA_CHUNK
    ncs = S // C
    nc = P // C
    dk_h = GLA_DK // GLA_HEADS
    dv_h = GLA_DV // GLA_HEADS

    def thirds(a):
        return a[:, 0:GLA_DK] + a[:, GLA_DK:2 * GLA_DK] + a[:, 2 * GLA_DK:3 * GLA_DK]

    def decays(gi, carry):
        rows = pl.ds(pl.multiple_of(gi * gr, gr), gr)
        q = bq_ref[rows, 0:GLA_DK].astype(F32)
        k = bq_ref[rows, GLA_DK:2 * GLA_DK].astype(F32)
        lr = lr_ref[rows, :]
        for d in range(2):
            z = _dot(lr, wg_ref[d]) + bg_ref[d]
            g = (jnp.minimum(z, 0.0) - jnp.log(1.0 + jnp.exp(-jnp.abs(z)))) * (1.0 / GLA_TAU)
            g3 = jnp.concatenate(_split3(g), axis=1)
            bcum = thirds(_dot(tblk_ref[d], g3))
            btot = thirds(_dot(oblk_ref[...], g3))
            qd_scr[d, rows, :] = (q * (dk_h ** -0.5) * jnp.exp(bcum)).astype(BF16)
            ki_scr[d, rows, :] = (k * jnp.exp(-bcum)).astype(BF16)
            ke_scr[d, rows, :] = (k * jnp.exp(btot - bcum)).astype(BF16)
            dec_scr[d, rows, :] = jnp.exp(btot)
        return carry

    lax.fori_loop(0, P // gr, decays, 0, unroll=2)

    lane_k = lax.broadcasted_iota(I32, (1, GLA_DK), 1) // dk_h
    lane_v = lax.broadcasted_iota(I32, (1, GLA_DV), 1) // dv_h
    bd = (lax.broadcasted_iota(I32, (GLA_DV, GLA_DK), 0) // dv_h
          == lax.broadcasted_iota(I32, (GLA_DV, GLA_DK), 1) // dk_h)

    def chunk(d, ci, st):
        rows = pl.ds(pl.multiple_of(ci * C, C), C)
        q_dec = qd_scr[d, rows, :]
        k_inv = ki_scr[d, rows, :]
        v = bq_ref[rows, 2 * GLA_DK:2 * GLA_DK + GLA_DV]
        k_stack = jnp.concatenate(
            [jnp.where(lane_k == h, k_inv, jnp.zeros_like(k_inv)) for h in range(GLA_HEADS)], axis=0)
        a_cat = (_dot_nt(q_dec, k_stack) * tri4_ref[d]).astype(BF16)
        v_stack = jnp.concatenate(
            [jnp.where(lane_v == h, v, jnp.zeros_like(v)) for h in range(GLA_HEADS)], axis=0)
        lhs = jnp.concatenate([a_cat, q_dec], axis=1)
        rhs = jnp.concatenate([v_stack, jnp.transpose(st).astype(BF16)], axis=0)
        o_ref[d, rows, :] = _dot(lhs, rhs).astype(BF16)
        v_t = jnp.transpose(v.astype(F32)).astype(BF16)
        ds_t = _dot(v_t, ke_scr[d, rows, :])
        return st * dec_scr[d, pl.ds(pl.multiple_of(ci * C, C), 1), :] + jnp.where(bd, ds_t, 0.0)

    def body(n, sts):
        cb = jnp.where(n < ncs, ncs - 1 - n, nc - 1 - (n - ncs))
        return chunk(0, n, sts[0]), chunk(1, cb, sts[1])

    zero = jnp.zeros((GLA_DV, GLA_DK), F32)
    lax.fori_loop(0, nc, body, (zero, zero), unroll=8)


def _gla(Bq, LR, wg_pad, bg, tri4, *, B, S, T):
    N = Bq.shape[0]
    P = S + T
    gr = 256 if P % 256 == 0 else 2 * GLA_CHUNK
    per = gr // GLA_CHUNK
    ii = np.arange(GLA_CHUNK)
    tril = (ii[:, None] >= ii[None, :]).astype(np.float32)
    eye = np.eye(per, dtype=np.float32)
    tblk = jnp.asarray(np.stack([np.kron(eye, tril), np.kron(eye, tril.T)]), BF16)
    oblk = jnp.asarray(np.kron(eye, np.ones((GLA_CHUNK, GLA_CHUNK), np.float32)), BF16)
    full = lambda shape: pl.BlockSpec(shape, lambda b: (0,) * len(shape))
    return pl.pallas_call(
        functools.partial(_gla_kernel, S=S, P=P, gr=gr),
        grid=(B,),
        in_specs=[pl.BlockSpec((P, 768), lambda b: (b, 0)),
                  pl.BlockSpec((P, LANES), lambda b: (b, 0)),
                  full((2, LANES, GLA_DK)), full((2, 1, GLA_DK)),
                  full((2, gr, gr)), full((gr, gr)),
                  full((2, GLA_CHUNK, GLA_HEADS * GLA_CHUNK))],
        out_specs=pl.BlockSpec((2, P, GLA_DV), lambda b: (0, b, 0)),
        out_shape=jax.ShapeDtypeStruct((2, N, GLA_DV), BF16),
        scratch_shapes=[pltpu.VMEM((2, P, GLA_DK), BF16), pltpu.VMEM((2, P, GLA_DK), BF16),
                        pltpu.VMEM((2, P, GLA_DK), BF16), pltpu.VMEM((2, P, GLA_DK), F32)],
        compiler_params=_cparams(1),
        name="gla",
    )(Bq, LR, wg_pad, bg, tblk, oblk, tri4)


def _merge_kernel(ya_ref, of_ref, ob_ref, rb_ref, yc_ref, gt_ref, x_ref,
                  wa_ref, wb_ref, wc_ref, wo_ref, seg_ref, gg_ref, n2_ref, wr_ref,
                  g1_ref, sc2_ref, sh2_ref, g1c_ref, sc2c_ref, sh2c_ref,
                  x1_ref, xn_ref, aff_ref, *, S):
    tr = x_ref.shape[0]
    first = pl.program_id(1) == 0
    head = (jnp.where(first, g1c_ref[0], g1_ref[0]), jnp.where(first, sc2c_ref[0], sc2_ref[0]),
            jnp.where(first, sh2c_ref[0], sh2_ref[0]))
    tail = (g1_ref[0], sc2_ref[0], sh2_ref[0])

    def halves(r0, r1):
        mid = (r0 + r1) // 2
        return [(r0, mid), (mid, r1)] if (r1 - r0) % 32 == 0 and r1 - r0 >= 64 else [(r0, r1)]

    parts = [(slice(0, S),) + head]
    if tr > S:
        parts += [(slice(a, b),) + tail for a, b in halves(S, tr)]

    wh, wl = _split2(wr_ref[...])
    for r, g1, sc2, sh2 in parts:
        o = of_ref[0, r, :].astype(F32) + ob_ref[0, r, :].astype(F32)
        yb = o * lax.rsqrt(_seg_meansq(o, seg_ref) + NORM_EPS) * gg_ref[...]
        yb = (yb * _silu(rb_ref[r, :].astype(F32))).astype(BF16)

        def gated(i, d):
            return jnp.tanh(gt_ref[r, i * 1024:(i + 1) * 1024].astype(F32)) * d + d

        m2 = (gated(0, _dot(ya_ref[r, :], wa_ref[...]))
              + gated(1, _dot(yb, wb_ref[...]))
              + gated(2, _dot(yc_ref[r, :], wc_ref[...])))
        x1 = x_ref[r, :] + (0.5 * g1) * _dot(m2.astype(BF16), wo_ref[...])
        x1_ref[r, :] = x1
        y = x1 * lax.rsqrt(jnp.mean(x1 * x1, axis=-1, keepdims=True) + NORM_EPS)
        xn = y * (n2_ref[...] * (1.0 + sc2)) + sh2
        xn_ref[r, :] = xn.astype(BF16)
        xh, xl = _split2(xn)
        logits = _dot(xh, wh) + (_dot(xl, wh) + _dot(xh, wl))
        e = jnp.exp(logits - logits.max(axis=-1, keepdims=True))
        aff_ref[r, :] = e / e.sum(axis=-1, keepdims=True)


def _merge(YA, O, Bq, YC, G, X, wa, wb, wc, wo, seg, gg, g1, n2, sc2, sh2, wr, *, B, S, T):
    N, D = X.shape
    P = S + T
    tm = _row_tile(S, P)
    NT = P // tm
    E = wr.shape[1]

    def row(b, j):
        return (b * NT + j, 0)

    grp = lambda b, j: (b, 0, 0)
    grp_c = lambda b, j: (B, 0, 0)
    const = lambda b, j: (0, 0)
    mod = lambda m: pl.BlockSpec((1, 1, D), m)
    return pl.pallas_call(
        functools.partial(_merge_kernel, S=S),
        grid=(B, NT),
        in_specs=[pl.BlockSpec((tm, HW), row),
                  pl.BlockSpec((1, tm, HW), lambda b, j: (0, b * NT + j, 0)),
                  pl.BlockSpec((1, tm, HW), lambda b, j: (1, b * NT + j, 0)),
                  pl.BlockSpec((tm, HW), lambda b, j: (b * NT + j, 2)),
                  pl.BlockSpec((tm, HW), row),
                  pl.BlockSpec((tm, 3072), row),
                  pl.BlockSpec((tm, D), row),
                  pl.BlockSpec((HW, D), const), pl.BlockSpec((HW, D), const), pl.BlockSpec((HW, D), const),
                  pl.BlockSpec((D, D), const),
                  pl.BlockSpec((HW, HW), const),
                  pl.BlockSpec((1, HW), const),
                  pl.BlockSpec((1, D), const),
                  pl.BlockSpec((D, E), const),
                  mod(grp), mod(grp), mod(grp), mod(grp_c), mod(grp_c), mod(grp_c)],
        out_specs=[pl.BlockSpec((tm, D), row), pl.BlockSpec((tm, D), row), pl.BlockSpec((tm, E), row)],
        out_shape=[jax.ShapeDtypeStruct((N, D), F32), jax.ShapeDtypeStruct((N, D), BF16),
                   jax.ShapeDtypeStruct((N, E), F32)],
        compiler_params=_cparams(2),
        name="merge",
    )(YA, O, O, Bq, YC, G, X, wa, wb, wc, wo, seg, gg, n2, wr, g1, sc2, sh2, g1, sc2, sh2)


def _cumsum_lanes(m01, upper, before, n):
    ahead = _dot(m01, before)
    return jnp.concatenate(
        [_dot(m01[:, c * LANES:(c + 1) * LANES], upper) + ahead[:, c:c + 1] for c in range(n // LANES)], axis=1)


def _kth_largest_keys(keys, caps):
    def step(i, thetas):
        bit = jnp.left_shift(jnp.int32(1), 30 - i)
        out = []
        for key, cap, theta in zip(keys, caps, thetas):
            cand = theta | bit
            cnt = jnp.sum(jnp.where(key >= cand, 1.0, 0.0), axis=1, keepdims=True)
            out.append(jnp.where(cnt >= cap, cand, theta))
        return tuple(out)

    zero = jnp.zeros((keys[0].shape[0], 1), I32)
    return lax.fori_loop(0, 31, step, tuple(zero for _ in keys))


def _topk_mask(key, theta, cap, upper, before):
    n = key.shape[1]
    gt = key > theta
    eq = key == theta
    need = cap - jnp.sum(jnp.where(gt, 1.0, 0.0), axis=1, keepdims=True)
    rank = _cumsum_lanes(jnp.where(eq, 1.0, 0.0).astype(BF16), upper, before[0:n], n)
    take = eq & (rank <= need)
    return jnp.where(gt | take, 1.0, 0.0)


ROUTE_HI, ROUTE_LO, ROUTE_W = 0, 1, 2


def _select_kernel(aff_ref, upper_ref, before_ref, ind_ref, slot_ref, tok_ref, off_ref, *, S, T, cap_c, cap_l):
    aff = aff_ref[...]
    upper = upper_ref[...]
    before = before_ref[...]
    key_c = lax.bitcast_convert_type(aff[:, 0:S], I32)
    key_l = lax.bitcast_convert_type(aff[:, S:S + T], I32)
    theta_c, theta_l = _kth_largest_keys((key_c, key_l), (cap_c, cap_l))
    sel = jnp.concatenate([_topk_mask(key_c, theta_c, cap_c, upper, before),
                           _topk_mask(key_l, theta_l, cap_l, upper, before)], axis=1)
    sel16 = sel.astype(BF16)
    pos = _cumsum_lanes(sel16, upper, before, S + T) - sel
    chosen = sel > 0.5
    slot = jnp.where(chosen, pos, -1.0).astype(I32)
    slot_ref[0] = slot
    off_ref[0] = _dot(sel16, ind_ref[...]).astype(I32)
    E = aff.shape[0]
    w1, w2, w3 = _split3(jnp.where(chosen, aff, 0.0))
    rows = [(slot >> 4).astype(F32), (slot & 15).astype(F32), w1.astype(F32), w2.astype(F32), w3.astype(F32),
            jnp.zeros((LANES - 5 * E, S + T), F32)]
    tok_ref[...] = jnp.transpose(jnp.concatenate(rows, axis=0)).astype(BF16)


def _select(AFF, *, B, S, T, tm, cap_c, cap_l):
    E = AFF.shape[0]
    P = S + T
    il = np.arange(LANES)
    ip = np.arange(P)
    upper = jnp.asarray(il[:, None] <= il[None, :], BF16)
    before = jnp.asarray(ip[:, None] < (il * LANES)[None, :], BF16)
    ind = jnp.asarray(ip[:, None] < (il * tm)[None, :], BF16)
    return pl.pallas_call(
        functools.partial(_select_kernel, S=S, T=T, cap_c=cap_c, cap_l=cap_l),
        grid=(B,),
        in_specs=[pl.BlockSpec((E, P), lambda b: (0, b)),
                  pl.BlockSpec((LANES, LANES), lambda b: (0, 0)),
                  pl.BlockSpec((P, LANES), lambda b: (0, 0)),
                  pl.BlockSpec((P, LANES), lambda b: (0, 0))],
        out_specs=[pl.BlockSpec((1, E, P), lambda b: (b, 0, 0)),
                   pl.BlockSpec((P, LANES), lambda b: (b, 0)),
                   pl.BlockSpec((1, E, LANES), lambda b: (b, 0, 0))],
        out_shape=[jax.ShapeDtypeStruct((B, E, P), I32), jax.ShapeDtypeStruct((B * P, LANES), BF16),
                   jax.ShapeDtypeStruct((B, E, LANES), I32)],
        compiler_params=_cparams(1),
        name="ec_select",
    )(AFF, upper, before, ind)


EC_WIN = 64
SLOT_ALIGN = 16


def _win_base(lo, capt):
    return pl.multiple_of(jnp.minimum(lo // SLOT_ALIGN * SLOT_ALIGN, capt - EC_WIN), SLOT_ALIGN)


def _n_windows(tm):
    return -(-(tm + SLOT_ALIGN - 1) // EC_WIN)


def _any_overflow(offs, bases, E):
    over = None
    for e in range(E):
        c = offs(e)[1] > bases[e] + EC_WIN
        over = c if over is None else (over | c)
    return over


def _gather_kernel(off_ref, x_ref, wc_ref, slot_ref, xg_ref, wg_ref, *, E, NT, tm, capt):
    b = pl.program_id(0)
    j = pl.program_id(1)
    W = EC_WIN

    @pl.when(j == 0)
    def _():
        xg_ref[...] = jnp.zeros_like(xg_ref)
        wg_ref[...] = jnp.zeros_like(wg_ref)

    def offs(e):
        i = (b * E + e) * (NT + 1) + j
        return off_ref[i], off_ref[i + 1]

    rows = lax.broadcasted_iota(I32, (W, tm), 0)
    x = x_ref[...]
    wc = wc_ref[...]
    bases = [_win_base(offs(e)[0], capt) for e in range(E)]
    lhs = jnp.concatenate(
        [jnp.where(rows == slot_ref[0, e:e + 1, :] - bases[e], 1.0, 0.0).astype(BF16) for e in range(E)], axis=0)
    rx = _dot(lhs, x).astype(BF16)
    rw = _dot(lhs, wc)
    for e in range(E):
        xg_ref[e, pl.ds(bases[e], W), :] += rx[e * W:(e + 1) * W]
        wg_ref[e, pl.ds(bases[e], W), :] += rw[e * W:(e + 1) * W]

    def extra(e, carry):
        lo, hi = offs(e)
        base = _win_base(lo, capt)
        for w in range(1, _n_windows(tm)):
            first = base + w * W

            @pl.when(hi > first)
            def _():
                start = pl.multiple_of(jnp.minimum(first, capt - W), SLOT_ALIGN)
                s = slot_ref[0, pl.ds(e, 1), :]
                onehot = jnp.where(rows == jnp.where(s >= first, s - start, -1), 1.0, 0.0).astype(BF16)
                xg_ref[e, pl.ds(start, W), :] += _dot(onehot, x).astype(BF16)
                wg_ref[e, pl.ds(start, W), :] += _dot(onehot, wc)
        return carry

    @pl.when(_any_overflow(offs, bases, E))
    def _():
        lax.fori_loop(0, E, extra, 0)


def _gather(off, XN, WC, slot, *, B, E, S, T, tm, capt):
    N, D = XN.shape
    P = S + T
    NT = P // tm
    row = lambda b, j, off: (b * NT + j, 0)
    return pl.pallas_call(
        functools.partial(_gather_kernel, E=E, NT=NT, tm=tm, capt=capt),
        grid_spec=pltpu.PrefetchScalarGridSpec(
            num_scalar_prefetch=1,
            grid=(B, NT),
            in_specs=[pl.BlockSpec((tm, D), row),
                      pl.BlockSpec((tm, LANES), row),
                      pl.BlockSpec((1, E, tm), lambda b, j, off: (b, 0, j))],
            out_specs=[pl.BlockSpec((E, capt, D), lambda b, j, off: (0, b, 0)),
                       pl.BlockSpec((E, capt, LANES), lambda b, j, off: (0, b, 0))]),
        out_shape=[jax.ShapeDtypeStruct((E, B * capt, D), BF16),
                   jax.ShapeDtypeStruct((E, B * capt, LANES), F32)],
        compiler_params=_cparams(2),
        name="ec_gather",
    )(off, XN, WC, slot)


def _ffn_kernel(x_ref, val_ref, wg_ref, wu_ref, wd_ref, o_ref, acc_ref, *, E):
    f = pl.program_id(2)

    def partial_down():
        x = x_ref[0]
        g = _dot(x, wg_ref[0, 0].astype(BF16))
        u = _dot(x, wu_ref[0, 0].astype(BF16))
        h = (_silu(g) * u).astype(BF16)
        return _dot(h, wd_ref[0, 0].astype(BF16))

    @pl.when(f == 0)
    def _():
        acc_ref[...] = partial_down()

    @pl.when(f > 0)
    def _():
        acc_ref[...] += partial_down()

    @pl.when(f == pl.num_programs(2) - 1)
    def _():
        lane = lax.broadcasted_iota(I32, (1, LANES), 1)
        mine = (lane % E == pl.program_id(0)) & (lane >= ROUTE_W * E) & (lane < (ROUTE_W + 3) * E)
        val = jnp.sum(jnp.where(mine, val_ref[0], 0.0), axis=1, keepdims=True)
        o_ref[0] = (acc_ref[...] * val).astype(BF16)


def _ffn(XG, WG, w_gate, w_up, w_down, l, *, rows, fc):
    E, R, D = XG.shape
    F = w_gate.shape[-1]
    return pl.pallas_call(
        functools.partial(_ffn_kernel, E=E),
        grid=(E, R // rows, F // fc),
        in_specs=[pl.BlockSpec((1, rows, D), lambda e, r, f: (e, r, 0)),
                  pl.BlockSpec((1, rows, LANES), lambda e, r, f: (e, r, 0)),
                  pl.BlockSpec((1, 1, D, fc), lambda e, r, f: (l, e, 0, f)),
                  pl.BlockSpec((1, 1, D, fc), lambda e, r, f: (l, e, 0, f)),
                  pl.BlockSpec((1, 1, fc, D), lambda e, r, f: (l, e, f, 0))],
        out_specs=pl.BlockSpec((1, rows, D), lambda e, r, f: (e, r, 0)),
        out_shape=jax.ShapeDtypeStruct((E, R, D), BF16),
        scratch_shapes=[pltpu.VMEM((rows, D), F32)],
        compiler_params=_cparams(3),
        name="ec_ffn",
    )(XG, WG, w_gate, w_up, w_down)


def _combine_kernel(off_ref, y_ref, tok_ref, spread_ref, slot_ref, x_ref, g2_ref, fg_ref, o_ref, acc_ref,
                    *, E, NT, j0, tm, capt, final):
    b = pl.program_id(0)
    j = pl.program_id(1) + j0
    W = EC_WIN

    def offs(e):
        i = (b * E + e) * (NT + 1) + j
        return off_ref[i], off_ref[i + 1]

    bases = [_win_base(offs(e)[0], capt) for e in range(E)]
    col = lax.broadcasted_iota(I32, (1, E * W), 1)
    target = (col % W).astype(F32)
    for e in range(E):
        target = target + jnp.where(col // W == e, bases[e].astype(F32), 0.0)
    spread = _dot(tok_ref[...], spread_ref[...])
    lhs = jnp.where(spread == target, 1.0, 0.0).astype(BF16)
    rhs = jnp.concatenate([y_ref[e, pl.ds(bases[e], W), :] for e in range(E)], axis=0)
    acc_ref[...] = _dot(lhs, rhs)

    rows = lax.broadcasted_iota(I32, (W, tm), 0)

    def extra(e, carry):
        lo, hi = offs(e)
        base = _win_base(lo, capt)
        for w in range(1, _n_windows(tm)):
            first = base + w * W

            @pl.when(hi > first)
            def _():
                start = pl.multiple_of(jnp.minimum(first, capt - W), SLOT_ALIGN)
                s = slot_ref[0, pl.ds(e, 1), :]
                onehot_t = jnp.where(rows == jnp.where(s >= first, s - start, -1), 1.0, 0.0)
                onehot = jnp.transpose(onehot_t).astype(BF16)
                acc_ref[...] += _dot(onehot, y_ref[e, pl.ds(start, W), :])
        return carry

    @pl.when(_any_overflow(offs, bases, E))
    def _():
        lax.fori_loop(0, E, extra, 0)
    x2 = x_ref[...] + g2_ref[0] * acc_ref[...]
    if final:
        x2 = x2 * lax.rsqrt(jnp.mean(x2 * x2, axis=-1, keepdims=True) + NORM_EPS) * fg_ref[...]
    if final:
        o_ref[0] = x2
    else:
        o_ref[...] = x2


def _combine(off, Y, TOK, slot, X1, g2, fg, *, B, E, S, T, tm, capt, final):
    N, D = X1.shape
    P = S + T
    NT = P // tm
    ns = S // tm
    j0 = ns if final else 0
    own = np.kron(np.eye(E, dtype=np.float32), np.ones((1, EC_WIN), np.float32))
    spread = np.zeros((LANES, E * EC_WIN), np.float32)
    spread[ROUTE_HI * E:(ROUTE_HI + 1) * E] = 16.0 * own
    spread[ROUTE_LO * E:(ROUTE_LO + 1) * E] = own
    spread = jnp.asarray(spread, BF16)

    def row(b, j, off):
        return (b * NT + j + j0, 0)

    if final:
        out_spec = pl.BlockSpec((1, tm, D), lambda b, j, off: (b, j, 0))
        out_shape = jax.ShapeDtypeStruct((B, T, D), F32)
    else:
        out_spec = pl.BlockSpec((tm, D), row)
        out_shape = jax.ShapeDtypeStruct((N, D), F32)
    return pl.pallas_call(
        functools.partial(_combine_kernel, E=E, NT=NT, j0=j0, tm=tm, capt=capt, final=final),
        grid_spec=pltpu.PrefetchScalarGridSpec(
            num_scalar_prefetch=1,
            grid=(B, NT - j0),
            in_specs=[_resident((E, capt, D), lambda b, j, off: (0, b, 0)),
                      pl.BlockSpec((tm, LANES), row),
                      pl.BlockSpec((LANES, E * EC_WIN), lambda b, j, off: (0, 0)),
                      pl.BlockSpec((1, E, tm), lambda b, j, off: (b, 0, j + j0)),
                      pl.BlockSpec((tm, D), row),
                      pl.BlockSpec((1, 1, D), lambda b, j, off: (jnp.where(j + j0 < ns, B, b), 0, 0)),
                      pl.BlockSpec((1, D), lambda b, j, off: (0, 0))],
            out_specs=out_spec,
            scratch_shapes=[pltpu.VMEM((tm, D), F32)]),
        out_shape=out_shape,
        compiler_params=_cparams(2),
        name="ec_combine",
    )(off, Y, TOK, spread, slot, X1, g2, fg)


def _permute_w_in(w):
    D = w.shape[0]
    o = np.cumsum([0, 256, 256, 256, 128, 128, 256, 256, 16, 16, 256, 128, 128, 1024, 1024, 1024])
    qa, ka, va, qb, kb, vb, rb, lf, lb, qc, kc, vc, mga, mgb, mgc = [w[:, o[i]:o[i + 1]] for i in range(15)]
    grp = GQA_HEADS // GQA_KV_HEADS

    def rep(t):
        return jnp.concatenate([t[:, (h // grp) * HEAD_DIM:(h // grp + 1) * HEAD_DIM] for h in range(GQA_HEADS)], axis=1)

    pad = jnp.zeros((D, COL_END - COL_LR - 2 * GLA_LOWRANK), w.dtype)
    gates = 0.5 * jnp.concatenate([mga, mgb, mgc], axis=1)
    return jnp.concatenate([qa, ka, va, qb, kb, vb, rb, qc, rep(kc), rep(vc), gates, lf, lb, pad],
                           axis=1).astype(BF16)


def _prep_w_kernel(w_ref, o_ref):
    o_ref[...] = _permute_w_in(w_ref[0])


def _prep_w(w_in, l):
    L, D, C = w_in.shape
    tr = 256
    return pl.pallas_call(
        _prep_w_kernel,
        grid=(D // tr,),
        in_specs=[pl.BlockSpec((1, tr, C), lambda i: (l, i, 0))],
        out_specs=pl.BlockSpec((tr, COL_END), lambda i: (i, 0)),
        out_shape=jax.ShapeDtypeStruct((D, COL_END), BF16),
        compiler_params=_cparams(1),
        name="prep_w_in",
    )(w_in)


def _rope_tables(T, S):
    t = np.arange(T)
    row = (t // GRID_W).astype(np.float32)
    col = (t % GRID_W).astype(np.float32)
    lane = np.arange(HEAD_DIM)
    is_col = lane >= HEAD_DIM // 2
    fi = lane % 16
    inv_freq = ROPE_BASE ** (-jnp.arange(16, dtype=F32) / 16)
    pos = jnp.where(jnp.asarray(is_col)[None, :], jnp.asarray(col)[:, None], jnp.asarray(row)[:, None])
    ang = pos * inv_freq[jnp.asarray(fi)][None, :]
    cos = jnp.cos(ang)
    sin = jnp.sin(ang)
    sin = jnp.where(jnp.asarray((lane % 32) < 16)[None, :], -sin, sin)
    cos = jnp.concatenate([jnp.ones((S, HW), F32), jnp.tile(cos, (1, HW // HEAD_DIM))], axis=0)
    sin = jnp.concatenate([jnp.zeros((S, HW), F32), jnp.tile(sin, (1, HW // HEAD_DIM))], axis=0)
    return cos, sin


def kernel(x, c, ctx, c_ctx, w_ada, b_ada, norm1_g, norm2_g, w_in, na_rpb, gla_wg_f, gla_bg_f, gla_wg_b,
           gla_bg_b, gla_norm_g, gqa_qn_g, gqa_kn_g, w_branch_a, w_branch_b, w_branch_c, w_out, w_router,
           w_e_gate, w_e_up, w_e_down, final_norm_g):
    B, T, D = x.shape
    S = ctx.shape[1]
    L = w_ada.shape[0]
    E = w_router.shape[-1]
    P = S + T
    tm = min(256, S)
    NT = P // tm
    assert S % tm == 0 and T % tm == 0 and S % GRID_W == 0 and T % GRID_W == 0 and P % S == 0
    cap_l = EC_CAPACITY_FACTOR * T // E
    cap_ctx = EC_CAPACITY_FACTOR * S // E
    assert cap_l % SLOT_ALIGN == 0 and cap_ctx % SLOT_ALIGN == 0 and cap_l >= EC_WIN
    assert 5 * E <= LANES
    R = T // GRID_W
    kh = min(NA_KH, R)

    seg = jnp.asarray(np.kron(np.eye(HW // HEAD_DIM), np.ones((HEAD_DIM, HEAD_DIM))), BF16)
    cos_t, sin_t = _rope_tables(T, S)
    ii = np.arange(GLA_CHUNK)
    tril = (ii[:, None] >= ii[None, :]).astype(np.float32)
    tri4 = jnp.asarray(np.stack([np.tile(tril, (1, GLA_HEADS)), np.tile(tril.T, (1, GLA_HEADS))]))

    RP = -(-(B + 1) // 8) * 8
    cv = jnp.concatenate([c, c_ctx[None], jnp.zeros((RP - B - 1, D), F32)], axis=0)
    mod = _adaln(cv, w_ada, b_ada)[:, :B + 1].reshape(L, B + 1, 6, 1, D)

    X = jnp.concatenate([ctx, x], axis=1).reshape(B * P, D)
    out = None
    for l in range(L):
        last = l == L - 1
        sh1, sc1, g1, sh2, sc2, g2 = [mod[l, :, i] for i in range(6)]
        w_p = _prep_w(w_in, l)
        qn = jnp.tile(gqa_qn_g[l], HW // HEAD_DIM)[None]
        kn = jnp.tile(gqa_kn_g[l], HW // HEAD_DIM)[None]
        A, Bq, C, G, LR = _proj_in(X, norm1_g[l][None], sc1, sh1, w_p, seg, qn, kn, cos_t, sin_t,
                                   B=B, S=S, T=T)

        YA = _na_attention(A, _na_bias_table(na_rpb[l], kh), B=B, S=S, T=T)
        YC = _gqa_attention(C, B=B, S=S, T=T, tq=min(256, S))
        YA, YC = _ctx_attention(A, C, YA, YC, B=B, S=S, T=T)

        zpad = jnp.zeros((LANES - 2 * GLA_LOWRANK, GLA_DK), F32)
        zlr = jnp.zeros((GLA_LOWRANK, GLA_DK), F32)
        wg_pad = jnp.stack([jnp.concatenate([gla_wg_f[l], zlr, zpad], axis=0),
                            jnp.concatenate([zlr, gla_wg_b[l], zpad], axis=0)]).astype(BF16)
        bg = jnp.stack([gla_bg_f[l], gla_bg_b[l]])[:, None, :]
        O = _gla(Bq, LR, wg_pad, bg, tri4, B=B, S=S, T=T)

        gg = jnp.tile(gla_norm_g[l], GLA_HEADS)[None]
        X1, XN, AFFT = _merge(YA, O, Bq, YC, G, X,
                              w_branch_a[l].astype(BF16), w_branch_b[l].astype(BF16),
                              w_branch_c[l].astype(BF16), w_out[l].astype(BF16),
                              seg, gg, g1, norm2_g[l][None], sc2, sh2, w_router[l],
                              B=B, S=S, T=T)

        cap_c = 0 if last else cap_ctx
        capt = cap_l + cap_c
        slot, TOK, off = _select(jnp.transpose(AFFT), B=B, S=S, T=T, tm=tm, cap_c=cap_c, cap_l=cap_l)
        off_flat = off[:, :, :NT + 1].reshape(-1)
        XG, WG = _gather(off_flat, XN, TOK, slot, B=B, E=E, S=S, T=T, tm=tm, capt=capt)
        rows = capt * (2 if B % 2 == 0 else 1)
        Y = _ffn(XG, WG, w_e_gate, w_e_up, w_e_down, l, rows=rows, fc=min(512, w_e_gate.shape[-1]))
        res = _combine(off_flat, Y, TOK, slot, X1, g2, final_norm_g[None],
                       B=B, E=E, S=S, T=T, tm=tm, capt=capt, final=last)
        if last:
            out = res
        else:
            X = res
    return out
```

```python
import functools

import jax
import jax.numpy as jnp
import numpy as np
from jax import lax
from jax.experimental import pallas as pl
from jax.experimental.pallas import tpu as pltpu

F32 = jnp.float32
BF16 = jnp.bfloat16
I32 = jnp.int32

GRID_W = 64
HEAD_DIM = 64
NORM_EPS = 1e-6
NA_HEADS = 4
NA_KH = 8
NA_KW = 16
GLA_HEADS = 4
GLA_DK = 128
GLA_DV = 256
GLA_LOWRANK = 16
GLA_TAU = 16.0
GLA_CHUNK = 64
GQA_HEADS = 4
GQA_KV_HEADS = 2
ROPE_BASE = 10000.0
N_EXPERTS = 16
EC_CAPACITY_FACTOR = 2
NEG_BIG = -1e30

LANES = 128
HW = 256
VMEM_LIMIT = 56 * 1024 * 1024

COL_A = 0
COL_B = 768
COL_C = 1536
COL_G = 2304
COL_LR = 5376
COL_END = 5504


def _cparams(n_grid):
    return pltpu.CompilerParams(dimension_semantics=("arbitrary",) * n_grid,
                                vmem_limit_bytes=VMEM_LIMIT)


def _resident(block_shape, index_map):
    return pl.BlockSpec(block_shape, index_map, pipeline_mode=pl.Buffered(1))


def _dot(a, b):
    return jnp.dot(a, b, preferred_element_type=F32)


def _dot_nt(a, b):
    return lax.dot_general(a, b, (((1,), (1,)), ((), ())), preferred_element_type=F32)


def _split2(a):
    hi = a.astype(BF16)
    lo = (a - hi.astype(F32)).astype(BF16)
    return hi, lo


def _split3(a):
    h1 = a.astype(BF16)
    r1 = a - h1.astype(F32)
    h2 = r1.astype(BF16)
    h3 = (r1 - h2.astype(F32)).astype(BF16)
    return h1, h2, h3


def _seg_meansq(x, seg_ref):
    hi, lo = _split2(x * x)
    seg = seg_ref[...]
    return (_dot(hi, seg) + _dot(lo, seg)) * (1.0 / HEAD_DIM)


def _sigmoid(x):
    return 0.5 * jnp.tanh(0.5 * x) + 0.5


def _silu(x):
    return x * _sigmoid(x)


def _adaln_kernel(cv_ref, w_ref, b_ref, o_ref):
    s = _silu(cv_ref[...])
    w = w_ref[0]
    s1, s2, s3 = _split3(s)
    w1, w2, w3 = _split3(w)
    acc = _dot(s1, w1) + (_dot(s1, w2) + _dot(s2, w1)) + (_dot(s2, w2) + _dot(s1, w3) + _dot(s3, w1))
    o_ref[0] = acc + b_ref[0]


def _adaln(cv, w_ada, b_ada):
    L, D, D6 = w_ada.shape
    R = cv.shape[0]
    tn = 1024
    return pl.pallas_call(
        _adaln_kernel,
        grid=(L, D6 // tn),
        in_specs=[pl.BlockSpec((R, D), lambda l, n: (0, 0)),
                  pl.BlockSpec((1, D, tn), lambda l, n: (l, 0, n)),
                  pl.BlockSpec((1, 1, tn), lambda l, n: (l, 0, n))],
        out_specs=pl.BlockSpec((1, R, tn), lambda l, n: (l, 0, n)),
        out_shape=jax.ShapeDtypeStruct((L, R, D6), F32),
        compiler_params=_cparams(2),
        name="adaln",
    )(cv, w_ada, b_ada.reshape(L, 1, D6))


def _swap16(y):
    lane = lax.broadcasted_iota(I32, y.shape, 1)
    first = (lane % 32) < 16
    return jnp.where(first, pltpu.roll(y, LANES - 16, 1), pltpu.roll(y, 16, 1))


def _norm_rope(x, seg_ref, g_ref, cos_ref, sin_ref, scale):
    y = x * lax.rsqrt(_seg_meansq(x, seg_ref) + NORM_EPS) * g_ref[...]
    halves = []
    for i in range(HW // LANES):
        sl = slice(i * LANES, (i + 1) * LANES)
        yh = y[:, sl]
        halves.append(yh * cos_ref[:, sl] + _swap16(yh) * sin_ref[:, sl])
    out = jnp.concatenate(halves, axis=1)
    return out * scale if scale != 1.0 else out


def _proj_in_kernel(x_ref, g_ref, sc_ref, sh_ref, scc_ref, shc_ref, w_ref, seg_ref, qn_ref, kn_ref,
                    cos_ref, sin_ref, a_ref, b_ref, c_ref, gt_ref, lr_ref, *, S):
    tr = x_ref.shape[0]
    first = pl.program_id(1) == 0
    parts = [(slice(0, S), jnp.where(first, scc_ref[0], sc_ref[0]), jnp.where(first, shc_ref[0], sh_ref[0]))]
    if tr > S:
        parts.append((slice(S, tr), sc_ref[0], sh_ref[0]))
    qscale = HEAD_DIM ** -0.5
    for r, sc, sh in parts:
        x = x_ref[r, :]
        y = x * lax.rsqrt(jnp.mean(x * x, axis=-1, keepdims=True) + NORM_EPS)
        u = (y * (g_ref[...] * (1.0 + sc)) + sh).astype(BF16)

        def proj(c0, c1):
            return _dot(u, w_ref[:, c0:c1])

        def rope(t, gain_ref, scale):
            return _norm_rope(t, seg_ref, gain_ref, cos_ref[r, :], sin_ref[r, :], scale)

        a_ref[r, 0:HW] = (proj(COL_A, COL_A + HW) * qscale).astype(BF16)
        a_ref[r, HW:3 * HW] = proj(COL_A + HW, COL_B).astype(BF16)
        b_ref[r, :] = proj(COL_B, COL_C).astype(BF16)
        c_ref[r, 0:HW] = rope(proj(COL_C, COL_C + HW), qn_ref, qscale).astype(BF16)
        c_ref[r, HW:2 * HW] = rope(proj(COL_C + HW, COL_C + 2 * HW), kn_ref, 1.0).astype(BF16)
        c_ref[r, 2 * HW:3 * HW] = proj(COL_C + 2 * HW, COL_G).astype(BF16)
        for i in range(3):
            gt_ref[r, i * 1024:(i + 1) * 1024] = proj(COL_G + i * 1024, COL_G + (i + 1) * 1024).astype(BF16)
        lr_ref[r, :] = proj(COL_LR, COL_END).astype(BF16)


def _row_tile(S, P):
    for k in (8, 4, 2, 1):
        if P % k == 0 and (P // k) % 16 == 0 and P // k >= S:
            return P // k
    raise ValueError("no row tile for these shapes")


def _proj_in(X, g, sc, sh, w_p, seg, qn, kn, cos_t, sin_t, *, B, S, T):
    N, D = X.shape
    P = S + T
    tm = _row_tile(S, P)
    if P % (2 * tm) == 0:
        tm = 2 * tm
    NT = P // tm

    def row(b, j):
        return (b * NT + j, 0)

    grp = lambda b, j: (b, 0, 0)
    grp_c = lambda b, j: (B, 0, 0)
    tab = lambda b, j: (j, 0)
    const = lambda b, j: (0, 0)
    outs = [jax.ShapeDtypeStruct((N, 768), BF16), jax.ShapeDtypeStruct((N, 768), BF16),
            jax.ShapeDtypeStruct((N, 768), BF16), jax.ShapeDtypeStruct((N, 3072), BF16),
            jax.ShapeDtypeStruct((N, LANES), BF16)]
    return pl.pallas_call(
        functools.partial(_proj_in_kernel, S=S),
        grid=(B, NT),
        in_specs=[pl.BlockSpec((tm, D), row),
                  pl.BlockSpec((1, D), const),
                  pl.BlockSpec((1, 1, D), grp),
                  pl.BlockSpec((1, 1, D), grp),
                  pl.BlockSpec((1, 1, D), grp_c),
                  pl.BlockSpec((1, 1, D), grp_c),
                  _resident((D, COL_END), const),
                  pl.BlockSpec((HW, HW), const),
                  pl.BlockSpec((1, HW), const),
                  pl.BlockSpec((1, HW), const),
                  pl.BlockSpec((tm, HW), tab),
                  pl.BlockSpec((tm, HW), tab)],
        out_specs=[pl.BlockSpec((tm, 768), row), pl.BlockSpec((tm, 768), row),
                   pl.BlockSpec((tm, 768), row), pl.BlockSpec((tm, 3072), row),
                   pl.BlockSpec((tm, LANES), row)],
        out_shape=outs,
        compiler_params=_cparams(2),
        name="proj_in",
    )(X, g, sc, sh, sc, sh, w_p, seg, qn, kn, cos_t, sin_t)


def _mh_attend(q, segments):
    nq = q.shape[0]
    nh = HW // HEAD_DIM
    lane = lax.broadcasted_iota(I32, (1, HW), 1) // HEAD_DIM
    qs = jnp.concatenate([jnp.where(lane == h, q, jnp.zeros_like(q)) for h in range(nh)], axis=0)
    mx = den = o = None
    for k, v, bias in segments:
        s = _dot_nt(qs, k)
        if bias is not None:
            s = s + bias
        smax = s.max(axis=-1, keepdims=True)
        if mx is None:
            mx = smax
            p = jnp.exp(s - mx)
            den = p.sum(axis=-1, keepdims=True)
            o = _dot(p.astype(BF16), v)
        else:
            mx_new = jnp.maximum(mx, smax)
            alpha = jnp.exp(mx - mx_new)
            p = jnp.exp(s - mx_new)
            den = den * alpha + p.sum(axis=-1, keepdims=True)
            o = o * alpha + _dot(p.astype(BF16), v)
            mx = mx_new
    o = o / den
    acc = jnp.zeros(q.shape, F32)
    for h in range(nh):
        acc = acc + jnp.where(lane == h, o[h * nq:(h + 1) * nq], 0.0)
    return acc


def _na_kernel(q_ref, k_ref, v_ref, bias_ref, o_ref, *, S, R, kh, rb):
    for i in range(rb):
        r = pl.program_id(1) * rb + i
        r0 = jnp.clip(r - kh // 2, 0, R - kh)
        start = pl.multiple_of(S + r0 * GRID_W, GRID_W)
        kwin = k_ref[pl.ds(start, kh * GRID_W), :]
        vwin = v_ref[pl.ds(start, kh * GRID_W), :]
        segs = [(kwin, vwin, bias_ref[r - r0]),
                (k_ref[0:S, :], v_ref[0:S, :], None)]
        rows = slice(i * GRID_W, (i + 1) * GRID_W)
        o_ref[rows, :] = _mh_attend(q_ref[rows, :], segs).astype(BF16)


def _na_bias_table(rpb, kh):
    H = rpb.shape[0]
    W = GRID_W
    col = np.arange(W)
    col_start = np.clip(col - NA_KW // 2, 0, W - NA_KW)
    inwin = (col[None, :] >= col_start[:, None]) & (col[None, :] < col_start[:, None] + NA_KW)
    rp = jnp.pad(rpb.astype(F32), ((0, 0), (0, 0), (W, W)))
    toe = jnp.stack([rp[:, :, W + NA_KW - 1 - q:2 * W + NA_KW - 1 - q] for q in range(W)], axis=2)
    toe = jnp.where(jnp.asarray(inwin)[None, None], toe, NEG_BIG)
    t = jnp.stack([toe[:, NA_KH - 1 - c:NA_KH - 1 - c + kh] for c in range(kh)], axis=0)
    t = jnp.transpose(t, (0, 1, 3, 2, 4))
    return t.reshape(kh, H * W, kh * W)


def _na_attention(A, bias_t, *, B, S, T):
    N = A.shape[0]
    P = S + T
    R = T // GRID_W
    kh = min(NA_KH, R)
    rb = 4 if (R % 4 == 0 and S % (4 * GRID_W) == 0) else 1
    tq = rb * GRID_W
    qoff = S // tq
    per_b = P // tq

    return pl.pallas_call(
        functools.partial(_na_kernel, S=S, R=R, kh=kh, rb=rb),
        grid=(B, R // rb),
        in_specs=[pl.BlockSpec((tq, HW), lambda b, r: (b * per_b + qoff + r, 0)),
                  pl.BlockSpec((P, HW), lambda b, r: (b, 1)),
                  pl.BlockSpec((P, HW), lambda b, r: (b, 2)),
                  pl.BlockSpec((kh, NA_HEADS * GRID_W, kh * GRID_W), lambda b, r: (0, 0, 0))],
        out_specs=pl.BlockSpec((tq, HW), lambda b, r: (b * per_b + qoff + r, 0)),
        out_shape=jax.ShapeDtypeStruct((N, HW), BF16),
        compiler_params=_cparams(2),
        name="na_attention",
    )(A, A, A, bias_t)


def _key_chunks(n, target=512, align=256):
    nchunks = max(1, n // target)
    base = (n // nchunks) // align * align
    if base == 0:
        return [(0, n)]
    bounds = [i * base for i in range(nchunks)] + [n]
    return [(bounds[i], bounds[i + 1]) for i in range(nchunks)]


def _gqa_kernel(q_ref, k_ref, v_ref, o_ref):
    q = q_ref[...]
    nq = q.shape[0]
    nh = HW // HEAD_DIM
    lane = lax.broadcasted_iota(I32, (1, HW), 1) // HEAD_DIM
    odd = (lane % 2) == 1
    qs = jnp.concatenate([jnp.where(lane == h, q, jnp.zeros_like(q)) for h in range(nh)], axis=0)
    mx = o = None
    for c0, c1 in _key_chunks(k_ref.shape[0]):
        v = v_ref[c0:c1, :]
        v1 = jnp.where(odd, jnp.ones_like(v), v)
        s = _dot_nt(qs, k_ref[c0:c1, :])
        smax = s.max(axis=-1, keepdims=True)
        if mx is None:
            mx = smax
            o = _dot(jnp.exp(s - mx).astype(BF16), v1)
        else:
            mx_new = jnp.maximum(mx, smax)
            o = o * jnp.exp(mx - mx_new) + _dot(jnp.exp(s - mx_new).astype(BF16), v1)
            mx = mx_new
    o = o / o[:, HEAD_DIM:HEAD_DIM + 1]
    acc = jnp.zeros(q.shape, F32)
    for h in range(nh):
        blk = o[h * nq:(h + 1) * nq]
        if h % 2 == 1:
            blk = jnp.concatenate([pltpu.roll(blk[:, i:i + LANES], HEAD_DIM, 1) for i in range(0, HW, LANES)], axis=1)
        acc = acc + jnp.where(lane == h, blk, 0.0)
    o_ref[...] = acc.astype(BF16)


def _gqa_attention(C, *, B, S, T, tq):
    N = C.shape[0]
    P = S + T
    nq = T // tq
    qoff = S // tq
    per_b = P // tq
    return pl.pallas_call(
        _gqa_kernel,
        grid=(B, nq),
        in_specs=[pl.BlockSpec((tq, HW), lambda b, i: (b * per_b + qoff + i, 0)),
                  pl.BlockSpec((P, HW), lambda b, i: (b, 1)),
                  pl.BlockSpec((P, HW), lambda b, i: (b, 2))],
        out_specs=pl.BlockSpec((tq, HW), lambda b, i: (b * per_b + qoff + i, 0)),
        out_shape=jax.ShapeDtypeStruct((N, HW), BF16),
        compiler_params=_cparams(2),
        name="gqa_attention",
    )(C, C, C)


def _ctx_attn_kernel(qa_ref, ka_ref, va_ref, qc_ref, kc_ref, vc_ref, ya_in, yc_in, ya_ref, yc_ref):
    del ya_in, yc_in
    ya_ref[...] = _mh_attend(qa_ref[...], [(ka_ref[...], va_ref[...], None)]).astype(BF16)
    yc_ref[...] = _mh_attend(qc_ref[...], [(kc_ref[...], vc_ref[...], None)]).astype(BF16)


def _ctx_attention(A, C, YA, YC, *, B, S, T):
    per_b = (S + T) // S
    blk = lambda c: pl.BlockSpec((S, HW), lambda b: (b * per_b, c))
    anyspec = pl.BlockSpec(memory_space=pl.ANY)
    return pl.pallas_call(
        _ctx_attn_kernel,
        grid=(B,),
        in_specs=[blk(0), blk(1), blk(2), blk(0), blk(1), blk(2), anyspec, anyspec],
        out_specs=[blk(0), blk(0)],
        out_shape=[jax.ShapeDtypeStruct(YA.shape, YA.dtype), jax.ShapeDtypeStruct(YC.shape, YC.dtype)],
        input_output_aliases={6: 0, 7: 1},
        compiler_params=_cparams(1),
        name="ctx_attention",
    )(A, A, A, C, C, C, YA, YC)


def _gla_kernel(bq_ref, lr_ref, wg_ref, bg_ref, tblk_ref, oblk_ref, tri4_ref, o_ref,
                qd_scr, ki_scr, ke_scr, dec_scr, *, S, P, gr):
    C = GLA_CHUNK
    ncs = S // C
    nc = P // C
    dk_h = GLA_DK // GLA_HEADS
    dv_h = GLA_DV // GLA_HEADS

    def thirds(a):
        return a[:, 0:GLA_DK] + a[:, GLA_DK:2 * GLA_DK] + a[:, 2 * GLA_DK:3 * GLA_DK]

    def decays(gi, carry):
        rows = pl.ds(pl.multiple_of(gi * gr, gr), gr)
        q = bq_ref[rows, 0:GLA_DK].astype(F32)
        k = bq_ref[rows, GLA_DK:2 * GLA_DK].astype(F32)
        lr = lr_ref[rows, :]
        for d in range(2):
            z = _dot(lr, wg_ref[d]) + bg_ref[d]
            g = (jnp.minimum(z, 0.0) - jnp.log(1.0 + jnp.exp(-jnp.abs(z)))) * (1.0 / GLA_TAU)
            g3 = jnp.concatenate(_split3(g), axis=1)
            bcum = thirds(_dot(tblk_ref[d], g3))
            btot = thirds(_dot(oblk_ref[...], g3))
            qd_scr[d, rows, :] = (q * (dk_h ** -0.5) * jnp.exp(bcum)).astype(BF16)
            ki_scr[d, rows, :] = (k * jnp.exp(-bcum)).astype(BF16)
            ke_scr[d, rows, :] = (k * jnp.exp(btot - bcum)).astype(BF16)
            dec_scr[d, rows, :] = jnp.exp(btot)
        return carry

    lax.fori_loop(0, P // gr, decays, 0, unroll=2)

    lane_k = lax.broadcasted_iota(I32, (1, GLA_DK), 1) // dk_h
    lane_v = lax.broadcasted_iota(I32, (1, GLA_DV), 1) // dv_h
    bd = (lax.broadcasted_iota(I32, (GLA_DV, GLA_DK), 0) // dv_h
          == lax.broadcasted_iota(I32, (GLA_DV, GLA_DK), 1) // dk_h)

    def chunk(d, ci, st):
        rows = pl.ds(pl.multiple_of(ci * C, C), C)
        q_dec = qd_scr[d, rows, :]
        k_inv = ki_scr[d, rows, :]
        v = bq_ref[rows, 2 * GLA_DK:2 * GLA_DK + GLA_DV]
        k_stack = jnp.concatenate(
            [jnp.where(lane_k == h, k_inv, jnp.zeros_like(k_inv)) for h in range(GLA_HEADS)], axis=0)
        a_cat = (_dot_nt(q_dec, k_stack) * tri4_ref[d]).astype(BF16)
        v_stack = jnp.concatenate(
            [jnp.where(lane_v == h, v, jnp.zeros_like(v)) for h in range(GLA_HEADS)], axis=0)
        lhs = jnp.concatenate([a_cat, q_dec], axis=1)
        rhs = jnp.concatenate([v_stack, jnp.transpose(st).astype(BF16)], axis=0)
        o_ref[d, rows, :] = _dot(lhs, rhs).astype(BF16)
        v_t = jnp.transpose(v.astype(F32)).astype(BF16)
        ds_t = _dot(v_t, ke_scr[d, rows, :])
        return st * dec_scr[d, pl.ds(pl.multiple_of(ci * C, C), 1), :] + jnp.where(bd, ds_t, 0.0)

    def body(n, sts):
        cb = jnp.where(n < ncs, ncs - 1 - n, nc - 1 - (n - ncs))
        return chunk(0, n, sts[0]), chunk(1, cb, sts[1])

    zero = jnp.zeros((GLA_DV, GLA_DK), F32)
    lax.fori_loop(0, nc, body, (zero, zero), unroll=8)


def _gla(Bq, LR, wg_pad, bg, tri4, *, B, S, T):
    N = Bq.shape[0]
    P = S + T
    gr = 256 if P % 256 == 0 else 2 * GLA_CHUNK
    per = gr // GLA_CHUNK
    ii = np.arange(GLA_CHUNK)
    tril = (ii[:, None] >= ii[None, :]).astype(np.float32)
    eye = np.eye(per, dtype=np.float32)
    tblk = jnp.asarray(np.stack([np.kron(eye, tril), np.kron(eye, tril.T)]), BF16)
    oblk = jnp.asarray(np.kron(eye, np.ones((GLA_CHUNK, GLA_CHUNK), np.float32)), BF16)
    full = lambda shape: pl.BlockSpec(shape, lambda b: (0,) * len(shape))
    return pl.pallas_call(
        functools.partial(_gla_kernel, S=S, P=P, gr=gr),
        grid=(B,),
        in_specs=[pl.BlockSpec((P, 768), lambda b: (b, 0)),
                  pl.BlockSpec((P, LANES), lambda b: (b, 0)),
                  full((2, LANES, GLA_DK)), full((2, 1, GLA_DK)),
                  full((2, gr, gr)), full((gr, gr)),
                  full((2, GLA_CHUNK, GLA_HEADS * GLA_CHUNK))],
        out_specs=pl.BlockSpec((2, P, GLA_DV), lambda b: (0, b, 0)),
        out_shape=jax.ShapeDtypeStruct((2, N, GLA_DV), BF16),
        scratch_shapes=[pltpu.VMEM((2, P, GLA_DK), BF16), pltpu.VMEM((2, P, GLA_DK), BF16),
                        pltpu.VMEM((2, P, GLA_DK), BF16), pltpu.VMEM((2, P, GLA_DK), F32)],
        compiler_params=_cparams(1),
        name="gla",
    )(Bq, LR, wg_pad, bg, tblk, oblk, tri4)


def _merge_kernel(ya_ref, of_ref, ob_ref, rb_ref, yc_ref, gt_ref, x_ref,
                  wa_ref, wb_ref, wc_ref, wo_ref, seg_ref, gg_ref, n2_ref, wr_ref,
                  g1_ref, sc2_ref, sh2_ref, g1c_ref, sc2c_ref, sh2c_ref,
                  x1_ref, xn_ref, aff_ref, *, S):
    tr = x_ref.shape[0]
    first = pl.program_id(1) == 0
    head = (jnp.where(first, g1c_ref[0], g1_ref[0]), jnp.where(first, sc2c_ref[0], sc2_ref[0]),
            jnp.where(first, sh2c_ref[0], sh2_ref[0]))
    tail = (g1_ref[0], sc2_ref[0], sh2_ref[0])

    def halves(r0, r1):
        mid = (r0 + r1) // 2
        return [(r0, mid), (mid, r1)] if (r1 - r0) % 32 == 0 and r1 - r0 >= 64 else [(r0, r1)]

    parts = [(slice(0, S),) + head]
    if tr > S:
        parts += [(slice(a, b),) + tail for a, b in halves(S, tr)]

    wh, wl = _split2(wr_ref[...])
    for r, g1, sc2, sh2 in parts:
        o = of_ref[0, r, :].astype(F32) + ob_ref[0, r, :].astype(F32)
        yb = o * lax.rsqrt(_seg_meansq(o, seg_ref) + NORM_EPS) * gg_ref[...]
        yb = (yb * _silu(rb_ref[r, :].astype(F32))).astype(BF16)

        def gated(i, d):
            return jnp.tanh(gt_ref[r, i * 1024:(i + 1) * 1024].astype(F32)) * d + d

        m2 = (gated(0, _dot(ya_ref[r, :], wa_ref[...]))
              + gated(1, _dot(yb, wb_ref[...]))
              + gated(2, _dot(yc_ref[r, :], wc_ref[...])))
        x1 = x_ref[r, :] + (0.5 * g1) * _dot(m2.astype(BF16), wo_ref[...])
        x1_ref[r, :] = x1
        y = x1 * lax.rsqrt(jnp.mean(x1 * x1, axis=-1, keepdims=True) + NORM_EPS)
        xn = y * (n2_ref[...] * (1.0 + sc2)) + sh2
        xn_ref[r, :] = xn.astype(BF16)
        xh, xl = _split2(xn)
        logits = _dot(xh, wh) + (_dot(xl, wh) + _dot(xh, wl))
        e = jnp.exp(logits - logits.max(axis=-1, keepdims=True))
        aff_ref[r, :] = e / e.sum(axis=-1, keepdims=True)


def _merge(YA, O, Bq, YC, G, X, wa, wb, wc, wo, seg, gg, g1, n2, sc2, sh2, wr, *, B, S, T):
    N, D = X.shape
    P = S + T
    tm = _row_tile(S, P)
    NT = P // tm
    E = wr.shape[1]

    def row(b, j):
        return (b * NT + j, 0)

    grp = lambda b, j: (b, 0, 0)
    grp_c = lambda b, j: (B, 0, 0)
    const = lambda b, j: (0, 0)
    mod = lambda m: pl.BlockSpec((1, 1, D), m)
    return pl.pallas_call(
        functools.partial(_merge_kernel, S=S),
        grid=(B, NT),
        in_specs=[pl.BlockSpec((tm, HW), row),
                  pl.BlockSpec((1, tm, HW), lambda b, j: (0, b * NT + j, 0)),
                  pl.BlockSpec((1, tm, HW), lambda b, j: (1, b * NT + j, 0)),
                  pl.BlockSpec((tm, HW), lambda b, j: (b * NT + j, 2)),
                  pl.BlockSpec((tm, HW), row),
                  pl.BlockSpec((tm, 3072), row),
                  pl.BlockSpec((tm, D), row),
                  pl.BlockSpec((HW, D), const), pl.BlockSpec((HW, D), const), pl.BlockSpec((HW, D), const),
                  pl.BlockSpec((D, D), const),
                  pl.BlockSpec((HW, HW), const),
                  pl.BlockSpec((1, HW), const),
                  pl.BlockSpec((1, D), const),
                  pl.BlockSpec((D, E), const),
                  mod(grp), mod(grp), mod(grp), mod(grp_c), mod(grp_c), mod(grp_c)],
        out_specs=[pl.BlockSpec((tm, D), row), pl.BlockSpec((tm, D), row), pl.BlockSpec((tm, E), row)],
        out_shape=[jax.ShapeDtypeStruct((N, D), F32), jax.ShapeDtypeStruct((N, D), BF16),
                   jax.ShapeDtypeStruct((N, E), F32)],
        compiler_params=_cparams(2),
        name="merge",
    )(YA, O, O, Bq, YC, G, X, wa, wb, wc, wo, seg, gg, n2, wr, g1, sc2, sh2, g1, sc2, sh2)


def _cumsum_lanes(m01, upper, before, n):
    ahead = _dot(m01, before)
    return jnp.concatenate(
        [_dot(m01[:, c * LANES:(c + 1) * LANES], upper) + ahead[:, c:c + 1] for c in range(n // LANES)], axis=1)


def _kth_largest_keys(keys, caps):
    def step(i, thetas):
        bit = jnp.left_shift(jnp.int32(1), 30 - i)
        out = []
        for key, cap, theta in zip(keys, caps, thetas):
            cand = theta | bit
            cnt = jnp.sum(jnp.where(key >= cand, 1.0, 0.0), axis=1, keepdims=True)
            out.append(jnp.where(cnt >= cap, cand, theta))
        return tuple(out)

    zero = jnp.zeros((keys[0].shape[0], 1), I32)
    return lax.fori_loop(0, 31, step, tuple(zero for _ in keys))


def _topk_mask(key, theta, cap, upper, before):
    n = key.shape[1]
    gt = key > theta
    eq = key == theta
    need = cap - jnp.sum(jnp.where(gt, 1.0, 0.0), axis=1, keepdims=True)
    rank = _cumsum_lanes(jnp.where(eq, 1.0, 0.0).astype(BF16), upper, before[0:n], n)
    take = eq & (rank <= need)
    return jnp.where(gt | take, 1.0, 0.0)


ROUTE_HI, ROUTE_LO, ROUTE_W = 0, 1, 2


def _select_kernel(aff_ref, upper_ref, before_ref, ind_ref, slot_ref, tok_ref, off_ref, *, S, T, cap_c, cap_l):
    aff = aff_ref[...]
    upper = upper_ref[...]
    before = before_ref[...]
    key_c = lax.bitcast_convert_type(aff[:, 0:S], I32)
    key_l = lax.bitcast_convert_type(aff[:, S:S + T], I32)
    theta_c, theta_l = _kth_largest_keys((key_c, key_l), (cap_c, cap_l))
    sel = jnp.concatenate([_topk_mask(key_c, theta_c, cap_c, upper, before),
                           _topk_mask(key_l, theta_l, cap_l, upper, before)], axis=1)
    sel16 = sel.astype(BF16)
    pos = _cumsum_lanes(sel16, upper, before, S + T) - sel
    chosen = sel > 0.5
    slot = jnp.where(chosen, pos, -1.0).astype(I32)
    slot_ref[0] = slot
    off_ref[0] = _dot(sel16, ind_ref[...]).astype(I32)
    E = aff.shape[0]
    w1, w2, w3 = _split3(jnp.where(chosen, aff, 0.0))
    rows = [(slot >> 4).astype(F32), (slot & 15).astype(F32), w1.astype(F32), w2.astype(F32), w3.astype(F32),
            jnp.zeros((LANES - 5 * E, S + T), F32)]
    tok_ref[...] = jnp.transpose(jnp.concatenate(rows, axis=0)).astype(BF16)


def _select(AFF, *, B, S, T, tm, cap_c, cap_l):
    E = AFF.shape[0]
    P = S + T
    il = np.arange(LANES)
    ip = np.arange(P)
    upper = jnp.asarray(il[:, None] <= il[None, :], BF16)
    before = jnp.asarray(ip[:, None] < (il * LANES)[None, :], BF16)
    ind = jnp.asarray(ip[:, None] < (il * tm)[None, :], BF16)
    return pl.pallas_call(
        functools.partial(_select_kernel, S=S, T=T, cap_c=cap_c, cap_l=cap_l),
        grid=(B,),
        in_specs=[pl.BlockSpec((E, P), lambda b: (0, b)),
                  pl.BlockSpec((LANES, LANES), lambda b: (0, 0)),
                  pl.BlockSpec((P, LANES), lambda b: (0, 0)),
                  pl.BlockSpec((P, LANES), lambda b: (0, 0))],
        out_specs=[pl.BlockSpec((1, E, P), lambda b: (b, 0, 0)),
                   pl.BlockSpec((P, LANES), lambda b: (b, 0)),
                   pl.BlockSpec((1, E, LANES), lambda b: (b, 0, 0))],
        out_shape=[jax.ShapeDtypeStruct((B, E, P), I32), jax.ShapeDtypeStruct((B * P, LANES), BF16),
                   jax.ShapeDtypeStruct((B, E, LANES), I32)],
        compiler_params=_cparams(1),
        name="ec_select",
    )(AFF, upper, before, ind)


EC_WIN = 64
SLOT_ALIGN = 16


def _win_base(lo, capt):
    return pl.multiple_of(jnp.minimum(lo // SLOT_ALIGN * SLOT_ALIGN, capt - EC_WIN), SLOT_ALIGN)


def _n_windows(tm):
    return -(-(tm + SLOT_ALIGN - 1) // EC_WIN)


def _any_overflow(offs, bases, E):
    over = None
    for e in range(E):
        c = offs(e)[1] > bases[e] + EC_WIN
        over = c if over is None else (over | c)
    return over


def _gather_kernel(off_ref, x_ref, wc_ref, slot_ref, xg_ref, wg_ref, *, E, NT, tm, capt):
    b = pl.program_id(0)
    j = pl.program_id(1)
    W = EC_WIN

    @pl.when(j == 0)
    def _():
        xg_ref[...] = jnp.zeros_like(xg_ref)
        wg_ref[...] = jnp.zeros_like(wg_ref)

    def offs(e):
        i = (b * E + e) * (NT + 1) + j
        return off_ref[i], off_ref[i + 1]

    rows = lax.broadcasted_iota(I32, (W, tm), 0)
    x = x_ref[...]
    wc = wc_ref[...]
    bases = [_win_base(offs(e)[0], capt) for e in range(E)]
    lhs = jnp.concatenate(
        [jnp.where(rows == slot_ref[0, e:e + 1, :] - bases[e], 1.0, 0.0).astype(BF16) for e in range(E)], axis=0)
    rx = _dot(lhs, x).astype(BF16)
    rw = _dot(lhs, wc)
    for e in range(E):
        xg_ref[e, pl.ds(bases[e], W), :] += rx[e * W:(e + 1) * W]
        wg_ref[e, pl.ds(bases[e], W), :] += rw[e * W:(e + 1) * W]

    def extra(e, carry):
        lo, hi = offs(e)
        base = _win_base(lo, capt)
        for w in range(1, _n_windows(tm)):
            first = base + w * W

            @pl.when(hi > first)
            def _():
                start = pl.multiple_of(jnp.minimum(first, capt - W), SLOT_ALIGN)
                s = slot_ref[0, pl.ds(e, 1), :]
                onehot = jnp.where(rows == jnp.where(s >= first, s - start, -1), 1.0, 0.0).astype(BF16)
                xg_ref[e, pl.ds(start, W), :] += _dot(onehot, x).astype(BF16)
                wg_ref[e, pl.ds(start, W), :] += _dot(onehot, wc)
        return carry

    @pl.when(_any_overflow(offs, bases, E))
    def _():
        lax.fori_loop(0, E, extra, 0)


def _gather(off, XN, WC, slot, *, B, E, S, T, tm, capt):
    N, D = XN.shape
    P = S + T
    NT = P // tm
    row = lambda b, j, off: (b * NT + j, 0)
    return pl.pallas_call(
        functools.partial(_gather_kernel, E=E, NT=NT, tm=tm, capt=capt),
        grid_spec=pltpu.PrefetchScalarGridSpec(
            num_scalar_prefetch=1,
            grid=(B, NT),
            in_specs=[pl.BlockSpec((tm, D), row),
                      pl.BlockSpec((tm, LANES), row),
                      pl.BlockSpec((1, E, tm), lambda b, j, off: (b, 0, j))],
            out_specs=[pl.BlockSpec((E, capt, D), lambda b, j, off: (0, b, 0)),
                       pl.BlockSpec((E, capt, LANES), lambda b, j, off: (0, b, 0))]),
        out_shape=[jax.ShapeDtypeStruct((E, B * capt, D), BF16),
                   jax.ShapeDtypeStruct((E, B * capt, LANES), F32)],
        compiler_params=_cparams(2),
        name="ec_gather",
    )(off, XN, WC, slot)


def _ffn_kernel(x_ref, val_ref, wg_ref, wu_ref, wd_ref, o_ref, acc_ref, *, E):
    f = pl.program_id(2)

    def partial_down():
        x = x_ref[0]
        g = _dot(x, wg_ref[0, 0].astype(BF16))
        u = _dot(x, wu_ref[0, 0].astype(BF16))
        h = (_silu(g) * u).astype(BF16)
        return _dot(h, wd_ref[0, 0].astype(BF16))

    @pl.when(f == 0)
    def _():
        acc_ref[...] = partial_down()

    @pl.when(f > 0)
    def _():
        acc_ref[...] += partial_down()

    @pl.when(f == pl.num_programs(2) - 1)
    def _():
        lane = lax.broadcasted_iota(I32, (1, LANES), 1)
        mine = (lane % E == pl.program_id(0)) & (lane >= ROUTE_W * E) & (lane < (ROUTE_W + 3) * E)
        val = jnp.sum(jnp.where(mine, val_ref[0], 0.0), axis=1, keepdims=True)
        o_ref[0] = (acc_ref[...] * val).astype(BF16)


def _ffn(XG, WG, w_gate, w_up, w_down, l, *, rows, fc):
    E, R, D = XG.shape
    F = w_gate.shape[-1]
    return pl.pallas_call(
        functools.partial(_ffn_kernel, E=E),
        grid=(E, R // rows, F // fc),
        in_specs=[pl.BlockSpec((1, rows, D), lambda e, r, f: (e, r, 0)),
                  pl.BlockSpec((1, rows, LANES), lambda e, r, f: (e, r, 0)),
                  pl.BlockSpec((1, 1, D, fc), lambda e, r, f: (l, e, 0, f)),
                  pl.BlockSpec((1, 1, D, fc), lambda e, r, f: (l, e, 0, f)),
                  pl.BlockSpec((1, 1, fc, D), lambda e, r, f: (l, e, f, 0))],
        out_specs=pl.BlockSpec((1, rows, D), lambda e, r, f: (e, r, 0)),
        out_shape=jax.ShapeDtypeStruct((E, R, D), BF16),
        scratch_shapes=[pltpu.VMEM((rows, D), F32)],
        compiler_params=_cparams(3),
        name="ec_ffn",
    )(XG, WG, w_gate, w_up, w_down)


def _combine_kernel(off_ref, y_ref, tok_ref, spread_ref, slot_ref, x_ref, g2_ref, fg_ref, o_ref, acc_ref,
                    *, E, NT, j0, tm, capt, final):
    b = pl.program_id(0)
    j = pl.program_id(1) + j0
    W = EC_WIN

    def offs(e):
        i = (b * E + e) * (NT + 1) + j
        return off_ref[i], off_ref[i + 1]

    bases = [_win_base(offs(e)[0], capt) for e in range(E)]
    col = lax.broadcasted_iota(I32, (1, E * W), 1)
    target = (col % W).astype(F32)
    for e in range(E):
        target = target + jnp.where(col // W == e, bases[e].astype(F32), 0.0)
    spread = _dot(tok_ref[...], spread_ref[...])
    lhs = jnp.where(spread == target, 1.0, 0.0).astype(BF16)
    rhs = jnp.concatenate([y_ref[e, pl.ds(bases[e], W), :] for e in range(E)], axis=0)
    acc_ref[...] = _dot(lhs, rhs)

    rows = lax.broadcasted_iota(I32, (W, tm), 0)

    def extra(e, carry):
        lo, hi = offs(e)
        base = _win_base(lo, capt)
        for w in range(1, _n_windows(tm)):
            first = base + w * W

            @pl.when(hi > first)
            def _():
                start = pl.multiple_of(jnp.minimum(first, capt - W), SLOT_ALIGN)
                s = slot_ref[0, pl.ds(e, 1), :]
                onehot_t = jnp.where(rows == jnp.where(s >= first, s - start, -1), 1.0, 0.0)
                onehot = jnp.transpose(onehot_t).astype(BF16)
                acc_ref[...] += _dot(onehot, y_ref[e, pl.ds(start, W), :])
        return carry

    @pl.when(_any_overflow(offs, bases, E))
    def _():
        lax.fori_loop(0, E, extra, 0)
    x2 = x_ref[...] + g2_ref[0] * acc_ref[...]
    if final:
        x2 = x2 * lax.rsqrt(jnp.mean(x2 * x2, axis=-1, keepdims=True) + NORM_EPS) * fg_ref[...]
    if final:
        o_ref[0] = x2
    else:
        o_ref[...] = x2


def _combine(off, Y, TOK, slot, X1, g2, fg, *, B, E, S, T, tm, capt, final):
    N, D = X1.shape
    P = S + T
    NT = P // tm
    ns = S // tm
    j0 = ns if final else 0
    own = np.kron(np.eye(E, dtype=np.float32), np.ones((1, EC_WIN), np.float32))
    spread = np.zeros((LANES, E * EC_WIN), np.float32)
    spread[ROUTE_HI * E:(ROUTE_HI + 1) * E] = 16.0 * own
    spread[ROUTE_LO * E:(ROUTE_LO + 1) * E] = own
    spread = jnp.asarray(spread, BF16)

    def row(b, j, off):
        return (b * NT + j + j0, 0)

    if final:
        out_spec = pl.BlockSpec((1, tm, D), lambda b, j, off: (b, j, 0))
        out_shape = jax.ShapeDtypeStruct((B, T, D), F32)
    else:
        out_spec = pl.BlockSpec((tm, D), row)
        out_shape = jax.ShapeDtypeStruct((N, D), F32)
    return pl.pallas_call(
        functools.partial(_combine_kernel, E=E, NT=NT, j0=j0, tm=tm, capt=capt, final=final),
        grid_spec=pltpu.PrefetchScalarGridSpec(
            num_scalar_prefetch=1,
            grid=(B, NT - j0),
            in_specs=[_resident((E, capt, D), lambda b, j, off: (0, b, 0)),
                      pl.BlockSpec((tm, LANES), row),
                      pl.BlockSpec((LANES, E * EC_WIN), lambda b, j, off: (0, 0)),
                      pl.BlockSpec((1, E, tm), lambda b, j, off: (b, 0, j + j0)),
                      pl.BlockSpec((tm, D), row),
                      pl.BlockSpec((1, 1, D), lambda b, j, off: (jnp.where(j + j0 < ns, B, b), 0, 0)),
                      pl.BlockSpec((1, D), lambda b, j, off: (0, 0))],
            out_specs=out_spec,
            scratch_shapes=[pltpu.VMEM((tm, D), F32)]),
        out_shape=out_shape,
        compiler_params=_cparams(2),
        name="ec_combine",
    )(off, Y, TOK, spread, slot, X1, g2, fg)


def _permute_w_in(w):
    D = w.shape[0]
    o = np.cumsum([0, 256, 256, 256, 128, 128, 256, 256, 16, 16, 256, 128, 128, 1024, 1024, 1024])
    qa, ka, va, qb, kb, vb, rb, lf, lb, qc, kc, vc, mga, mgb, mgc = [w[:, o[i]:o[i + 1]] for i in range(15)]
    grp = GQA_HEADS // GQA_KV_HEADS

    def rep(t):
        return jnp.concatenate([t[:, (h // grp) * HEAD_DIM:(h // grp + 1) * HEAD_DIM] for h in range(GQA_HEADS)], axis=1)

    pad = jnp.zeros((D, COL_END - COL_LR - 2 * GLA_LOWRANK), w.dtype)
    gates = 0.5 * jnp.concatenate([mga, mgb, mgc], axis=1)
    return jnp.concatenate([qa, ka, va, qb, kb, vb, rb, qc, rep(kc), rep(vc), gates, lf, lb, pad],
                           axis=1).astype(BF16)


def _prep_w_kernel(w_ref, o_ref):
    o_ref[...] = _permute_w_in(w_ref[0])


def _prep_w(w_in, l):
    L, D, C = w_in.shape
    tr = 256
    return pl.pallas_call(
        _prep_w_kernel,
        grid=(D // tr,),
        in_specs=[pl.BlockSpec((1, tr, C), lambda i: (l, i, 0))],
        out_specs=pl.BlockSpec((tr, COL_END), lambda i: (i, 0)),
        out_shape=jax.ShapeDtypeStruct((D, COL_END), BF16),
        compiler_params=_cparams(1),
        name="prep_w_in",
    )(w_in)


def _rope_tables(T, S):
    t = np.arange(T)
    row = (t // GRID_W).astype(np.float32)
    col = (t % GRID_W).astype(np.float32)
    lane = np.arange(HEAD_DIM)
    is_col = lane >= HEAD_DIM // 2
    fi = lane % 16
    inv_freq = ROPE_BASE ** (-jnp.arange(16, dtype=F32) / 16)
    pos = jnp.where(jnp.asarray(is_col)[None, :], jnp.asarray(col)[:, None], jnp.asarray(row)[:, None])
    ang = pos * inv_freq[jnp.asarray(fi)][None, :]
    cos = jnp.cos(ang)
    sin = jnp.sin(ang)
    sin = jnp.where(jnp.asarray((lane % 32) < 16)[None, :], -sin, sin)
    cos = jnp.concatenate([jnp.ones((S, HW), F32), jnp.tile(cos, (1, HW // HEAD_DIM))], axis=0)
    sin = jnp.concatenate([jnp.zeros((S, HW), F32), jnp.tile(sin, (1, HW // HEAD_DIM))], axis=0)
    return cos, sin


def kernel(x, c, ctx, c_ctx, w_ada, b_ada, norm1_g, norm2_g, w_in, na_rpb, gla_wg_f, gla_bg_f, gla_wg_b,
           gla_bg_b, gla_norm_g, gqa_qn_g, gqa_kn_g, w_branch_a, w_branch_b, w_branch_c, w_out, w_router,
           w_e_gate, w_e_up, w_e_down, final_norm_g):
    B, T, D = x.shape
    S = ctx.shape[1]
    L = w_ada.shape[0]
    E = w_router.shape[-1]
    P = S + T
    tm = min(256, S)
    NT = P // tm
    assert S % tm == 0 and T % tm == 0 and S % GRID_W == 0 and T % GRID_W == 0 and P % S == 0
    cap_l = EC_CAPACITY_FACTOR * T // E
    cap_ctx = EC_CAPACITY_FACTOR * S // E
    assert cap_l % SLOT_ALIGN == 0 and cap_ctx % SLOT_ALIGN == 0 and cap_l >= EC_WIN
    assert 5 * E <= LANES
    R = T // GRID_W
    kh = min(NA_KH, R)

    seg = jnp.asarray(np.kron(np.eye(HW // HEAD_DIM), np.ones((HEAD_DIM, HEAD_DIM))), BF16)
    cos_t, sin_t = _rope_tables(T, S)
    ii = np.arange(GLA_CHUNK)
    tril = (ii[:, None] >= ii[None, :]).astype(np.float32)
    tri4 = jnp.asarray(np.stack([np.tile(tril, (1, GLA_HEADS)), np.tile(tril.T, (1, GLA_HEADS))]))

    RP = -(-(B + 1) // 8) * 8
    cv = jnp.concatenate([c, c_ctx[None], jnp.zeros((RP - B - 1, D), F32)], axis=0)
    mod = _adaln(cv, w_ada, b_ada)[:, :B + 1].reshape(L, B + 1, 6, 1, D)

    X = jnp.concatenate([ctx, x], axis=1).reshape(B * P, D)
    out = None
    for l in range(L):
        last = l == L - 1
        sh1, sc1, g1, sh2, sc2, g2 = [mod[l, :, i] for i in range(6)]
        w_p = _prep_w(w_in, l)
        qn = jnp.tile(gqa_qn_g[l], HW // HEAD_DIM)[None]
        kn = jnp.tile(gqa_kn_g[l], HW // HEAD_DIM)[None]
        A, Bq, C, G, LR = _proj_in(X, norm1_g[l][None], sc1, sh1, w_p, seg, qn, kn, cos_t, sin_t,
                                   B=B, S=S, T=T)

        YA = _na_attention(A, _na_bias_table(na_rpb[l], kh), B=B, S=S, T=T)
        YC = _gqa_attention(C, B=B, S=S, T=T, tq=min(256, S))
        YA, YC = _ctx_attention(A, C, YA, YC, B=B, S=S, T=T)

        zpad = jnp.zeros((LANES - 2 * GLA_LOWRANK, GLA_DK), F32)
        zlr = jnp.zeros((GLA_LOWRANK, GLA_DK), F32)
        wg_pad = jnp.stack([jnp.concatenate([gla_wg_f[l], zlr, zpad], axis=0),
                            jnp.concatenate([zlr, gla_wg_b[l], zpad], axis=0)]).astype(BF16)
        bg = jnp.stack([gla_bg_f[l], gla_bg_b[l]])[:, None, :]
        O = _gla(Bq, LR, wg_pad, bg, tri4, B=B, S=S, T=T)

        gg = jnp.tile(gla_norm_g[l], GLA_HEADS)[None]
        X1, XN, AFFT = _merge(YA, O, Bq, YC, G, X,
                              w_branch_a[l].astype(BF16), w_branch_b[l].astype(BF16),
                              w_branch_c[l].astype(BF16), w_out[l].astype(BF16),
                              seg, gg, g1, norm2_g[l][None], sc2, sh2, w_router[l],
                              B=B, S=S, T=T)

        cap_c = 0 if last else cap_ctx
        capt = cap_l + cap_c
        slot, TOK, off = _select(jnp.transpose(AFFT), B=B, S=S, T=T, tm=tm, cap_c=cap_c, cap_l=cap_l)
        off_flat = off[:, :, :NT + 1].reshape(-1)
        XG, WG = _gather(off_flat, XN, TOK, slot, B=B, E=E, S=S, T=T, tm=tm, capt=capt)
        rows = capt * (2 if B % 2 == 0 else 1)
        Y = _ffn(XG, WG, w_e_gate, w_e_up, w_e_down, l, rows=rows, fc=min(1024, w_e_gate.shape[-1]))
        res = _combine(off_flat, Y, TOK, slot, X1, g2, final_norm_g[None],
                       B=B, E=E, S=S, T=T, tm=tm, capt=capt, final=last)
        if last:
            out = res
        else:
            X = res
    return out
```
